```python
import functools
import jax, jax.numpy as jnp
from jax import lax
import numpy as np

D_MODEL = 1024
BATCH = 8
SEQ = 4096
DEPTH = 1
DEC_BATCH = 128
DEC_SEQ = 1
PAST_LEN = 8192
PAGE_SIZE = 128

MIX_WIDTH = D_MODEL
POOL_WIDTH = MIX_WIDTH // 2
POOL_WINDOWS = (2, 4, 8, 16)
N_POOL_GROUPS = len(POOL_WINDOWS)
POOL_GC = POOL_WIDTH // N_POOL_GROUPS
POOL_STATE = max(POOL_WINDOWS) - 1
NSA_WIDTH = MIX_WIDTH - POOL_WIDTH
HEAD_DIM = 64
N_HEADS = NSA_WIDTH // HEAD_DIM
KV_HEADS = 2
GROUP = N_HEADS // KV_HEADS
N_BRANCH = 3
CMP_LEN = 32
CMP_STRIDE = 16
SEL_BLOCK = 64
N_SELECT = 16
WINDOW = 512
QBLK = 128
N_IN = POOL_WIDTH + NSA_WIDTH + N_BRANCH * 2 * KV_HEADS * HEAD_DIM + N_HEADS * N_BRANCH
MEM_LEN = 256
X_HEADS = 4
X_HEAD_DIM = D_MODEL // X_HEADS
X_WIDTH = X_HEADS * X_HEAD_DIM
D_FF = 2816
CONV_W = 3
EPS = 1e-6

kernel_name = "hymba_pool_nsa_convffn_decode_step"


def rmsnorm(x, g):
    xf = x.astype(jnp.float32)
    y = xf * lax.rsqrt(jnp.mean(xf * xf, axis=-1, keepdims=True) + EPS)
    return y.astype(x.dtype) * g


def masked_softmax(s, mask):
    s = jnp.where(mask, s.astype(jnp.float32), -jnp.inf)
    m = jnp.max(s, axis=-1, keepdims=True)
    m = jnp.where(jnp.isfinite(m), m, 0.0)
    e = jnp.exp(s - m)
    return e / jnp.maximum(jnp.sum(e, axis=-1, keepdims=True), 1e-30)


def mixer_inputs(x, g_mix, w_in):
    B, T, _ = x.shape
    z = rmsnorm(x, g_mix) @ w_in
    o1 = POOL_WIDTH
    o2 = o1 + NSA_WIDTH
    o3 = o2 + N_BRANCH * 2 * KV_HEADS * HEAD_DIM
    u = z[..., :o1]
    q = z[..., o1:o2].reshape(B, T, N_HEADS, HEAD_DIM)
    kv = z[..., o2:o3].reshape(B, T, N_BRANCH, 2, KV_HEADS, HEAD_DIM)
    gates = jax.nn.sigmoid(z[..., o3:]).reshape(B, T, N_HEADS, N_BRANCH)
    return u, q, kv, gates


def pool_mix(prefix, u, pos0, w_pool, pool_scale):
    B, T, C = u.shape
    P = prefix.shape[1]
    ext = jnp.concatenate([prefix, u], axis=1)
    cs = jnp.pad(jnp.cumsum(ext.astype(jnp.float32), axis=1), ((0, 0), (1, 0), (0, 0)))
    pos = pos0 + jnp.arange(T, dtype=jnp.int32)
    upto = cs[:, P + 1:P + 1 + T]
    means = []
    for gi, w in enumerate(POOL_WINDOWS):
        sl = slice(gi * POOL_GC, (gi + 1) * POOL_GC)
        win_sum = upto[..., sl] - cs[:, P + 1 - w:P + 1 - w + T, sl]
        cnt = jnp.minimum(pos + 1, w).astype(jnp.float32)[None, :, None]
        means.append(win_sum / cnt)
    d = (jnp.concatenate(means, axis=-1) - u.astype(jnp.float32)).astype(u.dtype)
    d = d.reshape(B, T, N_POOL_GROUPS, POOL_GC)
    y = jnp.einsum('btgc,gcd->btgd', d, w_pool).reshape(B, T, C) * pool_scale
    return y, ext[:, -POOL_STATE:]


def compress_blocks(rows, w_cmp, pe_cmp):
    B, T = rows.shape[:2]
    r = CMP_LEN // CMP_STRIDE
    n_chunk = T // CMP_STRIDE
    n_cmp = n_chunk - r + 1
    ch = rows[:, :n_chunk * CMP_STRIDE].reshape(B, n_chunk, CMP_STRIDE, 2, KV_HEADS, HEAD_DIM)
    w = w_cmp.reshape(2, r, CMP_STRIDE, HEAD_DIM, HEAD_DIM)
    pe = pe_cmp.reshape(2, r, CMP_STRIDE, HEAD_DIM)
    out = jnp.einsum('crsd,crsde->ce', pe, w).astype(rows.dtype)[None, None, :, None, :]
    for i in range(r):
        out = out + jnp.einsum('bnscgd,csde->bncge', ch[:, i:i + n_cmp], w[:, i])
    return out


def cmp_end_positions(n_cmp):
    return jnp.arange(n_cmp, dtype=jnp.int32) * CMP_STRIDE + (CMP_LEN - 1)


def cmp_to_sel(n_cmp, n_slc):
    cs = jnp.arange(n_cmp, dtype=jnp.int32) * CMP_STRIDE
    ss = jnp.arange(n_slc, dtype=jnp.int32) * SEL_BLOCK
    ov = (cs[:, None] + CMP_LEN - 1 >= ss[None, :]) & (cs[:, None] <= ss[None, :] + SEL_BLOCK - 1)
    return ov.astype(jnp.float32)


def sel_blocks(rows, n_slc):
    B, T = rows.shape[:2]
    rows = jnp.pad(rows, ((0, 0), (0, n_slc * SEL_BLOCK - T), (0, 0), (0, 0), (0, 0)))
    return rows.reshape(B, n_slc, SEL_BLOCK, 2, KV_HEADS, HEAD_DIM)


def nsa_core(q, gates, pos_q, kvc, cmp_end, kvs, kvw, pos_w, ovl):
    B, Tq = q.shape[:2]
    dt = q.dtype
    scale = HEAD_DIM ** -0.5
    qg = q.reshape(B, Tq, KV_HEADS, GROUP, HEAD_DIM)
    s_c = jnp.einsum('btghd,bngd->bghtn', qg, kvc[:, :, 0]) * scale
    p_c = masked_softmax(s_c, cmp_end[None, :] <= pos_q[:, None])
    o_c = jnp.einsum('bghtn,bngd->btghd', p_c.astype(dt), kvc[:, :, 1])
    n_slc = ovl.shape[1]
    imp = jnp.einsum('bghtn,nj->bgtj', p_c, ovl)
    jb = jnp.arange(n_slc, dtype=jnp.int32)[None, :]
    jq = (pos_q // SEL_BLOCK)[:, None]
    forced = (jb == 0) | (jb == jq) | (jb == jq - 1)
    imp = jnp.where(forced, jnp.inf, jnp.where(jb > jq, -jnp.inf, imp))
    k_sel = min(N_SELECT, n_slc)
    _, idx = lax.top_k(imp, k_sel)
    b_ar = jnp.arange(B)[:, None, None, None]
    g_ar = jnp.arange(KV_HEADS)[None, :, None, None]
    sel = kvs[b_ar, idx, :, :, g_ar, :]
    tok = idx[..., None] * SEL_BLOCK + jnp.arange(SEL_BLOCK, dtype=jnp.int32)
    valid = (tok <= pos_q[None, None, :, None, None]).reshape(B, KV_HEADS, 1, Tq, k_sel * SEL_BLOCK)
    s_s = jnp.einsum('btghd,bgtkrd->bghtkr', qg, sel[..., 0, :]) * scale
    p_s = masked_softmax(s_s.reshape(B, KV_HEADS, GROUP, Tq, k_sel * SEL_BLOCK), valid)
    o_s = jnp.einsum('bghtkr,bgtkrd->btghd', p_s.reshape(s_s.shape).astype(dt), sel[..., 1, :])
    s_w = jnp.einsum('btghd,bsgd->bghts', qg, kvw[:, :, 0]) * scale
    dpos = pos_q[:, None] - pos_w[None, :]
    mask_w = (dpos >= 0) & (dpos <= WINDOW) & (pos_w[None, :] >= 0)
    p_w = masked_softmax(s_w, mask_w)
    o_w = jnp.einsum('bghts,bsgd->btghd', p_w.astype(dt), kvw[:, :, 1])
    g = gates.reshape(B, Tq, KV_HEADS, GROUP, N_BRANCH)
    o = g[..., 0:1] * o_c + g[..., 1:2] * o_s + g[..., 2:3] * o_w
    return o.reshape(B, Tq, NSA_WIDTH)


def nsa_prompt(q, kv, gates, *, w_cmp, pe_cmp):
    B, T = q.shape[:2]
    kvc = compress_blocks(kv[:, :, 0], w_cmp, pe_cmp)
    n_cmp = kvc.shape[1]
    cmp_end = cmp_end_positions(n_cmp)
    n_slc = -(-T // SEL_BLOCK)
    kvs = sel_blocks(kv[:, :, 1], n_slc)
    ovl = cmp_to_sel(n_cmp, n_slc)
    kvw = jnp.pad(kv[:, :, 2], ((0, 0), (WINDOW, 0), (0, 0), (0, 0), (0, 0)))

    def one_block(bi):
        t0 = bi * QBLK
        qb = lax.dynamic_slice_in_dim(q, t0, QBLK, axis=1)
        gb = lax.dynamic_slice_in_dim(gates, t0, QBLK, axis=1)
        kwb = lax.dynamic_slice_in_dim(kvw, t0, WINDOW + QBLK, axis=1)
        pos_q = t0 + jnp.arange(QBLK, dtype=jnp.int32)
        pos_w = t0 - WINDOW + jnp.arange(WINDOW + QBLK, dtype=jnp.int32)
        return nsa_core(qb, gb, pos_q, kvc, cmp_end, kvs, kwb, pos_w, ovl)

    o = lax.map(one_block, jnp.arange(T // QBLK, dtype=jnp.int32))
    o = jnp.swapaxes(o, 0, 1).reshape(B, T, NSA_WIDTH)
    kv_rows = kv[:, :, :2].reshape(B, T, 4, KV_HEADS, HEAD_DIM)
    win_rows = kv[:, T - min(WINDOW, T):, 2]
    return o, (kv_rows, win_rows)


def nsa_sample(q, kv, gates, *, cache_kv_l, page_table, buf, w_cmp, pe_cmp):
    B, Tn = q.shape[:2]
    past = cache_kv_l[page_table]
    past = past.reshape(B, past.shape[1] * past.shape[2], 4, KV_HEADS, HEAD_DIM)
    past_len = past.shape[1]
    new4 = kv[:, :, :2].reshape(B, Tn, 4, KV_HEADS, HEAD_DIM)
    full = jnp.concatenate([past, new4], axis=1)
    T = full.shape[1]
    kvc = compress_blocks(full[:, :, 0:2], w_cmp, pe_cmp)
    n_cmp = kvc.shape[1]
    n_slc = -(-T // SEL_BLOCK)
    kvs = sel_blocks(full[:, :, 2:4], n_slc)
    kvw = jnp.concatenate([buf, kv[:, :, 2]], axis=1)
    pos_w = (past_len - buf.shape[1]) + jnp.arange(kvw.shape[1], dtype=jnp.int32)
    pos_q = past_len + jnp.arange(Tn, dtype=jnp.int32)
    o = nsa_core(q, gates, pos_q, kvc, cmp_end_positions(n_cmp), kvs, kvw, pos_w, cmp_to_sel(n_cmp, n_slc))
    new_buf = kvw[:, kvw.shape[1] - min(WINDOW, kvw.shape[1]):]
    return o, (new4, new_buf)


def mem_project(mem, g_mem, w_xkv):
    B, M, _ = mem.shape
    return (rmsnorm(mem, g_mem) @ w_xkv).reshape(B, M, 2, X_HEADS, X_HEAD_DIM)


def cross_attn(xn, mem_kv, w_xq, w_xo):
    B, T, _ = xn.shape
    q = (xn @ w_xq).reshape(B, T, X_HEADS, X_HEAD_DIM)
    s = jnp.einsum('bthd,bmhd->bhtm', q, mem_kv[:, :, 0]) * (X_HEAD_DIM ** -0.5)
    p = jax.nn.softmax(s.astype(jnp.float32), axis=-1).astype(xn.dtype)
    o = jnp.einsum('bhtm,bmhd->bthd', p, mem_kv[:, :, 1]).reshape(B, T, X_WIDTH)
    return o @ w_xo


def conv_ffn(xn, prefix, w_up, conv_w, conv_b, w_down):
    T = xn.shape[1]
    up = xn @ w_up
    ext = jnp.concatenate([prefix, up], axis=1)
    c = conv_b
    for k in range(CONV_W):
        c = c + conv_w[k] * ext[:, k:k + T]
    gate, val = jnp.split(c, 2, axis=-1)
    return (jax.nn.silu(gate) * val) @ w_down, ext[:, -(CONV_W - 1):]


def decoder_layer(h, pos0, pool_prefix, ffn_prefix, mem_kv, nsa_fn,
                  g_mix, w_in, w_pool, pool_scale, w_out, g_xattn, w_xq, w_xo,
                  g_ffn, w_up, conv_w, conv_b, w_down):
    u, q, kv, gates = mixer_inputs(h, g_mix, w_in)
    y_pool, pool_state = pool_mix(pool_prefix, u, pos0, w_pool, pool_scale)
    y_nsa, nsa_state = nsa_fn(q, kv, gates)
    h = h + jnp.concatenate([y_pool, y_nsa], axis=-1) @ w_out
    h = h + cross_attn(rmsnorm(h, g_xattn), mem_kv, w_xq, w_xo)
    y_ffn, ffn_state = conv_ffn(rmsnorm(h, g_ffn), ffn_prefix, w_up, conv_w, conv_b, w_down)
    return h + y_ffn, pool_state, nsa_state, ffn_state


def setup_inputs(seed: int = 0) -> dict:
    key = jax.random.key(seed)
    k = jax.random.split(key, 28)
    f32 = jnp.float32

    def nrm(kk, shape, scale):
        return jax.random.normal(kk, shape, f32) * scale

    def gain(kk, shape):
        return 1.0 + 0.05 * jax.random.normal(kk, shape, f32)

    n_pages = PAST_LEN // PAGE_SIZE
    n_used = DEC_BATCH * n_pages
    n_phys = n_used + n_used // 4
    page_table = jax.random.permutation(k[4], n_phys)[:n_used].reshape(DEC_BATCH, n_pages).astype(jnp.int32)
    wbuf = min(WINDOW, PAST_LEN)
    return {
        "x_prompt": nrm(k[0], (BATCH, SEQ, D_MODEL), 1.0),
        "x_sample": nrm(k[1], (DEC_BATCH, DEC_SEQ, D_MODEL), 1.0),
        "mem_prompt": nrm(k[2], (BATCH, MEM_LEN, D_MODEL), 1.0),
        "cache_kv": nrm(k[3], (DEPTH, n_phys, PAGE_SIZE, 4, KV_HEADS, HEAD_DIM), 1.0),
        "page_table": page_table,
        "cache_win": nrm(k[5], (DEPTH, DEC_BATCH, wbuf, 2, KV_HEADS, HEAD_DIM), 1.0),
        "state_pool": nrm(k[6], (DEPTH, DEC_BATCH, POOL_STATE, POOL_WIDTH), 1.0),
        "state_ffn": nrm(k[7], (DEPTH, DEC_BATCH, CONV_W - 1, 2 * D_FF), 1.0),
        "cache_mem": nrm(k[8], (DEPTH, DEC_BATCH, MEM_LEN, 2, X_HEADS, X_HEAD_DIM), 1.0),
        "g_mix": gain(k[9], (DEPTH, D_MODEL)),
        "w_in": nrm(k[10], (DEPTH, D_MODEL, N_IN), D_MODEL ** -0.5),
        "w_pool": nrm(k[11], (DEPTH, N_POOL_GROUPS, POOL_GC, POOL_GC), POOL_GC ** -0.5),
        "pool_scale": gain(k[12], (DEPTH, POOL_WIDTH)),
        "w_cmp": nrm(k[13], (DEPTH, 2, CMP_LEN, HEAD_DIM, HEAD_DIM), (CMP_LEN * HEAD_DIM) ** -0.5),
        "pe_cmp": nrm(k[14], (DEPTH, 2, CMP_LEN, HEAD_DIM), 0.1),
        "w_out": nrm(k[15], (DEPTH, MIX_WIDTH, D_MODEL), MIX_WIDTH ** -0.5),
        "g_xattn": gain(k[16], (DEPTH, D_MODEL)),
        "g_mem": gain(k[17], (DEPTH, D_MODEL)),
        "w_xq": nrm(k[18], (DEPTH, D_MODEL, X_WIDTH), D_MODEL ** -0.5),
        "w_xkv": nrm(k[19], (DEPTH, D_MODEL, 2 * X_WIDTH), D_MODEL ** -0.5),
        "w_xo": nrm(k[20], (DEPTH, X_WIDTH, D_MODEL), X_WIDTH ** -0.5),
        "g_ffn": gain(k[21], (DEPTH, D_MODEL)),
        "w_up": nrm(k[22], (DEPTH, D_MODEL, 2 * D_FF), D_MODEL ** -0.5),
        "conv_w": nrm(k[23], (DEPTH, CONV_W, 2 * D_FF), CONV_W ** -0.5),
        "conv_b": nrm(k[24], (DEPTH, 2 * D_FF), 0.01),
        "w_down": nrm(k[25], (DEPTH, D_FF, D_MODEL), D_FF ** -0.5),
        "g_final": gain(k[26], (D_MODEL,)),
    }


def reference(x_prompt, x_sample, mem_prompt, cache_kv, page_table, cache_win, state_pool, state_ffn, cache_mem,
              g_mix, w_in, w_pool, pool_scale, w_cmp, pe_cmp, w_out, g_xattn, g_mem, w_xq, w_xkv, w_xo,
              g_ffn, w_up, conv_w, conv_b, w_down, g_final):
    hp = x_prompt
    hs = x_sample
    bp = x_prompt.shape[0]
    kv_p_l, kv_s_l, win_p_l, win_s_l = [], [], [], []
    pool_p_l, pool_s_l, ffn_p_l, ffn_s_l, mem_p_l = [], [], [], [], []
    for l in range(DEPTH):
        wl = (g_mix[l], w_in[l], w_pool[l], pool_scale[l], w_out[l], g_xattn[l], w_xq[l], w_xo[l],
              g_ffn[l], w_up[l], conv_w[l], conv_b[l], w_down[l])
        mem_kv_p = mem_project(mem_prompt, g_mem[l], w_xkv[l])
        pool0 = jnp.zeros((bp, POOL_STATE, POOL_WIDTH), hp.dtype)
        ffn0 = jnp.zeros((bp, CONV_W - 1, 2 * D_FF), hp.dtype)
        nsa_p = functools.partial(nsa_prompt, w_cmp=w_cmp[l], pe_cmp=pe_cmp[l])
        hp, pool_p, (kv_p, win_p), ffn_p = decoder_layer(hp, 0, pool0, ffn0, mem_kv_p, nsa_p, *wl)
        nsa_s = functools.partial(nsa_sample, cache_kv_l=cache_kv[l], page_table=page_table,
                                  buf=cache_win[l], w_cmp=w_cmp[l], pe_cmp=pe_cmp[l])
        hs, pool_s, (kv_s, win_s), ffn_s = decoder_layer(hs, PAST_LEN, state_pool[l], state_ffn[l],
                                                         cache_mem[l], nsa_s, *wl)
        kv_p_l.append(kv_p); kv_s_l.append(kv_s); win_p_l.append(win_p); win_s_l.append(win_s)
        pool_p_l.append(pool_p); pool_s_l.append(pool_s); ffn_p_l.append(ffn_p); ffn_s_l.append(ffn_s)
        mem_p_l.append(mem_kv_p)
    y_prompt = rmsnorm(hp, g_final)
    y_sample = rmsnorm(hs, g_final)
    kv_prompt = jnp.stack(kv_p_l, 0)
    kv_sample = jnp.stack(kv_s_l, 0)
    win_prompt = jnp.stack(win_p_l, 0)
    win_sample = jnp.stack(win_s_l, 0)
    pool_prompt = jnp.stack(pool_p_l, 0)
    pool_sample = jnp.stack(pool_s_l, 0)
    ffn_prompt = jnp.stack(ffn_p_l, 0)
    ffn_sample = jnp.stack(ffn_s_l, 0)
    mem_prompt_kv = jnp.stack(mem_p_l, 0)
    return (y_prompt, y_sample, kv_prompt, kv_sample, win_prompt, win_sample,
            pool_prompt, pool_sample, ffn_prompt, ffn_sample, mem_prompt_kv)
```

```python
import functools

import jax
import jax.numpy as jnp
from jax import lax
from jax.experimental import pallas as pl
from jax.experimental.pallas import tpu as pltpu

F32 = jnp.float32
BF16 = jnp.bfloat16
I32 = jnp.int32

D_MODEL = 1024
POOL_WIDTH = 512
POOL_WINDOWS = (2, 4, 8, 16)
POOL_GC = 128
POOL_STATE = 15
NSA_WIDTH = 512
HEAD_DIM = 64
N_HEADS = 8
KV_HEADS = 2
GROUP = 4
N_BRANCH = 3
CMP_LEN = 32
CMP_STRIDE = 16
SEL_BLOCK = 64
N_SELECT = 16
WINDOW = 512
QBLK = 128
KV_W = 2 * KV_HEADS * HEAD_DIM
N_GATE = N_HEADS * N_BRANCH
LANE = 128
GATE_PAD = LANE
X_HEADS = 4
X_HEAD_DIM = 256
D_FF = 2816
EPS = 1e-6
NEG = -1e30
VMEM_LIMIT = 56 * 1024 * 1024


def _dot(a, b):
    return jnp.dot(a.astype(BF16), b.astype(BF16), preferred_element_type=F32)


def _dot_nt(a, b):
    return lax.dot_general(a.astype(BF16), b.astype(BF16), (((1,), (1,)), ((), ())),
                           preferred_element_type=F32)


def _dot_tn(a, b):
    return lax.dot_general(a.astype(BF16), b.astype(BF16), (((0,), (0,)), ((), ())),
                           preferred_element_type=F32)


def _rms(x, g):
    return x * lax.rsqrt(jnp.mean(x * x, axis=-1, keepdims=True) + EPS) * g


def _split3(x):
    p0 = x.astype(BF16)
    r = x - p0.astype(F32)
    p1 = r.astype(BF16)
    p2 = (r - p1.astype(F32)).astype(BF16)
    return p0, p1, p2


def _masked_softmax(s, mask):
    s = jnp.where(mask, s, -jnp.inf)
    m = jnp.max(s, axis=-1, keepdims=True)
    m = jnp.where(m == -jnp.inf, 0.0, m)
    e = jnp.exp(s - m)
    return e / jnp.maximum(jnp.sum(e, axis=-1, keepdims=True), 1e-30)


def _params(sem):
    return pltpu.CompilerParams(dimension_semantics=sem, vmem_limit_bytes=VMEM_LIMIT)


def _rms_proj_kernel(x_ref, g_ref, w_ref, *out_refs, widths, sigmoid_last):
    xn = _rms(x_ref[...], g_ref[...]).astype(BF16)
    off = 0
    for k, (o_ref, wd) in enumerate(zip(out_refs, widths)):
        y = jnp.dot(xn, w_ref[:, off:off + wd], preferred_element_type=F32)
        if sigmoid_last and k == len(widths) - 1:
            y = 1.0 / (1.0 + jnp.exp(-y))
        o_ref[...] = y
        off += wd


def _rms_proj(x, g, w, widths, tm, sigmoid_last=False, name="rms_proj"):
    n, d = x.shape
    ntot = w.shape[1]
    assert sum(widths) == ntot and n % tm == 0
    return pl.pallas_call(
        functools.partial(_rms_proj_kernel, widths=tuple(widths), sigmoid_last=sigmoid_last),
        grid=(n // tm,),
        in_specs=[pl.BlockSpec((tm, d), lambda i: (i, 0)),
                  pl.BlockSpec((1, d), lambda i: (0, 0)),
                  pl.BlockSpec((d, ntot), lambda i: (0, 0))],
        out_specs=[pl.BlockSpec((tm, wd), lambda i: (i, 0)) for wd in widths],
        out_shape=[jax.ShapeDtypeStruct((n, wd), F32) for wd in widths],
        compiler_params=_params(("arbitrary",)),
        name=name,
    )(x, g, w)


def _pool_tail(s_list, u, cnts, w_ref, sc_ref, y_ref, lead):
    for gi in range(len(POOL_WINDOWS)):
        cols = slice(gi * POOL_GC, (gi + 1) * POOL_GC)
        d = s_list[gi] / cnts[gi] - u[:, cols]
        y = _dot(d, w_ref[gi]) * sc_ref[:, cols]
        if lead:
            y_ref[0, :, cols] = y
        else:
            y_ref[:, cols] = y


def _pool_prompt_kernel(u_ref, w_ref, sc_ref, y_ref, ext_ref, *, tm):
    t = pl.program_id(1)
    hist = 16

    @pl.when(t == 0)
    def _():
        ext_ref[0:hist, :] = jnp.zeros((hist, POOL_WIDTH), F32)

    @pl.when(t > 0)
    def _():
        ext_ref[0:hist, :] = ext_ref[tm:tm + hist, :]

    u = u_ref[0]
    ext_ref[hist:hist + tm, :] = u
    pos = t * tm + lax.broadcasted_iota(I32, (tm, 1), 0)
    s_list, cnts = [], []
    for gi, w in enumerate(POOL_WINDOWS):
        cols = slice(gi * POOL_GC, (gi + 1) * POOL_GC)
        s = u[:, cols]
        for k in range(1, w):
            s = s + ext_ref[hist - k:hist - k + tm, cols]
        s_list.append(s)
        cnts.append(jnp.minimum(pos + 1, w).astype(F32))
    _pool_tail(s_list, u, cnts, w_ref, sc_ref, y_ref, True)


def _pool_prompt(u3, w_pool, pool_scale, tm):
    b, t, _ = u3.shape
    return pl.pallas_call(
        functools.partial(_pool_prompt_kernel, tm=tm),
        grid=(b, t // tm),
        in_specs=[pl.BlockSpec((1, tm, POOL_WIDTH), lambda i, j: (i, j, 0)),
                  pl.BlockSpec((4, POOL_GC, POOL_GC), lambda i, j: (0, 0, 0)),
                  pl.BlockSpec((1, POOL_WIDTH), lambda i, j: (0, 0))],
        out_specs=pl.BlockSpec((1, tm, POOL_WIDTH), lambda i, j: (i, j, 0)),
        out_shape=jax.ShapeDtypeStruct((b, t, POOL_WIDTH), F32),
        scratch_shapes=[pltpu.VMEM((tm + 16, POOL_WIDTH), F32)],
        compiler_params=_params(("arbitrary", "arbitrary")),
        name="pool_prompt",
    )(u3, w_pool, pool_scale)


def _pool_sample_kernel(st_ref, u_ref, w_ref, sc_ref, y_ref, *, pos0):
    u = u_ref[...]
    s_list, cnts = [], []
    for gi, w in enumerate(POOL_WINDOWS):
        cols = slice(gi * POOL_GC, (gi + 1) * POOL_GC)
        s = u[:, cols]
        for k in range(1, w):
            s = s + st_ref[POOL_STATE - k, :, cols]
        s_list.append(s)
        cnts.append(float(min(pos0 + 1, w)))
    _pool_tail(s_list, u, cnts, w_ref, sc_ref, y_ref, False)


def _pool_sample(st_t, u, w_pool, pool_scale, pos0):
    n = u.shape[0]
    return pl.pallas_call(
        functools.partial(_pool_sample_kernel, pos0=pos0),
        grid=(1,),
        in_specs=[pl.BlockSpec((POOL_STATE, n, POOL_WIDTH), lambda i: (0, 0, 0)),
                  pl.BlockSpec((n, POOL_WIDTH), lambda i: (0, 0)),
                  pl.BlockSpec((4, POOL_GC, POOL_GC), lambda i: (0, 0, 0)),
                  pl.BlockSpec((1, POOL_WIDTH), lambda i: (0, 0))],
        out_specs=pl.BlockSpec((n, POOL_WIDTH), lambda i: (0, 0)),
        out_shape=jax.ShapeDtypeStruct((n, POOL_WIDTH), F32),
        compiler_params=_params(("arbitrary",)),
        name="pool_sample",
    )(st_t, u, w_pool, pool_scale)


def _compress_rows(load_rows, wbig_ref, pe_ref, wpe_ref, n_chunk):
    acc = jnp.zeros((n_chunk, 2 * KV_W), F32)
    for s in range(CMP_STRIDE):
        acc = acc + _dot(load_rows(s), wbig_ref[s])
    pe_row = _dot(pe_ref[...], wpe_ref[...])[0:1]
    return acc[:, :KV_W] + pltpu.roll(acc[:, KV_W:], n_chunk - 1, axis=0) + pe_row


def _compress_prompt_kernel(k_ref, v_ref, wbig_ref, pe_ref, wpe_ref, out_ref, *, n_chunk):
    load = lambda s: jnp.concatenate([k_ref[0, pl.ds(s, n_chunk, stride=CMP_STRIDE), :],
                                      v_ref[0, pl.ds(s, n_chunk, stride=CMP_STRIDE), :]], axis=1)
    out_ref[0] = _compress_rows(load, wbig_ref, pe_ref, wpe_ref, n_chunk)


def _compress_prompt(kv4, wbig, pe8, wpe):
    b, t, _ = kv4.shape
    n_chunk = t // CMP_STRIDE
    return pl.pallas_call(
        functools.partial(_compress_prompt_kernel, n_chunk=n_chunk),
        grid=(b,),
        in_specs=[pl.BlockSpec((1, t, LANE), lambda i: (i, 0, 0)),
                  pl.BlockSpec((1, t, LANE), lambda i: (i, 0, 1)),
                  pl.BlockSpec(wbig.shape, lambda i: (0, 0, 0)),
                  pl.BlockSpec(pe8.shape, lambda i: (0, 0)),
                  pl.BlockSpec(wpe.shape, lambda i: (0, 0))],
        out_specs=pl.BlockSpec((1, n_chunk, KV_W), lambda i: (i, 0, 0)),
        out_shape=jax.ShapeDtypeStruct((b, n_chunk, KV_W), F32),
        compiler_params=_params(("arbitrary",)),
        name="compress_prompt",
    )(kv4, kv4, wbig, pe8, wpe)


def _stack_group_queries(q, g):
    lane = lax.broadcasted_iota(I32, (q.shape[0], LANE), 1)
    parts = []
    for hh in range(GROUP):
        h = GROUP * g + hh
        ch = q[:, (h // 2) * LANE:(h // 2 + 1) * LANE]
        if h % 2 != g:
            ch = pltpu.roll(ch, HEAD_DIM, axis=1)
        parts.append(jnp.where((lane >> 6) == g, ch, 0.0))
    return jnp.concatenate(parts, axis=0).astype(BF16)


def _place_heads(o_g, weights, g, outs):
    lane = lax.broadcasted_iota(I32, (QBLK, LANE), 1)
    for hh in range(GROUP):
        h = GROUP * g + hh
        rs = slice(hh * QBLK, (hh + 1) * QBLK)
        piece = sum(w_[:, 0:1] * o_[rs] for w_, o_ in zip(weights(h), o_g))
        if h % 2 != g:
            piece = pltpu.roll(piece, HEAD_DIM, axis=1)
        piece = jnp.where((lane >> 6) == (h % 2), piece, 0.0)
        outs[h // 2] = piece if outs[h // 2] is None else outs[h // 2] + piece


def _nsa_dense_kernel(q_ref, gt_ref, kvc_ref, kw_ref, vw_ref, part_ref, sel_ref, *, t_len):
    i = pl.program_id(1)
    t0 = i * QBLK
    nc = t_len // CMP_STRIDE
    ns = t_len // SEL_BLOCK
    wl = min(WINDOW + QBLK, t_len)
    rows = GROUP * QBLK
    q = q_ref[0] * (HEAD_DIM ** -0.5)
    gt = gt_ref[0]
    pos_q = t0 + (lax.broadcasted_iota(I32, (rows, 1), 0) & (QBLK - 1))

    oj = lax.broadcasted_iota(I32, (ns, nc), 0)
    on = lax.broadcasted_iota(I32, (ns, nc), 1)
    ovl_t = jnp.where((on * CMP_STRIDE + CMP_LEN - 1 >= oj * SEL_BLOCK)
                      & (on * CMP_STRIDE <= oj * SEL_BLOCK + SEL_BLOCK - 1), 1.0, 0.0).astype(BF16)
    jb = lax.broadcasted_iota(I32, (ns, QBLK), 0)
    jq = (t0 + lax.broadcasted_iota(I32, (ns, QBLK), 1)) >> 6
    forced = (jb == 0) | (jb == jq) | (jb == jq - 1)

    kc_all = kvc_ref[0, :, 0:LANE].astype(BF16)
    vc_all = kvc_ref[0, :, LANE:2 * LANE].astype(BF16)
    ws = pl.multiple_of(jnp.maximum(t0 - WINDOW, 0), QBLK)
    kwc = kw_ref[0, pl.ds(ws, wl), :].astype(BF16)
    vwc = vw_ref[0, pl.ds(ws, wl), :].astype(BF16)

    outs = [None] * (NSA_WIDTH // LANE)
    for g in range(KV_HEADS):
        qg = _stack_group_queries(q, g)

        s_c = _dot_nt(qg, kc_all)
        ncol = lax.broadcasted_iota(I32, (1, nc), 1)
        p_c = _masked_softmax(s_c, (ncol * CMP_STRIDE + CMP_LEN - 1) <= pos_q)
        o_c = _dot(p_c, vc_all)

        psum = p_c[0:QBLK] + p_c[QBLK:2 * QBLK] + p_c[2 * QBLK:3 * QBLK] + p_c[3 * QBLK:4 * QBLK]
        imp_t = sum(lax.dot_general(ovl_t, part, (((1,), (1,)), ((), ())), preferred_element_type=F32)
                    for part in _split3(psum))
        x = jnp.where(forced, jnp.inf, jnp.where(jb > jq, -jnp.inf, imp_t))
        cnt = jnp.zeros((ns, QBLK), I32)
        for jp in range(ns):
            row = x[jp:jp + 1, :]
            beats = (row > x) | ((row == x) & (jb > jp))
            cnt = cnt + jnp.where(beats, 1, 0)
        sel_ref[0, 0, g * ns:(g + 1) * ns, :] = jnp.where(cnt < N_SELECT, 1.0, 0.0)

        s_w = _dot_nt(qg, kwc)
        dpos = pos_q - (ws + lax.broadcasted_iota(I32, (1, wl), 1))
        p_w = _masked_softmax(s_w, (dpos >= 0) & (dpos <= WINDOW))
        o_w = _dot(p_w, vwc)

        _place_heads((o_c, o_w), lambda h: (gt[:, 3 * h:3 * h + 1], gt[:, 3 * h + 2:3 * h + 3]), g, outs)
    part_ref[0] = jnp.concatenate(outs, axis=1)


def _nsa_select_kernel(q_ref, gt_ref, sel_ref, part_ref, ks_ref, vs_ref, o_ref, *, t_len, kc):
    i = pl.program_id(1)
    t0 = i * QBLK
    ns = t_len // SEL_BLOCK
    rows = GROUP * QBLK
    q = q_ref[0] * (HEAD_DIM ** -0.5)
    gt = gt_ref[0]
    n_kchunk = (t0 + QBLK + kc - 1) // kc
    part = part_ref[0]
    outs = [part[:, c * LANE:(c + 1) * LANE] for c in range(NSA_WIDTH // LANE)]
    for g in range(KV_HEADS):
        qg = _stack_group_queries(q, g)
        sel_t = sel_ref[0, 0, g * ns:(g + 1) * ns, :].astype(BF16)

        def sel_chunk(c, carry):
            m_prev, l_prev, acc = carry
            k0 = pl.multiple_of(c * kc, kc)
            kch = ks_ref[0, pl.ds(k0, kc), :]
            vch = vs_ref[0, pl.ds(k0, kc), :]
            s = _dot_nt(qg, kch)
            ej = lax.broadcasted_iota(I32, (ns, kc), 0)
            en = lax.broadcasted_iota(I32, (ns, kc), 1)
            expand = jnp.where(((k0 + en) >> 6) == ej, 1.0, 0.0).astype(BF16)
            chosen = _dot_tn(sel_t, expand)
            key_pos = k0 + lax.broadcasted_iota(I32, (QBLK, kc), 1)
            qry_pos = t0 + lax.broadcasted_iota(I32, (QBLK, kc), 0)
            bias = jnp.where((chosen > 0.5) & (key_pos <= qry_pos), 0.0, NEG)
            s = s + jnp.concatenate([bias] * GROUP, axis=0)
            m_new = jnp.maximum(m_prev, jnp.max(s, axis=-1, keepdims=True))
            alpha = jnp.exp(m_prev - m_new)
            p = jnp.exp(s - m_new)
            l_new = alpha * l_prev + jnp.sum(p, axis=-1, keepdims=True)
            return m_new, l_new, alpha * acc + _dot(p, vch)

        _, l_s, acc_s = lax.fori_loop(
            0, n_kchunk, sel_chunk,
            (jnp.full((rows, 1), NEG, F32), jnp.zeros((rows, 1), F32), jnp.zeros((rows, LANE), F32)))
        _place_heads((acc_s / l_s,), lambda h: (gt[:, 3 * h + 1:3 * h + 2],), g, outs)
    o_ref[0] = jnp.concatenate(outs, axis=1)


def _nsa_prompt(q3, gt3, kvc, kv4, kvw, kc):
    b, t, _ = q3.shape
    assert t % kc == 0 and t % QBLK == 0
    nq = t // QBLK
    ns = t // SEL_BLOCK
    col = lambda c: (lambda i, j: (i, 0, c))
    tile = lambda w_: pl.BlockSpec((1, QBLK, w_), lambda i, j: (i, j, 0))
    sel_spec = pl.BlockSpec((1, 1, KV_HEADS * ns, QBLK), lambda i, j: (i, j, 0, 0))
    part, sel = pl.pallas_call(
        functools.partial(_nsa_dense_kernel, t_len=t),
        grid=(b, nq),
        in_specs=[tile(NSA_WIDTH), tile(GATE_PAD),
                  pl.BlockSpec((1, t // CMP_STRIDE, KV_W), lambda i, j: (i, 0, 0)),
                  pl.BlockSpec((1, t, LANE), col(0)),
                  pl.BlockSpec((1, t, LANE), col(1))],
        out_specs=[tile(NSA_WIDTH), sel_spec],
        out_shape=[jax.ShapeDtypeStruct((b, t, NSA_WIDTH), F32),
                   jax.ShapeDtypeStruct((b, nq, KV_HEADS * ns, QBLK), F32)],
        compiler_params=_params(("arbitrary", "arbitrary")),
        name="nsa_dense",
    )(q3, gt3, kvc, kvw, kvw)
    return pl.pallas_call(
        functools.partial(_nsa_select_kernel, t_len=t, kc=kc),
        grid=(b, nq),
        in_specs=[tile(NSA_WIDTH), tile(GATE_PAD), sel_spec, tile(NSA_WIDTH),
                  pl.BlockSpec((1, t, LANE), col(2)),
                  pl.BlockSpec((1, t, LANE), col(3))],
        out_specs=tile(NSA_WIDTH),
        out_shape=jax.ShapeDtypeStruct((b, t, NSA_WIDTH), F32),
        compiler_params=_params(("arbitrary", "arbitrary")),
        name="nsa_select",
    )(q3, gt3, sel, part, kv4, kv4)


def _xattn_prompt_kernel(x_ref, yp_ref, yn_ref, wo_ref, g_ref, wq_ref, mem_ref, wxo_ref, o_ref):
    y = jnp.concatenate([yp_ref[0], yn_ref[0]], axis=1)
    h1 = x_ref[0] + _dot(y, wo_ref[...])
    qx = _dot(_rms(h1, g_ref[...]), wq_ref[...]) * (X_HEAD_DIM ** -0.5)
    xw = X_HEADS * X_HEAD_DIM
    outs = []
    for h in range(X_HEADS):
        cs = slice(h * X_HEAD_DIM, (h + 1) * X_HEAD_DIM)
        s = _dot_nt(qx[:, cs], mem_ref[0, :, cs])
        e = jnp.exp(s - jnp.max(s, axis=-1, keepdims=True))
        p = e / jnp.sum(e, axis=-1, keepdims=True)
        outs.append(_dot(p, mem_ref[0, :, xw + h * X_HEAD_DIM:xw + (h + 1) * X_HEAD_DIM]))
    o_ref[0] = h1 + _dot(jnp.concatenate(outs, axis=1), wxo_ref[...])


def _xattn_prompt(x3, yp, yn, w_out, g_x, w_xq, mem_kv, w_xo, tm):
    b, t, d = x3.shape
    m = mem_kv.shape[1]
    full = lambda a: pl.BlockSpec(a.shape, lambda i, j: (0,) * a.ndim)
    return pl.pallas_call(
        _xattn_prompt_kernel,
        grid=(b, t // tm),
        in_specs=[pl.BlockSpec((1, tm, d), lambda i, j: (i, j, 0)),
                  pl.BlockSpec((1, tm, POOL_WIDTH), lambda i, j: (i, j, 0)),
                  pl.BlockSpec((1, tm, NSA_WIDTH), lambda i, j: (i, j, 0)),
                  full(w_out), full(g_x), full(w_xq),
                  pl.BlockSpec((1, m, mem_kv.shape[2]), lambda i, j: (i, 0, 0)),
                  full(w_xo)],
        out_specs=pl.BlockSpec((1, tm, d), lambda i, j: (i, j, 0)),
        out_shape=jax.ShapeDtypeStruct((b, t, d), F32),
        compiler_params=_params(("arbitrary", "arbitrary")),
        name="xattn_prompt",
    )(x3, yp, yn, w_out, g_x, w_xq, mem_kv, w_xo)


def _ffn_core(j, nj, h_ref, gf_ref, wg_ref, wv_ref, cw_ref, cb_ref, wd_ref, gfin_ref, y_ref,
              xn_ref, acc_ref, prev_rows):
    @pl.when(j == 0)
    def _():
        xn_ref[...] = _rms(h_ref[...], gf_ref[...]).astype(BF16)
        acc_ref[...] = jnp.zeros(acc_ref.shape, F32)

    xn = xn_ref[...]
    up_g = jnp.dot(xn, wg_ref[...], preferred_element_type=F32)
    up_v = jnp.dot(xn, wv_ref[...], preferred_element_type=F32)
    g1, g2, v1, v2 = prev_rows(up_g, up_v)
    cg = cb_ref[0:1, :] + cw_ref[0, 0:1, :] * g2 + cw_ref[1, 0:1, :] * g1 + cw_ref[2, 0:1, :] * up_g
    cv = cb_ref[1:2, :] + cw_ref[0, 1:2, :] * v2 + cw_ref[1, 1:2, :] * v1 + cw_ref[2, 1:2, :] * up_v
    act = cg / (1.0 + jnp.exp(-cg)) * cv
    acc_ref[...] += _dot(act, wd_ref[...])

    @pl.when(j == nj - 1)
    def _():
        y_ref[...] = _rms(h_ref[...] + acc_ref[...], gfin_ref[...])

    return up_g, up_v


def _ffn_prompt_kernel(h_ref, gf_ref, wg_ref, wv_ref, cw_ref, cb_ref, wd_ref, gfin_ref,
                       y_ref, st_ref, xn_ref, acc_ref, carry_ref, *, tiles_per_seq, nj, tm):
    r = pl.program_id(0)
    j = pl.program_id(1)
    row = lax.broadcasted_iota(I32, (tm, 1), 0)

    @pl.when((r % tiles_per_seq) == 0)
    def _():
        carry_ref[j] = jnp.zeros(carry_ref.shape[1:], F32)

    def prev_rows(up_g, up_v):
        res = []
        for half, up in enumerate((up_g, up_v)):
            prev = carry_ref[j, half]
            m1 = jnp.where(row == 0, prev[7:8], pltpu.roll(up, 1, axis=0))
            m2 = jnp.where(row == 0, prev[6:7], jnp.where(row == 1, prev[7:8], pltpu.roll(up, 2, axis=0)))
            res += [m1, m2]
        return res

    up_g, up_v = _ffn_core(j, nj, h_ref, gf_ref, wg_ref, wv_ref, cw_ref, cb_ref, wd_ref, gfin_ref,
                           y_ref, xn_ref, acc_ref, prev_rows)
    for half, up in enumerate((up_g, up_v)):
        carry_ref[j, half] = up[tm - 8:tm]
        st_ref[0, half] = up[tm - 8:tm]


def _ffn_specs(d, tf, nj):
    return [pl.BlockSpec((1, d), lambda r, j: (0, 0)),
            pl.BlockSpec((d, tf), lambda r, j: (0, j)),
            pl.BlockSpec((d, tf), lambda r, j: (0, nj + j)),
            pl.BlockSpec((3, 2, tf), lambda r, j: (0, 0, j)),
            pl.BlockSpec((2, tf), lambda r, j: (0, j)),
            pl.BlockSpec((tf, d), lambda r, j: (j, 0)),
            pl.BlockSpec((1, d), lambda r, j: (0, 0))]


def _ffn_prompt(h2, g_ffn, w_up, cw3, cb2, w_down, g_final, batch, tm, tf):
    n, d = h2.shape
    nj = D_FF // tf
    tps = n // batch // tm
    return pl.pallas_call(
        functools.partial(_ffn_prompt_kernel, tiles_per_seq=tps, nj=nj, tm=tm),
        grid=(n // tm, nj),
        in_specs=[pl.BlockSpec((tm, d), lambda r, j: (r, 0))] + _ffn_specs(d, tf, nj),
        out_specs=[pl.BlockSpec((tm, d), lambda r, j: (r, 0)),
                   pl.BlockSpec((1, 2, 8, tf), lambda r, j: (r, 0, 0, j))],
        out_shape=[jax.ShapeDtypeStruct((n, d), F32),
                   jax.ShapeDtypeStruct((n // tm, 2, 8, D_FF), F32)],
        scratch_shapes=[pltpu.VMEM((tm, d), BF16), pltpu.VMEM((tm, d), F32),
                        pltpu.VMEM((nj, 2, 8, tf), F32)],
        compiler_params=_params(("arbitrary", "arbitrary")),
        name="ffn_prompt",
    )(h2, g_ffn, w_up, w_up, cw3, cb2, w_down, g_final)


def _ffn_sample_kernel(h_ref, gf_ref, wg_ref, wv_ref, cw_ref, cb_ref, wd_ref, gfin_ref, sg_ref, sv_ref,
                       y_ref, up_ref, xn_ref, acc_ref, *, nj):
    j = pl.program_id(1)
    prev_rows = lambda up_g, up_v: (sg_ref[1], sg_ref[0], sv_ref[1], sv_ref[0])
    up_g, up_v = _ffn_core(j, nj, h_ref, gf_ref, wg_ref, wv_ref, cw_ref, cb_ref, wd_ref, gfin_ref,
                           y_ref, xn_ref, acc_ref, prev_rows)
    up_ref[0] = up_g
    up_ref[1] = up_v


def _ffn_sample(h2, g_ffn, w_up, cw3, cb2, w_down, g_final, st_t, tf):
    n, d = h2.shape
    nj = D_FF // tf
    return pl.pallas_call(
        functools.partial(_ffn_sample_kernel, nj=nj),
        grid=(1, nj),
        in_specs=[pl.BlockSpec((n, d), lambda r, j: (0, 0))] + _ffn_specs(d, tf, nj)
        + [pl.BlockSpec((2, n, tf), lambda r, j: (0, 0, j)),
           pl.BlockSpec((2, n, tf), lambda r, j: (0, 0, nj + j))],
        out_specs=[pl.BlockSpec((n, d), lambda r, j: (0, 0)),
                   pl.BlockSpec((2, n, tf), lambda r, j: (0, 0, j))],
        out_shape=[jax.ShapeDtypeStruct((n, d), F32),
                   jax.ShapeDtypeStruct((2, n, D_FF), F32)],
        scratch_shapes=[pltpu.VMEM((n, d), BF16), pltpu.VMEM((n, d), F32)],
        compiler_params=_params(("arbitrary", "arbitrary")),
        name="ffn_sample",
    )(h2, g_ffn, w_up, w_up, cw3, cb2, w_down, g_final, st_t, st_t)


def _head_rows(qrow):
    rowi = lax.broadcasted_iota(I32, (N_HEADS, LANE), 0)
    lane = lax.broadcasted_iota(I32, (N_HEADS, LANE), 1)
    x = jnp.zeros((N_HEADS, LANE), F32)
    for c in range(NSA_WIDTH // LANE):
        x = jnp.where((rowi >> 1) == c, jnp.broadcast_to(qrow[:, c * LANE:(c + 1) * LANE], (N_HEADS, LANE)), x)
    x = jnp.where((rowi & 1) != (rowi >> 2), pltpu.roll(x, HEAD_DIM, axis=1), x)
    return jnp.where((lane >> 6) == (rowi >> 2), x, 0.0)


def _page_copy(cache_ref, buf_ref, sem_ref, phys, p, slot, half):
    page = cache_ref.shape[1]
    return pltpu.make_async_copy(cache_ref.at[phys, :, pl.ds(half * LANE, LANE)],
                                 buf_ref.at[slot, half, pl.ds(p * page, page), :], sem_ref.at[slot])


def _nsa_sample_cmp_kernel(pt_ref, q_ref, cache_ref, wbig_ref, pe_ref, wpe_ref, oc_ref, imp_ref,
                           buf_ref, sem_ref, *, n_pages, n_batch, page):
    b = pl.program_id(0)
    slot = b % 2
    n_tok = n_pages * page
    n_chunk = n_tok // CMP_STRIDE

    def start_all(bb, sl):
        def body(p, _):
            phys = pt_ref[bb * n_pages + p]
            for half in range(2):
                _page_copy(cache_ref, buf_ref, sem_ref, phys, p, sl, half).start()
            return 0
        lax.fori_loop(0, n_pages, body, 0)

    @pl.when(b == 0)
    def _():
        start_all(0, 0)

    @pl.when(b + 1 < n_batch)
    def _():
        start_all(b + 1, 1 - slot)

    def wait_body(p, _):
        for half in range(2):
            _page_copy(cache_ref, buf_ref, sem_ref, 0, p, slot, half).wait()
        return 0
    lax.fori_loop(0, n_pages, wait_body, 0)

    load = lambda s: jnp.concatenate([buf_ref[slot, 0, pl.ds(s, n_chunk, stride=CMP_STRIDE), :],
                                      buf_ref[slot, 1, pl.ds(s, n_chunk, stride=CMP_STRIDE), :]], axis=1)
    kvc = _compress_rows(load, wbig_ref, pe_ref, wpe_ref, n_chunk)
    q8 = _head_rows(q_ref[0] * (HEAD_DIM ** -0.5))
    s = _dot_nt(q8, kvc[:, 0:LANE])
    ncol = lax.broadcasted_iota(I32, (1, n_chunk), 1)
    p = _masked_softmax(s, (ncol * CMP_STRIDE + CMP_LEN - 1) <= n_tok)
    oc_ref[0] = _dot(p, kvc[:, LANE:2 * LANE])
    rowi = lax.broadcasted_iota(I32, (N_HEADS, n_chunk), 0)
    psum = jnp.where(rowi == 0, jnp.sum(p[0:GROUP], axis=0, keepdims=True),
                     jnp.where(rowi == 1, jnp.sum(p[GROUP:2 * GROUP], axis=0, keepdims=True), 0.0))
    ns_pad = imp_ref.shape[2]
    on = lax.broadcasted_iota(I32, (n_chunk, ns_pad), 0)
    oj = lax.broadcasted_iota(I32, (n_chunk, ns_pad), 1)
    ovl = jnp.where((on * CMP_STRIDE + CMP_LEN - 1 >= oj * SEL_BLOCK)
                    & (on * CMP_STRIDE <= oj * SEL_BLOCK + SEL_BLOCK - 1), 1.0, 0.0).astype(BF16)
    imp_ref[0] = sum(jnp.dot(part, ovl, preferred_element_type=F32) for part in _split3(psum))


def _nsa_sample_cmp(pt_flat, q3, cache3, wbig, pe8, wpe, n_pages, ns_pad):
    nb = q3.shape[0]
    page = cache3.shape[1]
    n_tok = n_pages * page
    gs = pltpu.PrefetchScalarGridSpec(
        num_scalar_prefetch=1, grid=(nb,),
        in_specs=[pl.BlockSpec((1, 1, NSA_WIDTH), lambda i, pt: (i, 0, 0)),
                  pl.BlockSpec(memory_space=pl.ANY),
                  pl.BlockSpec(wbig.shape, lambda i, pt: (0, 0, 0)),
                  pl.BlockSpec(pe8.shape, lambda i, pt: (0, 0)),
                  pl.BlockSpec(wpe.shape, lambda i, pt: (0, 0))],
        out_specs=[pl.BlockSpec((1, N_HEADS, LANE), lambda i, pt: (i, 0, 0)),
                   pl.BlockSpec((1, N_HEADS, ns_pad), lambda i, pt: (i, 0, 0))],
        scratch_shapes=[pltpu.VMEM((2, 2, n_tok, LANE), F32), pltpu.SemaphoreType.DMA((2,))])
    return pl.pallas_call(
        functools.partial(_nsa_sample_cmp_kernel, n_pages=n_pages, n_batch=nb, page=page),
        grid_spec=gs,
        out_shape=[jax.ShapeDtypeStruct((nb, N_HEADS, LANE), F32),
                   jax.ShapeDtypeStruct((nb, N_HEADS, ns_pad), F32)],
        compiler_params=_params(("arbitrary",)),
        name="nsa_sample_cmp",
    )(pt_flat, q3, cache3, wbig, pe8, wpe)


def _topk_sample_kernel(imp_ref, idx_ref, *, jq, k_past):
    x = imp_ref[...]
    col = lax.broadcasted_iota(I32, x.shape, 1)
    x = jnp.where((col == 0) | (col == jq - 1), jnp.inf, jnp.where(col >= jq, -jnp.inf, x))
    lane = lax.broadcasted_iota(I32, idx_ref.shape, 1)
    out = jnp.zeros(idx_ref.shape, I32)
    big = jnp.int32(1 << 30)
    for k in range(k_past):
        m = jnp.max(x, axis=-1, keepdims=True)
        idx = jnp.min(jnp.where(x == m, col, big), axis=-1, keepdims=True)
        out = jnp.where(lane == k, idx, out)
        x = jnp.where(col == idx, -jnp.inf, x)
    idx_ref[...] = out


def _topk_sample(imp2, jq, k_past):
    n, w = imp2.shape
    return pl.pallas_call(
        functools.partial(_topk_sample_kernel, jq=jq, k_past=k_past),
        grid=(1,),
        in_specs=[pl.BlockSpec((n, w), lambda i: (0, 0))],
        out_specs=pl.BlockSpec((n, LANE), lambda i: (0, 0)),
        out_shape=jax.ShapeDtypeStruct((n, LANE), I32),
        compiler_params=_params(("arbitrary",)),
        name="topk_sample",
    )(imp2)


def _blk_copy(cache_ref, buf_ref, sem_ref, phys, off, r, slot):
    return pltpu.make_async_copy(cache_ref.at[phys, pl.ds(off, SEL_BLOCK), pl.ds(KV_W, KV_W)],
                                 buf_ref.at[slot, pl.ds(r * SEL_BLOCK, SEL_BLOCK), :], sem_ref.at[slot])


def _softmax_with_new(s, s_new):
    m = jnp.maximum(jnp.max(s, axis=-1, keepdims=True), s_new)
    e = jnp.exp(s - m)
    e_new = jnp.exp(s_new - m)
    return e, e_new, jnp.sum(e, axis=-1, keepdims=True) + e_new


def _nsa_sample_sel_kernel(idx_ref, pt_ref, q_ref, gt_ref, kvn_ref, oc_ref, win_ref, cache_ref,
                           o_ref, wout_ref, buf_ref, sem_ref, *, n_pages, n_batch, page, k_past):
    b = pl.program_id(0)
    slot = b % 2
    n_blk = KV_HEADS * k_past
    per_page = page // SEL_BLOCK

    def start_all(bb, sl):
        def body(r, _):
            j = idx_ref[bb * n_blk + r]
            phys = pt_ref[bb * n_pages + j // per_page]
            _blk_copy(cache_ref, buf_ref, sem_ref, phys, (j % per_page) * SEL_BLOCK, r, sl).start()
            return 0
        lax.fori_loop(0, n_blk, body, 0)

    @pl.when(b == 0)
    def _():
        start_all(0, 0)

    @pl.when(b + 1 < n_batch)
    def _():
        start_all(b + 1, 1 - slot)

    def wait_body(r, _):
        _blk_copy(cache_ref, buf_ref, sem_ref, 0, 0, r, slot).wait()
        return 0
    lax.fori_loop(0, n_blk, wait_body, 0)

    q8 = _head_rows(q_ref[0] * (HEAD_DIM ** -0.5))
    rowi = lax.broadcasted_iota(I32, (N_HEADS, 1), 0)
    lane = lax.broadcasted_iota(I32, (N_HEADS, LANE), 1)
    own = (lane >> 6) == (rowi >> 2)
    kvn = kvn_ref[0]
    nk = k_past * SEL_BLOCK

    def new_score(kcols):
        return jnp.sum(q8 * kvn[:, kcols], axis=-1, keepdims=True)

    def pick(a0, a1):
        return jnp.where(rowi < GROUP, a0, a1)

    s = pick(_dot_nt(q8, buf_ref[slot, 0:nk, 0:LANE]), _dot_nt(q8, buf_ref[slot, nk:2 * nk, 0:LANE]))
    e, e_new, l = _softmax_with_new(s, new_score(slice(0, LANE)))
    o_s = (pick(_dot(e, buf_ref[slot, 0:nk, LANE:2 * LANE]), _dot(e, buf_ref[slot, nk:2 * nk, LANE:2 * LANE]))
           + e_new * kvn[:, LANE:2 * LANE]) / l

    cw = win_ref[0]
    s = _dot_nt(q8, cw[:, 0:LANE])
    e, e_new, l = _softmax_with_new(s, new_score(slice(2 * LANE, 3 * LANE)))
    o_w = (_dot(e, cw[:, LANE:2 * LANE]) + e_new * kvn[:, 3 * LANE:4 * LANE]) / l
    wrow = lax.broadcasted_iota(I32, (cw.shape[0], 1), 0)
    wout_ref[0] = jnp.where(wrow == cw.shape[0] - 1, kvn[:, 2 * LANE:4 * LANE],
                            pltpu.roll(cw, cw.shape[0] - 1, axis=0))

    gt = jnp.broadcast_to(gt_ref[0], (N_HEADS, GATE_PAD))
    gate = lambda br: jnp.sum(jnp.where(lane == rowi * N_BRANCH + br, gt, 0.0), axis=-1, keepdims=True)
    o = gate(0) * oc_ref[0] + gate(1) * o_s + gate(2) * o_w
    o_ref[0] = jnp.where(own, o, 0.0)


def _nsa_sample_sel(idx_flat, pt_flat, q3, gt3, kvn3, oc, win3, cache3, n_pages, k_past):
    nb = q3.shape[0]
    page = cache3.shape[1]
    wlen = win3.shape[1]
    n_blk = KV_HEADS * k_past
    gs = pltpu.PrefetchScalarGridSpec(
        num_scalar_prefetch=2, grid=(nb,),
        in_specs=[pl.BlockSpec((1, 1, NSA_WIDTH), lambda i, a, c: (i, 0, 0)),
                  pl.BlockSpec((1, 1, GATE_PAD), lambda i, a, c: (i, 0, 0)),
                  pl.BlockSpec((1, 1, 2 * KV_W), lambda i, a, c: (i, 0, 0)),
                  pl.BlockSpec((1, N_HEADS, LANE), lambda i, a, c: (i, 0, 0)),
                  pl.BlockSpec((1, wlen, KV_W), lambda i, a, c: (i, 0, 0)),
                  pl.BlockSpec(memory_space=pl.ANY)],
        out_specs=[pl.BlockSpec((1, N_HEADS, LANE), lambda i, a, c: (i, 0, 0)),
                   pl.BlockSpec((1, wlen, KV_W), lambda i, a, c: (i, 0, 0))],
        scratch_shapes=[pltpu.VMEM((2, n_blk * SEL_BLOCK, KV_W), F32), pltpu.SemaphoreType.DMA((2,))])
    return pl.pallas_call(
        functools.partial(_nsa_sample_sel_kernel, n_pages=n_pages, n_batch=nb, page=page, k_past=k_past),
        grid_spec=gs,
        out_shape=[jax.ShapeDtypeStruct((nb, N_HEADS, LANE), F32),
                   jax.ShapeDtypeStruct((nb, wlen, KV_W), F32)],
        compiler_params=_params(("arbitrary",)),
        name="nsa_sample_sel",
    )(idx_flat, pt_flat, q3, gt3, kvn3, oc, win3, cache3)


def _mix_sample_kernel(x_ref, yp_ref, yn_ref, wo_ref, g_ref, wq_ref, h1_ref, qx_ref):
    y = jnp.concatenate([yp_ref[...], yn_ref[...]], axis=1)
    h1 = x_ref[...] + _dot(y, wo_ref[...])
    h1_ref[...] = h1
    qx_ref[...] = _dot(_rms(h1, g_ref[...]), wq_ref[...]) * (X_HEAD_DIM ** -0.5)


def _mix_sample(x, yp, yn, w_out, g_x, w_xq):
    n, d = x.shape
    full = lambda a: pl.BlockSpec(a.shape, lambda i: (0,) * a.ndim)
    args = (x, yp, yn, w_out, g_x, w_xq)
    return pl.pallas_call(
        _mix_sample_kernel, grid=(1,),
        in_specs=[full(a) for a in args],
        out_specs=[pl.BlockSpec((n, d), lambda i: (0, 0))] * 2,
        out_shape=[jax.ShapeDtypeStruct((n, d), F32)] * 2,
        compiler_params=_params(("arbitrary",)),
        name="mix_sample",
    )(*args)


def _xattn_sample_kernel(qx_ref, mem_ref, o_ref):
    xw = X_HEADS * X_HEAD_DIM
    rowi = lax.broadcasted_iota(I32, (8, xw), 0)
    lane = lax.broadcasted_iota(I32, (8, xw), 1)
    own = (lane >> 8) == rowi
    q8 = jnp.where(own, jnp.broadcast_to(qx_ref[0], (8, xw)), 0.0)
    s = _dot_nt(q8, mem_ref[0, :, 0:xw])
    e = jnp.exp(s - jnp.max(s, axis=-1, keepdims=True))
    p = e / jnp.sum(e, axis=-1, keepdims=True)
    o = _dot(p, mem_ref[0, :, xw:2 * xw])
    o_ref[0] = jnp.sum(jnp.where(own, o, 0.0), axis=0, keepdims=True)


def _xattn_sample(qx3, mem3):
    nb, _, xw = qx3.shape
    m = mem3.shape[1]
    return pl.pallas_call(
        _xattn_sample_kernel, grid=(nb,),
        in_specs=[pl.BlockSpec((1, 1, xw), lambda i: (i, 0, 0)),
                  pl.BlockSpec((1, m, 2 * xw), lambda i: (i, 0, 0))],
        out_specs=pl.BlockSpec((1, 1, xw), lambda i: (i, 0, 0)),
        out_shape=jax.ShapeDtypeStruct((nb, 1, xw), F32),
        compiler_params=_params(("arbitrary",)),
        name="xattn_sample",
    )(qx3, mem3)


def _resid_proj_kernel(h_ref, o_ref, w_ref, y_ref):
    y_ref[...] = h_ref[...] + _dot(o_ref[...], w_ref[...])


def _resid_proj(h, o, w):
    n, d = h.shape
    full = lambda a: pl.BlockSpec(a.shape, lambda i: (0,) * a.ndim)
    return pl.pallas_call(
        _resid_proj_kernel, grid=(1,),
        in_specs=[full(h), full(o), full(w)],
        out_specs=pl.BlockSpec((n, d), lambda i: (0, 0)),
        out_shape=jax.ShapeDtypeStruct((n, d), F32),
        compiler_params=_params(("arbitrary",)),
        name="resid_proj",
    )(h, o, w)


def _prep_weights(g_mix, w_in, w_pool, pool_scale, w_cmp, pe_cmp, w_out, g_xattn, g_mem, w_xq, w_xkv,
                  w_xo, g_ffn, w_up, conv_w, conv_b, w_down, g_final):
    n_in = w_in.shape[1]
    pad = 2 * POOL_WIDTH + N_BRANCH * KV_W + GATE_PAD - n_in
    r = CMP_LEN // CMP_STRIDE
    w5 = w_cmp.reshape(2, r, CMP_STRIDE, HEAD_DIM, HEAD_DIM)
    eye = jnp.eye(2, dtype=F32)
    wbig = jnp.einsum('cisde,cC,gG->scgdiCGe', w5, eye, eye).reshape(CMP_STRIDE, KV_W, r * KV_W)
    wpe = jnp.einsum('cke,cC->ckCe', w_cmp.reshape(2, CMP_LEN * HEAD_DIM, HEAD_DIM), eye)
    wpe = jnp.broadcast_to(wpe[:, :, :, None, :], (2, CMP_LEN * HEAD_DIM, 2, KV_HEADS, HEAD_DIM))
    row = lambda v: v.reshape(1, -1)
    return dict(
        g_mix=row(g_mix), w_in=jnp.pad(w_in, ((0, 0), (0, pad))).astype(BF16),
        w_pool=w_pool.astype(BF16), pool_scale=row(pool_scale),
        wbig=wbig.astype(BF16), wpe=wpe.reshape(2 * CMP_LEN * HEAD_DIM, KV_W).astype(BF16),
        pe8=jnp.broadcast_to(pe_cmp.reshape(1, -1), (8, 2 * CMP_LEN * HEAD_DIM)),
        w_out=w_out.astype(BF16), g_xattn=row(g_xattn), g_mem=row(g_mem), w_xq=w_xq.astype(BF16),
        w_xkv=w_xkv.astype(BF16), w_xo=w_xo.astype(BF16), g_ffn=row(g_ffn), w_up=w_up.astype(BF16),
        cw3=conv_w.reshape(3, 2, D_FF), cb2=conv_b.reshape(2, D_FF), w_down=w_down.astype(BF16),
        g_final=row(g_final))


_MIX_WIDTHS = (POOL_WIDTH, NSA_WIDTH, 2 * KV_W, KV_W, GATE_PAD)


def _prompt_layer(x, mem, w, *, tm=512, tf=1408, kc=512):
    b, t, d = x.shape
    n = b * t
    u, q, kv4, kvw, gt = _rms_proj(x.reshape(n, d), w['g_mix'], w['w_in'], _MIX_WIDTHS, tm,
                                   sigmoid_last=True, name="mix_in_prompt")
    r3 = lambda a: a.reshape(b, t, a.shape[-1])
    u, q, kv4, kvw, gt = map(r3, (u, q, kv4, kvw, gt))
    y_pool = _pool_prompt(u, w['w_pool'], w['pool_scale'], tm)
    kvc = _compress_prompt(kv4, w['wbig'], w['pe8'], w['wpe'])
    y_nsa = _nsa_prompt(q, gt, kvc, kv4, kvw, kc)
    m = mem.shape[1]
    (mem_kv,) = _rms_proj(mem.reshape(b * m, d), w['g_mem'], w['w_xkv'], (2 * X_HEADS * X_HEAD_DIM,), m,
                          name="mem_project")
    mem_kv = mem_kv.reshape(b, m, 2 * X_HEADS * X_HEAD_DIM)
    h2 = _xattn_prompt(x, y_pool, y_nsa, w['w_out'], w['g_xattn'], w['w_xq'], mem_kv, w['w_xo'], tm)
    y, st = _ffn_prompt(h2.reshape(n, d), w['g_ffn'], w['w_up'], w['cw3'], w['cb2'], w['w_down'],
                        w['g_final'], b, tm, tf)
    wlen = min(WINDOW, t)
    return (y.reshape(b, t, d),
            kv4.reshape(b, t, 4, KV_HEADS, HEAD_DIM),
            kvw[:, t - wlen:].reshape(b, wlen, 2, KV_HEADS, HEAD_DIM),
            u[:, t - POOL_STATE:],
            jnp.swapaxes(st[t // tm - 1::t // tm, :, 6:8], 1, 2).reshape(b, 2, 2 * D_FF),
            mem_kv.reshape(b, m, 2, X_HEADS, X_HEAD_DIM))


def _sample_layer(x, cache_kv, page_table, cache_win, state_pool, state_ffn, cache_mem, w, *, tf=1408):
    nb, _, d = x.shape
    n_phys, page = cache_kv.shape[:2]
    n_pages = page_table.shape[1]
    past = n_pages * page
    u, q, kv4, kvw, gt = _rms_proj(x.reshape(nb, d), w['g_mix'], w['w_in'], _MIX_WIDTHS, nb,
                                   sigmoid_last=True, name="mix_in_sample")
    y_pool = _pool_sample(jnp.swapaxes(state_pool, 0, 1), u, w['w_pool'], w['pool_scale'], past)
    cache3 = cache_kv.reshape(n_phys, page, 2 * KV_W)
    pt_flat = page_table.reshape(-1)
    jq = past // SEL_BLOCK
    ns_pad = -(-(jq + 1) // LANE) * LANE
    k_past = min(N_SELECT, jq + 1) - 1
    q3 = q.reshape(nb, 1, NSA_WIDTH)
    oc, imp = _nsa_sample_cmp(pt_flat, q3, cache3, w['wbig'], w['pe8'], w['wpe'], n_pages, ns_pad)
    idx = _topk_sample(imp[:, :KV_HEADS].reshape(nb * KV_HEADS, ns_pad), jq, k_past)
    idx_flat = idx[:, :k_past].reshape(-1)
    kvn3 = jnp.concatenate([kv4[:, KV_W:], kvw], axis=1).reshape(nb, 1, 2 * KV_W)
    win3 = cache_win.reshape(nb, cache_win.shape[1], KV_W)
    o8, win_out = _nsa_sample_sel(idx_flat, pt_flat, q3, gt.reshape(nb, 1, GATE_PAD), kvn3, oc, win3,
                                  cache3, n_pages, k_past)
    y_nsa = jnp.concatenate([o8[:, :GROUP, :HEAD_DIM].reshape(nb, GROUP * HEAD_DIM),
                             o8[:, GROUP:, HEAD_DIM:].reshape(nb, GROUP * HEAD_DIM)], axis=1)
    h1, qx = _mix_sample(x.reshape(nb, d), y_pool, y_nsa, w['w_out'], w['g_xattn'], w['w_xq'])
    m = cache_mem.shape[1]
    ox = _xattn_sample(qx.reshape(nb, 1, d), cache_mem.reshape(nb, m, 2 * X_HEADS * X_HEAD_DIM))
    h2 = _resid_proj(h1, ox.reshape(nb, d), w['w_xo'])
    y, up2 = _ffn_sample(h2, w['g_ffn'], w['w_up'], w['cw3'], w['cb2'], w['w_down'], w['g_final'],
                         jnp.swapaxes(state_ffn, 0, 1), tf)
    up = jnp.swapaxes(up2, 0, 1).reshape(nb, 1, 2 * D_FF)
    return (y.reshape(nb, 1, d),
            kv4.reshape(nb, 1, 4, KV_HEADS, HEAD_DIM),
            win_out.reshape(nb, win_out.shape[1], 2, KV_HEADS, HEAD_DIM),
            jnp.concatenate([state_pool[:, 1:], u[:, None, :]], axis=1),
            jnp.concatenate([state_ffn[:, 1:], up], axis=1),
            )


def kernel(x_prompt, x_sample, mem_prompt, cache_kv, page_table, cache_win, state_pool, state_ffn, cache_mem, g_mix, w_in, w_pool, pool_scale, w_cmp, pe_cmp, w_out, g_xattn, g_mem, w_xq, w_xkv, w_xo, g_ffn, w_up, conv_w, conv_b, w_down, g_final):
    assert g_mix.shape[0] == 1, "single layer"
    w = _prep_weights(g_mix[0], w_in[0], w_pool[0], pool_scale[0], w_cmp[0], pe_cmp[0], w_out[0], g_xattn[0],
                      g_mem[0], w_xq[0], w_xkv[0], w_xo[0], g_ffn[0], w_up[0], conv_w[0], conv_b[0], w_down[0],
                      g_final)
    yp, kv_p, win_p, pool_p, ffn_p, mem_p = _prompt_layer(x_prompt, mem_prompt, w)
    ys, kv_s, win_s, pool_s, ffn_s = _sample_layer(x_sample, cache_kv[0], page_table, cache_win[0],
                                                    state_pool[0], state_ffn[0], cache_mem[0], w)
    lead = lambda a: a[None]
    return (yp, ys, lead(kv_p), lead(kv_s), lead(win_p), lead(win_s), lead(pool_p), lead(pool_s),
            lead(ffn_p), lead(ffn_s), lead(mem_p))
```

```python
import functools

import jax
import jax.numpy as jnp
from jax import lax
from jax.experimental import pallas as pl
from jax.experimental.pallas import tpu as pltpu

F32 = jnp.float32
BF16 = jnp.bfloat16
I32 = jnp.int32

D_MODEL = 1024
POOL_WIDTH = 512
POOL_WINDOWS = (2, 4, 8, 16)
POOL_GC = 128
POOL_STATE = 15
NSA_WIDTH = 512
HEAD_DIM = 64
N_HEADS = 8
KV_HEADS = 2
GROUP = 4
N_BRANCH = 3
CMP_LEN = 32
CMP_STRIDE = 16
SEL_BLOCK = 64
N_SELECT = 16
WINDOW = 512
QBLK = 128
KV_W = 2 * KV_HEADS * HEAD_DIM
N_GATE = N_HEADS * N_BRANCH
LANE = 128
GATE_PAD = LANE
X_HEADS = 4
X_HEAD_DIM = 256
D_FF = 2816
EPS = 1e-6
NEG = -1e30
VMEM_LIMIT = 56 * 1024 * 1024


def _dot(a, b):
    return jnp.dot(a.astype(BF16), b.astype(BF16), preferred_element_type=F32)


def _dot_nt(a, b):
    return lax.dot_general(a.astype(BF16), b.astype(BF16), (((1,), (1,)), ((), ())),
                           preferred_element_type=F32)


def _dot_tn(a, b):
    return lax.dot_general(a.astype(BF16), b.astype(BF16), (((0,), (0,)), ((), ())),
                           preferred_element_type=F32)


def _rms(x, g):
    return x * lax.rsqrt(jnp.mean(x * x, axis=-1, keepdims=True) + EPS) * g


def _split3(x):
    p0 = x.astype(BF16)
    r = x - p0.astype(F32)
    p1 = r.astype(BF16)
    p2 = (r - p1.astype(F32)).astype(BF16)
    return p0, p1, p2


def _masked_softmax(s, mask):
    s = jnp.where(mask, s, -jnp.inf)
    m = jnp.max(s, axis=-1, keepdims=True)
    m = jnp.where(m == -jnp.inf, 0.0, m)
    e = jnp.exp(s - m)
    return e / jnp.maximum(jnp.sum(e, axis=-1, keepdims=True), 1e-30)


def _params(sem):
    return pltpu.CompilerParams(dimension_semantics=sem, vmem_limit_bytes=VMEM_LIMIT)


def _rms_proj_kernel(x_ref, g_ref, w_ref, *out_refs, widths, sigmoid_last):
    xn = _rms(x_ref[...], g_ref[...]).astype(BF16)
    off = 0
    for k, (o_ref, wd) in enumerate(zip(out_refs, widths)):
        y = jnp.dot(xn, w_ref[:, off:off + wd], preferred_element_type=F32)
        if sigmoid_last and k == len(widths) - 1:
            y = 1.0 / (1.0 + jnp.exp(-y))
        o_ref[...] = y
        off += wd


def _rms_proj(x, g, w, widths, tm, sigmoid_last=False, name="rms_proj"):
    n, d = x.shape
    ntot = w.shape[1]
    assert sum(widths) == ntot and n % tm == 0
    return pl.pallas_call(
        functools.partial(_rms_proj_kernel, widths=tuple(widths), sigmoid_last=sigmoid_last),
        grid=(n // tm,),
        in_specs=[pl.BlockSpec((tm, d), lambda i: (i, 0)),
                  pl.BlockSpec((1, d), lambda i: (0, 0)),
                  pl.BlockSpec((d, ntot), lambda i: (0, 0))],
        out_specs=[pl.BlockSpec((tm, wd), lambda i: (i, 0)) for wd in widths],
        out_shape=[jax.ShapeDtypeStruct((n, wd), F32) for wd in widths],
        compiler_params=_params(("arbitrary",)),
        name=name,
    )(x, g, w)


def _mix_in_kernel(x_ref, g_ref, wt_ref, wf_ref, u_ref, q_ref, kc_ref, gt_ref, kv4_ref, kvw_ref):
    xn = _rms(x_ref[...], g_ref[...]).astype(BF16)
    off = 0
    for o_ref in (u_ref, q_ref, kc_ref, gt_ref):
        wd = o_ref.shape[-1]
        y = jnp.dot(xn, wt_ref[:, off:off + wd], preferred_element_type=F32)
        o_ref[...] = 1.0 / (1.0 + jnp.exp(-y)) if o_ref is gt_ref else y
        off += wd
    kv_t = lax.dot_general(wf_ref[...], xn, (((1,), (1,)), ((), ())), preferred_element_type=F32)
    kv4_ref[0] = kv_t[0:2 * KV_W]
    kvw_ref[0] = kv_t[2 * KV_W:3 * KV_W]


def _mix_in(x3, g, w_tok, w_feat, tm, name):
    b, t, d = x3.shape
    n = b * t
    tps = t // tm
    widths = (POOL_WIDTH, NSA_WIDTH, KV_W, GATE_PAD)
    assert sum(widths) == w_tok.shape[1] and t % tm == 0
    return pl.pallas_call(
        _mix_in_kernel,
        grid=(n // tm,),
        in_specs=[pl.BlockSpec((tm, d), lambda i: (i, 0)),
                  pl.BlockSpec((1, d), lambda i: (0, 0)),
                  pl.BlockSpec(w_tok.shape, lambda i: (0, 0)),
                  pl.BlockSpec(w_feat.shape, lambda i: (0, 0))],
        out_specs=[pl.BlockSpec((tm, wd), lambda i: (i, 0)) for wd in widths]
        + [pl.BlockSpec((1, 2 * KV_W, tm), lambda i: (i // tps, 0, i % tps)),
           pl.BlockSpec((1, KV_W, tm), lambda i: (i // tps, 0, i % tps))],
        out_shape=[jax.ShapeDtypeStruct((n, wd), F32) for wd in widths]
        + [jax.ShapeDtypeStruct((b, 2 * KV_W, t), F32), jax.ShapeDtypeStruct((b, KV_W, t), F32)],
        compiler_params=_params(("arbitrary",)),
        name=name,
    )(x3.reshape(n, d), g, w_tok, w_feat)


def _pool_tail(s_list, u, cnts, w_ref, sc_ref, y_ref, lead):
    for gi in range(len(POOL_WINDOWS)):
        cols = slice(gi * POOL_GC, (gi + 1) * POOL_GC)
        d = s_list[gi] / cnts[gi] - u[:, cols]
        y = _dot(d, w_ref[gi]) * sc_ref[:, cols]
        if lead:
            y_ref[0, :, cols] = y
        else:
            y_ref[:, cols] = y


def _pool_prompt_kernel(u_ref, w_ref, sc_ref, y_ref, ext_ref, *, tm):
    t = pl.program_id(1)
    hist = 16

    @pl.when(t == 0)
    def _():
        ext_ref[0:hist, :] = jnp.zeros((hist, POOL_WIDTH), F32)

    @pl.when(t > 0)
    def _():
        ext_ref[0:hist, :] = ext_ref[tm:tm + hist, :]

    u = u_ref[0]
    ext_ref[hist:hist + tm, :] = u
    pos = t * tm + lax.broadcasted_iota(I32, (tm, 1), 0)
    s_list, cnts = [], []
    for gi, w in enumerate(POOL_WINDOWS):
        cols = slice(gi * POOL_GC, (gi + 1) * POOL_GC)
        s = u[:, cols]
        for k in range(1, w):
            s = s + ext_ref[hist - k:hist - k + tm, cols]
        s_list.append(s)
        cnts.append(jnp.minimum(pos + 1, w).astype(F32))
    _pool_tail(s_list, u, cnts, w_ref, sc_ref, y_ref, True)


def _pool_prompt(u3, w_pool, pool_scale, tm):
    b, t, _ = u3.shape
    return pl.pallas_call(
        functools.partial(_pool_prompt_kernel, tm=tm),
        grid=(b, t // tm),
        in_specs=[pl.BlockSpec((1, tm, POOL_WIDTH), lambda i, j: (i, j, 0)),
                  pl.BlockSpec((4, POOL_GC, POOL_GC), lambda i, j: (0, 0, 0)),
                  pl.BlockSpec((1, POOL_WIDTH), lambda i, j: (0, 0))],
        out_specs=pl.BlockSpec((1, tm, POOL_WIDTH), lambda i, j: (i, j, 0)),
        out_shape=jax.ShapeDtypeStruct((b, t, POOL_WIDTH), F32),
        scratch_shapes=[pltpu.VMEM((tm + 16, POOL_WIDTH), F32)],
        compiler_params=_params(("arbitrary", "arbitrary")),
        name="pool_prompt",
    )(u3, w_pool, pool_scale)


def _pool_sample_kernel(st_ref, u_ref, w_ref, sc_ref, y_ref, *, pos0):
    u = u_ref[...]
    s_list, cnts = [], []
    for gi, w in enumerate(POOL_WINDOWS):
        cols = slice(gi * POOL_GC, (gi + 1) * POOL_GC)
        s = u[:, cols]
        for k in range(1, w):
            s = s + st_ref[POOL_STATE - k, :, cols]
        s_list.append(s)
        cnts.append(float(min(pos0 + 1, w)))
    _pool_tail(s_list, u, cnts, w_ref, sc_ref, y_ref, False)


def _pool_sample(st_t, u, w_pool, pool_scale, pos0):
    n = u.shape[0]
    return pl.pallas_call(
        functools.partial(_pool_sample_kernel, pos0=pos0),
        grid=(1,),
        in_specs=[pl.BlockSpec((POOL_STATE, n, POOL_WIDTH), lambda i: (0, 0, 0)),
                  pl.BlockSpec((n, POOL_WIDTH), lambda i: (0, 0)),
                  pl.BlockSpec((4, POOL_GC, POOL_GC), lambda i: (0, 0, 0)),
                  pl.BlockSpec((1, POOL_WIDTH), lambda i: (0, 0))],
        out_specs=pl.BlockSpec((n, POOL_WIDTH), lambda i: (0, 0)),
        out_shape=jax.ShapeDtypeStruct((n, POOL_WIDTH), F32),
        compiler_params=_params(("arbitrary",)),
        name="pool_sample",
    )(st_t, u, w_pool, pool_scale)


def _compress_rows(load_rows, wbig_ref, pe_ref, wpe_ref, n_chunk):
    acc = jnp.zeros((n_chunk, 2 * KV_W), F32)
    for s in range(CMP_STRIDE):
        acc = acc + _dot(load_rows(s), wbig_ref[s])
    pe_row = _dot(pe_ref[...], wpe_ref[...])[0:1]
    return acc[:, :KV_W] + pltpu.roll(acc[:, KV_W:], n_chunk - 1, axis=0) + pe_row


def _compress_prompt_kernel(k_ref, v_ref, wbig_ref, pe_ref, wpe_ref, out_ref, *, n_chunk):
    load = lambda s: jnp.concatenate([k_ref[0, pl.ds(s, n_chunk, stride=CMP_STRIDE), :],
                                      v_ref[0, pl.ds(s, n_chunk, stride=CMP_STRIDE), :]], axis=1)
    out_ref[0] = _compress_rows(load, wbig_ref, pe_ref, wpe_ref, n_chunk)


def _compress_prompt(kv4, wbig, pe8, wpe):
    b, t, _ = kv4.shape
    n_chunk = t // CMP_STRIDE
    return pl.pallas_call(
        functools.partial(_compress_prompt_kernel, n_chunk=n_chunk),
        grid=(b,),
        in_specs=[pl.BlockSpec((1, t, LANE), lambda i: (i, 0, 0)),
                  pl.BlockSpec((1, t, LANE), lambda i: (i, 0, 1)),
                  pl.BlockSpec(wbig.shape, lambda i: (0, 0, 0)),
                  pl.BlockSpec(pe8.shape, lambda i: (0, 0)),
                  pl.BlockSpec(wpe.shape, lambda i: (0, 0))],
        out_specs=pl.BlockSpec((1, n_chunk, KV_W), lambda i: (i, 0, 0)),
        out_shape=jax.ShapeDtypeStruct((b, n_chunk, KV_W), F32),
        compiler_params=_params(("arbitrary",)),
        name="compress_prompt",
    )(kv4, kv4, wbig, pe8, wpe)


def _stack_group_queries(q, g):
    lane = lax.broadcasted_iota(I32, (q.shape[0], LANE), 1)
    parts = []
    for hh in range(GROUP):
        h = GROUP * g + hh
        ch = q[:, (h // 2) * LANE:(h // 2 + 1) * LANE]
        if h % 2 != g:
            ch = pltpu.roll(ch, HEAD_DIM, axis=1)
        parts.append(jnp.where((lane >> 6) == g, ch, 0.0))
    return jnp.concatenate(parts, axis=0).astype(BF16)


def _place_heads(o_g, weights, g, outs):
    lane = lax.broadcasted_iota(I32, (QBLK, LANE), 1)
    for hh in range(GROUP):
        h = GROUP * g + hh
        rs = slice(hh * QBLK, (hh + 1) * QBLK)
        piece = sum(w_[:, 0:1] * o_[rs] for w_, o_ in zip(weights(h), o_g))
        if h % 2 != g:
            piece = pltpu.roll(piece, HEAD_DIM, axis=1)
        piece = jnp.where((lane >> 6) == (h % 2), piece, 0.0)
        outs[h // 2] = piece if outs[h // 2] is None else outs[h // 2] + piece


def _nsa_dense_kernel(q_ref, gt_ref, kvc_ref, kw_ref, vw_ref, part_ref, sel_ref, *, t_len):
    i = pl.program_id(1)
    t0 = i * QBLK
    nc = t_len // CMP_STRIDE
    ns = t_len // SEL_BLOCK
    wl = min(WINDOW + QBLK, t_len)
    rows = GROUP * QBLK
    q = q_ref[0] * (HEAD_DIM ** -0.5)
    gt = gt_ref[0]
    pos_q = t0 + (lax.broadcasted_iota(I32, (rows, 1), 0) & (QBLK - 1))

    oj = lax.broadcasted_iota(I32, (ns, nc), 0)
    on = lax.broadcasted_iota(I32, (ns, nc), 1)
    ovl_t = jnp.where((on * CMP_STRIDE + CMP_LEN - 1 >= oj * SEL_BLOCK)
                      & (on * CMP_STRIDE <= oj * SEL_BLOCK + SEL_BLOCK - 1), 1.0, 0.0).astype(BF16)
    jb = lax.broadcasted_iota(I32, (ns, QBLK), 0)
    jq = (t0 + lax.broadcasted_iota(I32, (ns, QBLK), 1)) >> 6
    forced = (jb == 0) | (jb == jq) | (jb == jq - 1)

    kc_all = kvc_ref[0, :, 0:LANE].astype(BF16)
    vc_all = kvc_ref[0, :, LANE:2 * LANE].astype(BF16)
    ws = pl.multiple_of(jnp.maximum(t0 - WINDOW, 0), QBLK)
    kwc = kw_ref[0, :, pl.ds(ws, wl)].astype(BF16)
    vwc = vw_ref[0, :, pl.ds(ws, wl)].astype(BF16)

    outs = [None] * (NSA_WIDTH // LANE)
    for g in range(KV_HEADS):
        qg = _stack_group_queries(q, g)

        s_c = _dot_nt(qg, kc_all)
        ncol = lax.broadcasted_iota(I32, (1, nc), 1)
        p_c = _masked_softmax(s_c, (ncol * CMP_STRIDE + CMP_LEN - 1) <= pos_q)
        o_c = _dot(p_c, vc_all)

        psum = p_c[0:QBLK] + p_c[QBLK:2 * QBLK] + p_c[2 * QBLK:3 * QBLK] + p_c[3 * QBLK:4 * QBLK]
        imp_t = sum(lax.dot_general(ovl_t, part, (((1,), (1,)), ((), ())), preferred_element_type=F32)
                    for part in _split3(psum))
        x = jnp.where(forced, jnp.inf, jnp.where(jb > jq, -jnp.inf, imp_t))
        cnt = jnp.zeros((ns, QBLK), I32)
        for jp in range(ns):
            row = x[jp:jp + 1, :]
            beats = (row > x) | ((row == x) & (jb > jp))
            cnt = cnt + jnp.where(beats, 1, 0)
        sel_ref[0, 0, g * ns:(g + 1) * ns, :] = jnp.where(cnt < N_SELECT, 1.0, 0.0)

        s_w = _dot(qg, kwc)
        dpos = pos_q - (ws + lax.broadcasted_iota(I32, (1, wl), 1))
        p_w = _masked_softmax(s_w, (dpos >= 0) & (dpos <= WINDOW))
        o_w = _dot_nt(p_w, vwc)

        _place_heads((o_c, o_w), lambda h: (gt[:, 3 * h:3 * h + 1], gt[:, 3 * h + 2:3 * h + 3]), g, outs)
    part_ref[0] = jnp.concatenate(outs, axis=1)


def _nsa_select_kernel(q_ref, gt_ref, sel_ref, part_ref, ks_ref, vs_ref, o_ref, *, t_len, kc):
    i = pl.program_id(1)
    t0 = i * QBLK
    ns = t_len // SEL_BLOCK
    rows = GROUP * QBLK
    q = q_ref[0] * (HEAD_DIM ** -0.5)
    gt = gt_ref[0]
    n_kchunk = (t0 + QBLK + kc - 1) // kc
    part = part_ref[0]
    outs = [part[:, c * LANE:(c + 1) * LANE] for c in range(NSA_WIDTH // LANE)]
    for g in range(KV_HEADS):
        qg = _stack_group_queries(q, g)
        sel_t = sel_ref[0, 0, g * ns:(g + 1) * ns, :].astype(BF16)

        def sel_chunk(c, carry):
            m_prev, l_prev, acc = carry
            k0 = pl.multiple_of(c * kc, kc)
            kch = ks_ref[0, :, pl.ds(k0, kc)]
            vch = vs_ref[0, :, pl.ds(k0, kc)]
            s = _dot(qg, kch)
            ej = lax.broadcasted_iota(I32, (ns, kc), 0)
            en = lax.broadcasted_iota(I32, (ns, kc), 1)
            expand = jnp.where(((k0 + en) >> 6) == ej, 1.0, 0.0).astype(BF16)
            chosen = _dot_tn(sel_t, expand)
            key_pos = k0 + lax.broadcasted_iota(I32, (QBLK, kc), 1)
            qry_pos = t0 + lax.broadcasted_iota(I32, (QBLK, kc), 0)
            bias = jnp.where((chosen > 0.5) & (key_pos <= qry_pos), 0.0, NEG)
            s = s + jnp.concatenate([bias] * GROUP, axis=0)
            m_new = jnp.maximum(m_prev, jnp.max(s, axis=-1, keepdims=True))
            alpha = jnp.exp(m_prev - m_new)
            p = jnp.exp(s - m_new)
            l_new = alpha * l_prev + jnp.sum(p, axis=-1, keepdims=True)
            return m_new, l_new, alpha * acc + _dot_nt(p, vch)

        _, l_s, acc_s = lax.fori_loop(
            0, n_kchunk, sel_chunk,
            (jnp.full((rows, 1), NEG, F32), jnp.zeros((rows, 1), F32), jnp.zeros((rows, LANE), F32)))
        _place_heads((acc_s / l_s,), lambda h: (gt[:, 3 * h + 1:3 * h + 2],), g, outs)
    o_ref[0] = jnp.concatenate(outs, axis=1)


def _nsa_prompt(q3, gt3, kvc, kv4t, kvwt, kc):
    b, t, _ = q3.shape
    assert t % kc == 0 and t % QBLK == 0
    nq = t // QBLK
    ns = t // SEL_BLOCK
    slab = lambda c: pl.BlockSpec((1, LANE, t), lambda i, j: (i, c, 0))
    tile = lambda w_: pl.BlockSpec((1, QBLK, w_), lambda i, j: (i, j, 0))
    sel_spec = pl.BlockSpec((1, 1, KV_HEADS * ns, QBLK), lambda i, j: (i, j, 0, 0))
    part, sel = pl.pallas_call(
        functools.partial(_nsa_dense_kernel, t_len=t),
        grid=(b, nq),
        in_specs=[tile(NSA_WIDTH), tile(GATE_PAD),
                  pl.BlockSpec((1, t // CMP_STRIDE, KV_W), lambda i, j: (i, 0, 0)),
                  slab(0), slab(1)],
        out_specs=[tile(NSA_WIDTH), sel_spec],
        out_shape=[jax.ShapeDtypeStruct((b, t, NSA_WIDTH), F32),
                   jax.ShapeDtypeStruct((b, nq, KV_HEADS * ns, QBLK), F32)],
        compiler_params=_params(("arbitrary", "arbitrary")),
        name="nsa_dense",
    )(q3, gt3, kvc, kvwt, kvwt)
    return pl.pallas_call(
        functools.partial(_nsa_select_kernel, t_len=t, kc=kc),
        grid=(b, nq),
        in_specs=[tile(NSA_WIDTH), tile(GATE_PAD), sel_spec, tile(NSA_WIDTH), slab(2), slab(3)],
        out_specs=tile(NSA_WIDTH),
        out_shape=jax.ShapeDtypeStruct((b, t, NSA_WIDTH), F32),
        compiler_params=_params(("arbitrary", "arbitrary")),
        name="nsa_select",
    )(q3, gt3, sel, part, kv4t, kv4t)


def _xattn_prompt_kernel(x_ref, yp_ref, yn_ref, wo_ref, g_ref, wq_ref, mem_ref, wxo_ref, o_ref):
    y = jnp.concatenate([yp_ref[0], yn_ref[0]], axis=1)
    h1 = x_ref[0] + _dot(y, wo_ref[...])
    qx = _dot(_rms(h1, g_ref[...]), wq_ref[...]) * (X_HEAD_DIM ** -0.5)
    xw = X_HEADS * X_HEAD_DIM
    outs = []
    for h in range(X_HEADS):
        cs = slice(h * X_HEAD_DIM, (h + 1) * X_HEAD_DIM)
        s = _dot_nt(qx[:, cs], mem_ref[0, :, cs])
        e = jnp.exp(s - jnp.max(s, axis=-1, keepdims=True))
        p = e / jnp.sum(e, axis=-1, keepdims=True)
        outs.append(_dot(p, mem_ref[0, :, xw + h * X_HEAD_DIM:xw + (h + 1) * X_HEAD_DIM]))
    o_ref[0] = h1 + _dot(jnp.concatenate(outs, axis=1), wxo_ref[...])


def _xattn_prompt(x3, yp, yn, w_out, g_x, w_xq, mem_kv, w_xo, tm):
    b, t, d = x3.shape
    m = mem_kv.shape[1]
    full = lambda a: pl.BlockSpec(a.shape, lambda i, j: (0,) * a.ndim)
    return pl.pallas_call(
        _xattn_prompt_kernel,
        grid=(b, t // tm),
        in_specs=[pl.BlockSpec((1, tm, d), lambda i, j: (i, j, 0)),
                  pl.BlockSpec((1, tm, POOL_WIDTH), lambda i, j: (i, j, 0)),
                  pl.BlockSpec((1, tm, NSA_WIDTH), lambda i, j: (i, j, 0)),
                  full(w_out), full(g_x), full(w_xq),
                  pl.BlockSpec((1, m, mem_kv.shape[2]), lambda i, j: (i, 0, 0)),
                  full(w_xo)],
        out_specs=pl.BlockSpec((1, tm, d), lambda i, j: (i, j, 0)),
        out_shape=jax.ShapeDtypeStruct((b, t, d), F32),
        compiler_params=_params(("arbitrary", "arbitrary")),
        name="xattn_prompt",
    )(x3, yp, yn, w_out, g_x, w_xq, mem_kv, w_xo)


def _ffn_core(j, nj, h_ref, gf_ref, wg_ref, wv_ref, cw_ref, cb_ref, wd_ref, gfin_ref, y_ref,
              xn_ref, acc_ref, prev_rows):
    @pl.when(j == 0)
    def _():
        xn_ref[...] = _rms(h_ref[...], gf_ref[...]).astype(BF16)
        acc_ref[...] = jnp.zeros(acc_ref.shape, F32)

    xn = xn_ref[...]
    up_g = jnp.dot(xn, wg_ref[...], preferred_element_type=F32)
    up_v = jnp.dot(xn, wv_ref[...], preferred_element_type=F32)
    g1, g2, v1, v2 = prev_rows(up_g, up_v)
    cg = cb_ref[0:1, :] + cw_ref[0, 0:1, :] * g2 + cw_ref[1, 0:1, :] * g1 + cw_ref[2, 0:1, :] * up_g
    cv = cb_ref[1:2, :] + cw_ref[0, 1:2, :] * v2 + cw_ref[1, 1:2, :] * v1 + cw_ref[2, 1:2, :] * up_v
    act = cg / (1.0 + jnp.exp(-cg)) * cv
    acc_ref[...] += _dot(act, wd_ref[...])

    @pl.when(j == nj - 1)
    def _():
        y_ref[...] = _rms(h_ref[...] + acc_ref[...], gfin_ref[...])

    return up_g, up_v


def _ffn_prompt_kernel(h_ref, gf_ref, wg_ref, wv_ref, cw_ref, cb_ref, wd_ref, gfin_ref,
                       y_ref, st_ref, xn_ref, acc_ref, carry_ref, *, tiles_per_seq, nj, tm):
    r = pl.program_id(0)
    j = pl.program_id(1)
    row = lax.broadcasted_iota(I32, (tm, 1), 0)

    @pl.when((r % tiles_per_seq) == 0)
    def _():
        carry_ref[j] = jnp.zeros(carry_ref.shape[1:], F32)

    def prev_rows(up_g, up_v):
        res = []
        for half, up in enumerate((up_g, up_v)):
            prev = carry_ref[j, half]
            m1 = jnp.where(row == 0, prev[7:8], pltpu.roll(up, 1, axis=0))
            m2 = jnp.where(row == 0, prev[6:7], jnp.where(row == 1, prev[7:8], pltpu.roll(up, 2, axis=0)))
            res += [m1, m2]
        return res

    up_g, up_v = _ffn_core(j, nj, h_ref, gf_ref, wg_ref, wv_ref, cw_ref, cb_ref, wd_ref, gfin_ref,
                           y_ref, xn_ref, acc_ref, prev_rows)
    for half, up in enumerate((up_g, up_v)):
        carry_ref[j, half] = up[tm - 8:tm]
        st_ref[0, half] = up[tm - 8:tm]


def _ffn_specs(d, tf, nj):
    return [pl.BlockSpec((1, d), lambda r, j: (0, 0)),
            pl.BlockSpec((d, tf), lambda r, j: (0, j)),
            pl.BlockSpec((d, tf), lambda r, j: (0, nj + j)),
            pl.BlockSpec((3, 2, tf), lambda r, j: (0, 0, j)),
            pl.BlockSpec((2, tf), lambda r, j: (0, j)),
            pl.BlockSpec((tf, d), lambda r, j: (j, 0)),
            pl.BlockSpec((1, d), lambda r, j: (0, 0))]


def _ffn_prompt(h2, g_ffn, w_up, cw3, cb2, w_down, g_final, batch, tm, tf):
    n, d = h2.shape
    nj = D_FF // tf
    tps = n // batch // tm
    return pl.pallas_call(
        functools.partial(_ffn_prompt_kernel, tiles_per_seq=tps, nj=nj, tm=tm),
        grid=(n // tm, nj),
        in_specs=[pl.BlockSpec((tm, d), lambda r, j: (r, 0))] + _ffn_specs(d, tf, nj),
        out_specs=[pl.BlockSpec((tm, d), lambda r, j: (r, 0)),
                   pl.BlockSpec((1, 2, 8, tf), lambda r, j: (r, 0, 0, j))],
        out_shape=[jax.ShapeDtypeStruct((n, d), F32),
                   jax.ShapeDtypeStruct((n // tm, 2, 8, D_FF), F32)],
        scratch_shapes=[pltpu.VMEM((tm, d), BF16), pltpu.VMEM((tm, d), F32),
                        pltpu.VMEM((nj, 2, 8, tf), F32)],
        compiler_params=_params(("arbitrary", "arbitrary")),
        name="ffn_prompt",
    )(h2, g_ffn, w_up, w_up, cw3, cb2, w_down, g_final)


def _ffn_sample_kernel(h_ref, gf_ref, wg_ref, wv_ref, cw_ref, cb_ref, wd_ref, gfin_ref, sg_ref, sv_ref,
                       y_ref, up_ref, xn_ref, acc_ref, *, nj):
    j = pl.program_id(1)
    prev_rows = lambda up_g, up_v: (sg_ref[1], sg_ref[0], sv_ref[1], sv_ref[0])
    up_g, up_v = _ffn_core(j, nj, h_ref, gf_ref, wg_ref, wv_ref, cw_ref, cb_ref, wd_ref, gfin_ref,
                           y_ref, xn_ref, acc_ref, prev_rows)
    up_ref[0] = up_g
    up_ref[1] = up_v


def _ffn_sample(h2, g_ffn, w_up, cw3, cb2, w_down, g_final, st_t, tf):
    n, d = h2.shape
    nj = D_FF // tf
    return pl.pallas_call(
        functools.partial(_ffn_sample_kernel, nj=nj),
        grid=(1, nj),
        in_specs=[pl.BlockSpec((n, d), lambda r, j: (0, 0))] + _ffn_specs(d, tf, nj)
        + [pl.BlockSpec((2, n, tf), lambda r, j: (0, 0, j)),
           pl.BlockSpec((2, n, tf), lambda r, j: (0, 0, nj + j))],
        out_specs=[pl.BlockSpec((n, d), lambda r, j: (0, 0)),
                   pl.BlockSpec((2, n, tf), lambda r, j: (0, 0, j))],
        out_shape=[jax.ShapeDtypeStruct((n, d), F32),
                   jax.ShapeDtypeStruct((2, n, D_FF), F32)],
        scratch_shapes=[pltpu.VMEM((n, d), BF16), pltpu.VMEM((n, d), F32)],
        compiler_params=_params(("arbitrary", "arbitrary")),
        name="ffn_sample",
    )(h2, g_ffn, w_up, w_up, cw3, cb2, w_down, g_final, st_t, st_t)


def _head_rows(qrow):
    rowi = lax.broadcasted_iota(I32, (N_HEADS, LANE), 0)
    lane = lax.broadcasted_iota(I32, (N_HEADS, LANE), 1)
    x = jnp.zeros((N_HEADS, LANE), F32)
    for c in range(NSA_WIDTH // LANE):
        x = jnp.where((rowi >> 1) == c, jnp.broadcast_to(qrow[:, c * LANE:(c + 1) * LANE], (N_HEADS, LANE)), x)
    x = jnp.where((rowi & 1) != (rowi >> 2), pltpu.roll(x, HEAD_DIM, axis=1), x)
    return jnp.where((lane >> 6) == (rowi >> 2), x, 0.0)


def _page_copy(cache_ref, raw_ref, sem_ref, phys, p, slot):
    return pltpu.make_async_copy(cache_ref.at[phys, pl.ds(0, KV_W), :], raw_ref.at[slot, p], sem_ref.at[slot, p])


def _nsa_sample_cmp_kernel(pt_ref, q_ref, cache_ref, wbig_ref, pe_ref, wpe_ref, oc_ref, imp_ref,
                           raw_ref, tok_ref, sem_ref, *, n_pages, n_batch, page):
    b = pl.program_id(0)
    slot = b % 2
    n_tok = n_pages * page
    n_chunk = n_tok // CMP_STRIDE

    def start_all(bb, sl):
        def body(p, _):
            _page_copy(cache_ref, raw_ref, sem_ref, pt_ref[bb * n_pages + p], p, sl).start()
            return 0
        lax.fori_loop(0, n_pages, body, 0)

    @pl.when(b == 0)
    def _():
        start_all(0, 0)

    @pl.when(b + 1 < n_batch)
    def _():
        start_all(b + 1, 1 - slot)

    def to_token_major(p, _):
        _page_copy(cache_ref, raw_ref, sem_ref, 0, p, slot).wait()
        xt = raw_ref[slot, p]
        r0 = pl.multiple_of(p * page, page)
        for half in range(2):
            tok_ref[half, pl.ds(r0, page), :] = xt[half * LANE:(half + 1) * LANE].T
        return 0
    lax.fori_loop(0, n_pages, to_token_major, 0)

    load = lambda s: jnp.concatenate([tok_ref[0, pl.ds(s, n_chunk, stride=CMP_STRIDE), :],
                                      tok_ref[1, pl.ds(s, n_chunk, stride=CMP_STRIDE), :]], axis=1)
    kvc = _compress_rows(load, wbig_ref, pe_ref, wpe_ref, n_chunk)
    q8 = _head_rows(q_ref[0] * (HEAD_DIM ** -0.5))
    s = _dot_nt(q8, kvc[:, 0:LANE])
    ncol = lax.broadcasted_iota(I32, (1, n_chunk), 1)
    p = _masked_softmax(s, (ncol * CMP_STRIDE + CMP_LEN - 1) <= n_tok)
    oc_ref[0] = _dot(p, kvc[:, LANE:2 * LANE])
    rowi = lax.broadcasted_iota(I32, (N_HEADS, n_chunk), 0)
    psum = jnp.where(rowi == 0, jnp.sum(p[0:GROUP], axis=0, keepdims=True),
                     jnp.where(rowi == 1, jnp.sum(p[GROUP:2 * GROUP], axis=0, keepdims=True), 0.0))
    ns_pad = imp_ref.shape[2]
    on = lax.broadcasted_iota(I32, (n_chunk, ns_pad), 0)
    oj = lax.broadcasted_iota(I32, (n_chunk, ns_pad), 1)
    ovl = jnp.where((on * CMP_STRIDE + CMP_LEN - 1 >= oj * SEL_BLOCK)
                    & (on * CMP_STRIDE <= oj * SEL_BLOCK + SEL_BLOCK - 1), 1.0, 0.0).astype(BF16)
    imp_ref[0] = sum(jnp.dot(part, ovl, preferred_element_type=F32) for part in _split3(psum))


def _nsa_sample_cmp(pt_flat, q3, cache3, wbig, pe8, wpe, n_pages, ns_pad):
    nb = q3.shape[0]
    page = cache3.shape[2]
    assert page == LANE
    n_tok = n_pages * page
    gs = pltpu.PrefetchScalarGridSpec(
        num_scalar_prefetch=1, grid=(nb,),
        in_specs=[pl.BlockSpec((1, 1, NSA_WIDTH), lambda i, pt: (i, 0, 0)),
                  pl.BlockSpec(memory_space=pl.ANY),
                  pl.BlockSpec(wbig.shape, lambda i, pt: (0, 0, 0)),
                  pl.BlockSpec(pe8.shape, lambda i, pt: (0, 0)),
                  pl.BlockSpec(wpe.shape, lambda i, pt: (0, 0))],
        out_specs=[pl.BlockSpec((1, N_HEADS, LANE), lambda i, pt: (i, 0, 0)),
                   pl.BlockSpec((1, N_HEADS, ns_pad), lambda i, pt: (i, 0, 0))],
        scratch_shapes=[pltpu.VMEM((2, n_pages, KV_W, page), F32), pltpu.VMEM((2, n_tok, LANE), F32),
                        pltpu.SemaphoreType.DMA((2, n_pages))])
    return pl.pallas_call(
        functools.partial(_nsa_sample_cmp_kernel, n_pages=n_pages, n_batch=nb, page=page),
        grid_spec=gs,
        out_shape=[jax.ShapeDtypeStruct((nb, N_HEADS, LANE), F32),
                   jax.ShapeDtypeStruct((nb, N_HEADS, ns_pad), F32)],
        compiler_params=_params(("arbitrary",)),
        name="nsa_sample_cmp",
    )(pt_flat, q3, cache3, wbig, pe8, wpe)


def _topk_sample_kernel(imp_ref, idx_ref, *, jq, k_past):
    x = imp_ref[...]
    col = lax.broadcasted_iota(I32, x.shape, 1)
    x = jnp.where((col == 0) | (col == jq - 1), jnp.inf, jnp.where(col >= jq, -jnp.inf, x))
    lane = lax.broadcasted_iota(I32, idx_ref.shape, 1)
    out = jnp.zeros(idx_ref.shape, I32)
    big = jnp.int32(1 << 30)
    for k in range(k_past):
        m = jnp.max(x, axis=-1, keepdims=True)
        idx = jnp.min(jnp.where(x == m, col, big), axis=-1, keepdims=True)
        out = jnp.where(lane == k, idx, out)
        x = jnp.where(col == idx, -jnp.inf, x)
    idx_ref[...] = out


def _topk_sample(imp2, jq, k_past):
    n, w = imp2.shape
    return pl.pallas_call(
        functools.partial(_topk_sample_kernel, jq=jq, k_past=k_past),
        grid=(1,),
        in_specs=[pl.BlockSpec((n, w), lambda i: (0, 0))],
        out_specs=pl.BlockSpec((n, LANE), lambda i: (0, 0)),
        out_shape=jax.ShapeDtypeStruct((n, LANE), I32),
        compiler_params=_params(("arbitrary",)),
        name="topk_sample",
    )(imp2)


def _blk_copy(cache_ref, buf_ref, sem_ref, phys, kv, g, r, slot):
    row0 = KV_W + kv * LANE + g * HEAD_DIM
    return pltpu.make_async_copy(cache_ref.at[phys, pl.ds(row0, HEAD_DIM), :],
                                 buf_ref.at[slot, kv, pl.ds(g * HEAD_DIM, HEAD_DIM), pl.ds(r * LANE, LANE)],
                                 sem_ref.at[slot])


def _nsa_sample_sel_kernel(idx_ref, pt_ref, q_ref, gt_ref, oc_ref, win_ref, kvn4_ref, kvnw_ref, cache_ref,
                           o_ref, wout_ref, buf_ref, sem_ref, *, n_pages, n_batch, page, k_past):
    b = pl.program_id(0)
    slot = b % 2
    per_page = page // SEL_BLOCK
    nl = k_past * page

    def for_blocks(bb, fn):
        def body(r, _):
            for g in range(KV_HEADS):
                j = idx_ref[(bb * KV_HEADS + g) * k_past + r]
                phys = pt_ref[bb * n_pages + j // per_page]
                for kv in range(2):
                    fn(phys, kv, g, r)
            return 0
        lax.fori_loop(0, k_past, body, 0)

    def start_all(bb, sl):
        for_blocks(bb, lambda phys, kv, g, r: _blk_copy(cache_ref, buf_ref, sem_ref, phys, kv, g, r, sl).start())

    @pl.when(b == 0)
    def _():
        start_all(0, 0)

    @pl.when(b + 1 < n_batch)
    def _():
        start_all(b + 1, 1 - slot)

    def wait_body(r, _):
        for g in range(KV_HEADS):
            for kv in range(2):
                _blk_copy(cache_ref, buf_ref, sem_ref, 0, kv, g, r, slot).wait()
        return 0
    lax.fori_loop(0, k_past, wait_body, 0)

    q8 = _head_rows(q_ref[0] * (HEAD_DIM ** -0.5))
    rowi = lax.broadcasted_iota(I32, (N_HEADS, 1), 0)
    lane = lax.broadcasted_iota(I32, (N_HEADS, LANE), 1)
    own = (lane >> 6) == (rowi >> 2)
    mine = jnp.where(lax.broadcasted_iota(I32, (N_HEADS, n_batch), 1) == b, 0.0, -jnp.inf)

    def attend(k_t, v_t, bias):
        s = _dot(q8, k_t) + bias
        e = jnp.exp(s - jnp.max(s, axis=-1, keepdims=True))
        return _dot_nt(e, v_t) / jnp.sum(e, axis=-1, keepdims=True)

    lane_s = lax.broadcasted_iota(I32, (1, nl), 1)
    halves = []
    for g in range(KV_HEADS):
        hv = jnp.zeros((1, nl), I32)
        for r in range(k_past):
            hv = jnp.where((lane_s // page) == r, idx_ref[(b * KV_HEADS + g) * k_past + r] % per_page, hv)
        halves.append(hv)
    vis = ((lane_s % page) // SEL_BLOCK) == jnp.where(rowi < GROUP, halves[0], halves[1])
    o_s = attend(jnp.concatenate([buf_ref[slot, 0], kvn4_ref[0, KV_W:KV_W + LANE]], axis=1),
                 jnp.concatenate([buf_ref[slot, 1], kvn4_ref[0, KV_W + LANE:2 * KV_W]], axis=1),
                 jnp.concatenate([jnp.where(vis, 0.0, -jnp.inf), mine], axis=1))

    cw = win_ref[0]
    wlen = cw.shape[1]
    o_w = attend(jnp.concatenate([cw[0:LANE], kvnw_ref[0, 0:LANE]], axis=1),
                 jnp.concatenate([cw[LANE:2 * LANE], kvnw_ref[0, LANE:2 * LANE]], axis=1),
                 jnp.concatenate([jnp.zeros((N_HEADS, wlen), F32), mine], axis=1))
    lane_n = lax.broadcasted_iota(I32, (KV_W, n_batch), 1)
    new_col = jnp.sum(jnp.where(lane_n == b, kvnw_ref[0], 0.0), axis=1, keepdims=True)
    lane_w = lax.broadcasted_iota(I32, (KV_W, wlen), 1)
    wout_ref[0] = jnp.where(lane_w == wlen - 1, new_col, pltpu.roll(cw, wlen - 1, axis=1))

    gt = jnp.broadcast_to(gt_ref[0], (N_HEADS, GATE_PAD))
    gate = lambda br: jnp.sum(jnp.where(lane == rowi * N_BRANCH + br, gt, 0.0), axis=-1, keepdims=True)
    o = gate(0) * oc_ref[0] + gate(1) * o_s + gate(2) * o_w
    o_ref[0] = jnp.where(own, o, 0.0)


def _nsa_sample_sel(idx_flat, pt_flat, q3, gt3, oc, win_t, kvn4_t, kvnw_t, cache3, n_pages, k_past):
    nb = q3.shape[0]
    page = cache3.shape[2]
    wlen = win_t.shape[2]
    assert page == LANE
    per = lambda *blk: pl.BlockSpec((1,) + blk, lambda i, a, c: (i, 0, 0))
    whole = lambda arr: pl.BlockSpec(arr.shape, lambda i, a, c: (0, 0, 0))
    gs = pltpu.PrefetchScalarGridSpec(
        num_scalar_prefetch=2, grid=(nb,),
        in_specs=[per(1, NSA_WIDTH), per(1, GATE_PAD), per(N_HEADS, LANE), per(KV_W, wlen),
                  whole(kvn4_t), whole(kvnw_t), pl.BlockSpec(memory_space=pl.ANY)],
        out_specs=[per(N_HEADS, LANE), per(KV_W, wlen)],
        scratch_shapes=[pltpu.VMEM((2, 2, LANE, k_past * page), F32), pltpu.SemaphoreType.DMA((2,))])
    return pl.pallas_call(
        functools.partial(_nsa_sample_sel_kernel, n_pages=n_pages, n_batch=nb, page=page, k_past=k_past),
        grid_spec=gs,
        out_shape=[jax.ShapeDtypeStruct((nb, N_HEADS, LANE), F32),
                   jax.ShapeDtypeStruct((nb, KV_W, wlen), F32)],
        compiler_params=_params(("arbitrary",)),
        name="nsa_sample_sel",
    )(idx_flat, pt_flat, q3, gt3, oc, win_t, kvn4_t, kvnw_t, cache3)


def _mix_sample_kernel(x_ref, yp_ref, yn_ref, wo_ref, g_ref, wq_ref, h1_ref, qx_ref):
    y = jnp.concatenate([yp_ref[...], yn_ref[...]], axis=1)
    h1 = x_ref[...] + _dot(y, wo_ref[...])
    h1_ref[...] = h1
    qx_ref[...] = _dot(_rms(h1, g_ref[...]), wq_ref[...]) * (X_HEAD_DIM ** -0.5)


def _mix_sample(x, yp, yn, w_out, g_x, w_xq):
    n, d = x.shape
    full = lambda a: pl.BlockSpec(a.shape, lambda i: (0,) * a.ndim)
    args = (x, yp, yn, w_out, g_x, w_xq)
    return pl.pallas_call(
        _mix_sample_kernel, grid=(1,),
        in_specs=[full(a) for a in args],
        out_specs=[pl.BlockSpec((n, d), lambda i: (0, 0))] * 2,
        out_shape=[jax.ShapeDtypeStruct((n, d), F32)] * 2,
        compiler_params=_params(("arbitrary",)),
        name="mix_sample",
    )(*args)


def _xattn_sample_kernel(qx_ref, mem_ref, o_ref):
    xw = X_HEADS * X_HEAD_DIM
    rowi = lax.broadcasted_iota(I32, (8, xw), 0)
    lane = lax.broadcasted_iota(I32, (8, xw), 1)
    own = (lane >> 8) == rowi
    q8 = jnp.where(own, jnp.broadcast_to(qx_ref[0], (8, xw)), 0.0)
    s = _dot_nt(q8, mem_ref[0, :, 0:xw])
    e = jnp.exp(s - jnp.max(s, axis=-1, keepdims=True))
    p = e / jnp.sum(e, axis=-1, keepdims=True)
    o = _dot(p, mem_ref[0, :, xw:2 * xw])
    o_ref[0] = jnp.sum(jnp.where(own, o, 0.0), axis=0, keepdims=True)


def _xattn_sample(qx3, mem3):
    nb, _, xw = qx3.shape
    m = mem3.shape[1]
    return pl.pallas_call(
        _xattn_sample_kernel, grid=(nb,),
        in_specs=[pl.BlockSpec((1, 1, xw), lambda i: (i, 0, 0)),
                  pl.BlockSpec((1, m, 2 * xw), lambda i: (i, 0, 0))],
        out_specs=pl.BlockSpec((1, 1, xw), lambda i: (i, 0, 0)),
        out_shape=jax.ShapeDtypeStruct((nb, 1, xw), F32),
        compiler_params=_params(("arbitrary",)),
        name="xattn_sample",
    )(qx3, mem3)


def _resid_proj_kernel(h_ref, o_ref, w_ref, y_ref):
    y_ref[...] = h_ref[...] + _dot(o_ref[...], w_ref[...])


def _resid_proj(h, o, w):
    n, d = h.shape
    full = lambda a: pl.BlockSpec(a.shape, lambda i: (0,) * a.ndim)
    return pl.pallas_call(
        _resid_proj_kernel, grid=(1,),
        in_specs=[full(h), full(o), full(w)],
        out_specs=pl.BlockSpec((n, d), lambda i: (0, 0)),
        out_shape=jax.ShapeDtypeStruct((n, d), F32),
        compiler_params=_params(("arbitrary",)),
        name="resid_proj",
    )(h, o, w)


def _prep_weights(g_mix, w_in, w_pool, pool_scale, w_cmp, pe_cmp, w_out, g_xattn, g_mem, w_xq, w_xkv,
                  w_xo, g_ffn, w_up, conv_w, conv_b, w_down, g_final):
    o_kv = POOL_WIDTH + NSA_WIDTH
    o_gate = o_kv + N_BRANCH * KV_W
    w_tok = jnp.concatenate([w_in[:, :o_kv + KV_W], jnp.pad(w_in[:, o_gate:], ((0, 0), (0, GATE_PAD - N_GATE)))],
                            axis=1)
    r = CMP_LEN // CMP_STRIDE
    w5 = w_cmp.reshape(2, r, CMP_STRIDE, HEAD_DIM, HEAD_DIM)
    eye = jnp.eye(2, dtype=F32)
    wbig = jnp.einsum('cisde,cC,gG->scgdiCGe', w5, eye, eye).reshape(CMP_STRIDE, KV_W, r * KV_W)
    wpe = jnp.einsum('cke,cC->ckCe', w_cmp.reshape(2, CMP_LEN * HEAD_DIM, HEAD_DIM), eye)
    wpe = jnp.broadcast_to(wpe[:, :, :, None, :], (2, CMP_LEN * HEAD_DIM, 2, KV_HEADS, HEAD_DIM))
    row = lambda v: v.reshape(1, -1)
    return dict(
        g_mix=row(g_mix), w_tok=w_tok.astype(BF16), w_feat=w_in[:, o_kv:o_gate].T.astype(BF16),
        w_pool=w_pool.astype(BF16), pool_scale=row(pool_scale),
        wbig=wbig.astype(BF16), wpe=wpe.reshape(2 * CMP_LEN * HEAD_DIM, KV_W).astype(BF16),
        pe8=jnp.broadcast_to(pe_cmp.reshape(1, -1), (8, 2 * CMP_LEN * HEAD_DIM)),
        w_out=w_out.astype(BF16), g_xattn=row(g_xattn), g_mem=row(g_mem), w_xq=w_xq.astype(BF16),
        w_xkv=w_xkv.astype(BF16), w_xo=w_xo.astype(BF16), g_ffn=row(g_ffn), w_up=w_up.astype(BF16),
        cw3=conv_w.reshape(3, 2, D_FF), cb2=conv_b.reshape(2, D_FF), w_down=w_down.astype(BF16),
        g_final=row(g_final))


def _token_major(a_t, lead):
    b, _, t = a_t.shape
    nd = len(lead)
    return jnp.transpose(a_t.reshape((b,) + lead + (t,)), (0, nd + 1) + tuple(range(1, nd + 1)))


def _feature_major(a, nfeat):
    nd = a.ndim
    return jnp.transpose(a, (0,) + tuple(range(2, nd)) + (1,)).reshape(a.shape[0], nfeat, a.shape[1])


def _prompt_layer(x, mem, w, *, tm=512, tf=1408, kc=512):
    b, t, d = x.shape
    n = b * t
    u, q, kvc_tok, gt, kv4t, kvwt = _mix_in(x, w['g_mix'], w['w_tok'], w['w_feat'], tm, "mix_in_prompt")
    r3 = lambda a: a.reshape(b, t, a.shape[-1])
    u, q, kvc_tok, gt = map(r3, (u, q, kvc_tok, gt))
    y_pool = _pool_prompt(u, w['w_pool'], w['pool_scale'], tm)
    kvc = _compress_prompt(kvc_tok, w['wbig'], w['pe8'], w['wpe'])
    y_nsa = _nsa_prompt(q, gt, kvc, kv4t, kvwt, kc)
    m = mem.shape[1]
    (mem_kv,) = _rms_proj(mem.reshape(b * m, d), w['g_mem'], w['w_xkv'], (2 * X_HEADS * X_HEAD_DIM,), m,
                          name="mem_project")
    mem_kv = mem_kv.reshape(b, m, 2 * X_HEADS * X_HEAD_DIM)
    h2 = _xattn_prompt(x, y_pool, y_nsa, w['w_out'], w['g_xattn'], w['w_xq'], mem_kv, w['w_xo'], tm)
    y, st = _ffn_prompt(h2.reshape(n, d), w['g_ffn'], w['w_up'], w['cw3'], w['cb2'], w['w_down'],
                        w['g_final'], b, tm, tf)
    wlen = min(WINDOW, t)
    return (y.reshape(b, t, d),
            _token_major(kv4t, (4, KV_HEADS, HEAD_DIM)),
            _token_major(kvwt[:, :, t - wlen:], (2, KV_HEADS, HEAD_DIM)),
            u[:, t - POOL_STATE:],
            jnp.swapaxes(st[t // tm - 1::t // tm, :, 6:8], 1, 2).reshape(b, 2, 2 * D_FF),
            mem_kv.reshape(b, m, 2, X_HEADS, X_HEAD_DIM))


def _sample_layer(x, cache_kv, page_table, cache_win, state_pool, state_ffn, cache_mem, w, *, tf=1408):
    nb, _, d = x.shape
    n_phys, page = cache_kv.shape[:2]
    n_pages = page_table.shape[1]
    past = n_pages * page
    u, q, _, gt, kv4t, kvwt = _mix_in(x.reshape(1, nb, d), w['g_mix'], w['w_tok'], w['w_feat'], nb, "mix_in_sample")
    y_pool = _pool_sample(jnp.swapaxes(state_pool, 0, 1), u, w['w_pool'], w['pool_scale'], past)
    cache3 = _feature_major(cache_kv, 2 * KV_W)
    pt_flat = page_table.reshape(-1)
    jq = past // SEL_BLOCK
    ns_pad = -(-(jq + 1) // LANE) * LANE
    k_past = min(N_SELECT, jq + 1) - 1
    q3 = q.reshape(nb, 1, NSA_WIDTH)
    oc, imp = _nsa_sample_cmp(pt_flat, q3, cache3, w['wbig'], w['pe8'], w['wpe'], n_pages, ns_pad)
    idx = _topk_sample(imp[:, :KV_HEADS].reshape(nb * KV_HEADS, ns_pad), jq, k_past)
    idx_flat = idx[:, :k_past].reshape(-1)
    o8, win_out = _nsa_sample_sel(idx_flat, pt_flat, q3, gt.reshape(nb, 1, GATE_PAD), oc,
                                  _feature_major(cache_win, KV_W), kv4t, kvwt, cache3, n_pages, k_past)
    y_nsa = jnp.concatenate([o8[:, :GROUP, :HEAD_DIM].reshape(nb, GROUP * HEAD_DIM),
                             o8[:, GROUP:, HEAD_DIM:].reshape(nb, GROUP * HEAD_DIM)], axis=1)
    h1, qx = _mix_sample(x.reshape(nb, d), y_pool, y_nsa, w['w_out'], w['g_xattn'], w['w_xq'])
    m = cache_mem.shape[1]
    ox = _xattn_sample(qx.reshape(nb, 1, d), cache_mem.reshape(nb, m, 2 * X_HEADS * X_HEAD_DIM))
    h2 = _resid_proj(h1, ox.reshape(nb, d), w['w_xo'])
    y, up2 = _ffn_sample(h2, w['g_ffn'], w['w_up'], w['cw3'], w['cb2'], w['w_down'], w['g_final'],
                         jnp.swapaxes(state_ffn, 0, 1), tf)
    up = jnp.swapaxes(up2, 0, 1).reshape(nb, 1, 2 * D_FF)
    return (y.reshape(nb, 1, d),
            jnp.swapaxes(_token_major(kv4t, (4, KV_HEADS, HEAD_DIM)), 0, 1),
            _token_major(win_out, (2, KV_HEADS, HEAD_DIM)),
            jnp.concatenate([state_pool[:, 1:], u[:, None, :]], axis=1),
            jnp.concatenate([state_ffn[:, 1:], up], axis=1),
            )


def kernel(x_prompt, x_sample, mem_prompt, cache_kv, page_table, cache_win, state_pool, state_ffn, cache_mem, g_mix, w_in, w_pool, pool_scale, w_cmp, pe_cmp, w_out, g_xattn, g_mem, w_xq, w_xkv, w_xo, g_ffn, w_up, conv_w, conv_b, w_down, g_final):
    assert g_mix.shape[0] == 1, "single layer"
    w = _prep_weights(g_mix[0], w_in[0], w_pool[0], pool_scale[0], w_cmp[0], pe_cmp[0], w_out[0], g_xattn[0],
                      g_mem[0], w_xq[0], w_xkv[0], w_xo[0], g_ffn[0], w_up[0], conv_w[0], conv_b[0], w_down[0],
                      g_final)
    yp, kv_p, win_p, pool_p, ffn_p, mem_p = _prompt_layer(x_prompt, mem_prompt, w)
    ys, kv_s, win_s, pool_s, ffn_s = _sample_layer(x_sample, cache_kv[0], page_table, cache_win[0],
                                                    state_pool[0], state_ffn[0], cache_mem[0], w)
    lead = lambda a: a[None]
    return (yp, ys, lead(kv_p), lead(kv_s), lead(win_p), lead(win_s), lead(pool_p), lead(pool_s),
            lead(ffn_p), lead(ffn_s), lead(mem_p))
```

```python
import functools

import jax
import jax.numpy as jnp
from jax import lax
from jax.experimental import pallas as pl
from jax.experimental.pallas import tpu as pltpu

F32 = jnp.float32
BF16 = jnp.bfloat16
I32 = jnp.int32

D_MODEL = 1024
POOL_WIDTH = 512
POOL_WINDOWS = (2, 4, 8, 16)
POOL_GC = 128
POOL_STATE = 15
NSA_WIDTH = 512
HEAD_DIM = 64
N_HEADS = 8
KV_HEADS = 2
GROUP = 4
N_BRANCH = 3
CMP_LEN = 32
CMP_STRIDE = 16
SEL_BLOCK = 64
N_SELECT = 16
WINDOW = 512
QBLK = 128
KV_W = 2 * KV_HEADS * HEAD_DIM
N_GATE = N_HEADS * N_BRANCH
LANE = 128
GATE_PAD = LANE
X_HEADS = 4
X_HEAD_DIM = 256
D_FF = 2816
EPS = 1e-6
NEG = -1e30
VMEM_LIMIT = 56 * 1024 * 1024
CHUNK_PITCH = 24
PAGE_UNROLL = 4


def _dot(a, b):
    return jnp.dot(a.astype(BF16), b.astype(BF16), preferred_element_type=F32)


def _dot_nt(a, b):
    return lax.dot_general(a.astype(BF16), b.astype(BF16), (((1,), (1,)), ((), ())),
                           preferred_element_type=F32)


def _dot_tn(a, b):
    return lax.dot_general(a.astype(BF16), b.astype(BF16), (((0,), (0,)), ((), ())),
                           preferred_element_type=F32)


def _rms(x, g):
    return x * lax.rsqrt(jnp.mean(x * x, axis=-1, keepdims=True) + EPS) * g


def _split3(x):
    p0 = x.astype(BF16)
    r = x - p0.astype(F32)
    p1 = r.astype(BF16)
    p2 = (r - p1.astype(F32)).astype(BF16)
    return p0, p1, p2


def _masked_softmax(s, mask):
    s = jnp.where(mask, s, -jnp.inf)
    m = jnp.max(s, axis=-1, keepdims=True)
    m = jnp.where(m == -jnp.inf, 0.0, m)
    e = jnp.exp(s - m)
    return e / jnp.maximum(jnp.sum(e, axis=-1, keepdims=True), 1e-30)


def _params(sem):
    return pltpu.CompilerParams(dimension_semantics=sem, vmem_limit_bytes=VMEM_LIMIT)


def _rms_proj_kernel(x_ref, g_ref, w_ref, *out_refs, widths, sigmoid_last):
    xn = _rms(x_ref[...], g_ref[...]).astype(BF16)
    off = 0
    for k, (o_ref, wd) in enumerate(zip(out_refs, widths)):
        y = jnp.dot(xn, w_ref[:, off:off + wd], preferred_element_type=F32)
        if sigmoid_last and k == len(widths) - 1:
            y = 1.0 / (1.0 + jnp.exp(-y))
        o_ref[...] = y
        off += wd


def _rms_proj(x, g, w, widths, tm, sigmoid_last=False, name="rms_proj"):
    n, d = x.shape
    ntot = w.shape[1]
    assert sum(widths) == ntot and n % tm == 0
    return pl.pallas_call(
        functools.partial(_rms_proj_kernel, widths=tuple(widths), sigmoid_last=sigmoid_last),
        grid=(n // tm,),
        in_specs=[pl.BlockSpec((tm, d), lambda i: (i, 0)),
                  pl.BlockSpec((1, d), lambda i: (0, 0)),
                  pl.BlockSpec((d, ntot), lambda i: (0, 0))],
        out_specs=[pl.BlockSpec((tm, wd), lambda i: (i, 0)) for wd in widths],
        out_shape=[jax.ShapeDtypeStruct((n, wd), F32) for wd in widths],
        compiler_params=_params(("arbitrary",)),
        name=name,
    )(x, g, w)


def _mix_in_kernel(x_ref, g_ref, wt_ref, wf_ref, u_ref, q_ref, kc_ref, ks_ref, gt_ref, kv4_ref, kvw_ref):
    xn = _rms(x_ref[...], g_ref[...]).astype(BF16)
    off = 0
    for o_ref in (u_ref, q_ref, kc_ref, ks_ref, gt_ref):
        wd = o_ref.shape[-1]
        y = jnp.dot(xn, wt_ref[:, off:off + wd], preferred_element_type=F32)
        o_ref[...] = 1.0 / (1.0 + jnp.exp(-y)) if o_ref is gt_ref else y
        off += wd
    kv_t = lax.dot_general(wf_ref[...], xn, (((1,), (1,)), ((), ())), preferred_element_type=F32)
    kv4_ref[0] = kv_t[0:2 * KV_W]
    kvw_ref[0] = kv_t[2 * KV_W:3 * KV_W]


def _mix_in(x3, g, w_tok, w_feat, tm, name):
    b, t, d = x3.shape
    n = b * t
    tps = t // tm
    widths = (POOL_WIDTH, NSA_WIDTH, KV_W, LANE, GATE_PAD)
    assert sum(widths) == w_tok.shape[1] and t % tm == 0
    return pl.pallas_call(
        _mix_in_kernel,
        grid=(n // tm,),
        in_specs=[pl.BlockSpec((tm, d), lambda i: (i, 0)),
                  pl.BlockSpec((1, d), lambda i: (0, 0)),
                  pl.BlockSpec(w_tok.shape, lambda i: (0, 0)),
                  pl.BlockSpec(w_feat.shape, lambda i: (0, 0))],
        out_specs=[pl.BlockSpec((tm, wd), lambda i: (i, 0)) for wd in widths]
        + [pl.BlockSpec((1, 2 * KV_W, tm), lambda i: (i // tps, 0, i % tps)),
           pl.BlockSpec((1, KV_W, tm), lambda i: (i // tps, 0, i % tps))],
        out_shape=[jax.ShapeDtypeStruct((n, wd), F32) for wd in widths]
        + [jax.ShapeDtypeStruct((b, 2 * KV_W, t), F32), jax.ShapeDtypeStruct((b, KV_W, t), F32)],
        compiler_params=_params(("arbitrary",)),
        name=name,
    )(x3.reshape(n, d), g, w_tok, w_feat)


def _pool_tail(s_list, u, cnts, w_ref, sc_ref, y_ref, lead):
    for gi in range(len(POOL_WINDOWS)):
        cols = slice(gi * POOL_GC, (gi + 1) * POOL_GC)
        d = s_list[gi] / cnts[gi] - u[:, cols]
        y = _dot(d, w_ref[gi]) * sc_ref[:, cols]
        if lead:
            y_ref[0, :, cols] = y
        else:
            y_ref[:, cols] = y


def _pool_prompt_kernel(u_ref, w_ref, sc_ref, y_ref, ext_ref, *, tm):
    t = pl.program_id(1)
    hist = 16

    @pl.when(t == 0)
    def _():
        ext_ref[0:hist, :] = jnp.zeros((hist, POOL_WIDTH), F32)

    @pl.when(t > 0)
    def _():
        ext_ref[0:hist, :] = ext_ref[tm:tm + hist, :]

    u = u_ref[0]
    ext_ref[hist:hist + tm, :] = u
    pos = t * tm + lax.broadcasted_iota(I32, (tm, 1), 0)
    s_list, cnts = [], []
    for gi, w in enumerate(POOL_WINDOWS):
        cols = slice(gi * POOL_GC, (gi + 1) * POOL_GC)
        s = u[:, cols]
        for k in range(1, w):
            s = s + ext_ref[hist - k:hist - k + tm, cols]
        s_list.append(s)
        cnts.append(jnp.minimum(pos + 1, w).astype(F32))
    _pool_tail(s_list, u, cnts, w_ref, sc_ref, y_ref, True)


def _pool_prompt(u3, w_pool, pool_scale, tm):
    b, t, _ = u3.shape
    return pl.pallas_call(
        functools.partial(_pool_prompt_kernel, tm=tm),
        grid=(b, t // tm),
        in_specs=[pl.BlockSpec((1, tm, POOL_WIDTH), lambda i, j: (i, j, 0)),
                  pl.BlockSpec((4, POOL_GC, POOL_GC), lambda i, j: (0, 0, 0)),
                  pl.BlockSpec((1, POOL_WIDTH), lambda i, j: (0, 0))],
        out_specs=pl.BlockSpec((1, tm, POOL_WIDTH), lambda i, j: (i, j, 0)),
        out_shape=jax.ShapeDtypeStruct((b, t, POOL_WIDTH), F32),
        scratch_shapes=[pltpu.VMEM((tm + 16, POOL_WIDTH), F32)],
        compiler_params=_params(("arbitrary", "arbitrary")),
        name="pool_prompt",
    )(u3, w_pool, pool_scale)


def _pool_sample_kernel(st_ref, u_ref, w_ref, sc_ref, y_ref, *, pos0):
    u = u_ref[...]
    s_list, cnts = [], []
    for gi, w in enumerate(POOL_WINDOWS):
        cols = slice(gi * POOL_GC, (gi + 1) * POOL_GC)
        s = u[:, cols]
        for k in range(1, w):
            s = s + st_ref[POOL_STATE - k, :, cols]
        s_list.append(s)
        cnts.append(float(min(pos0 + 1, w)))
    _pool_tail(s_list, u, cnts, w_ref, sc_ref, y_ref, False)


def _pool_sample(st_t, u, w_pool, pool_scale, pos0):
    n = u.shape[0]
    return pl.pallas_call(
        functools.partial(_pool_sample_kernel, pos0=pos0),
        grid=(1,),
        in_specs=[pl.BlockSpec((POOL_STATE, n, POOL_WIDTH), lambda i: (0, 0, 0)),
                  pl.BlockSpec((n, POOL_WIDTH), lambda i: (0, 0)),
                  pl.BlockSpec((4, POOL_GC, POOL_GC), lambda i: (0, 0, 0)),
                  pl.BlockSpec((1, POOL_WIDTH), lambda i: (0, 0))],
        out_specs=pl.BlockSpec((n, POOL_WIDTH), lambda i: (0, 0)),
        out_shape=jax.ShapeDtypeStruct((n, POOL_WIDTH), F32),
        compiler_params=_params(("arbitrary",)),
        name="pool_sample",
    )(st_t, u, w_pool, pool_scale)


def _compress_rows(load_half, wbig_ref, pe_ref, wpe_ref, n_chunk):
    pe_row = _dot(pe_ref[...], wpe_ref[...])[0:1]
    halves = []
    for c in range(2):
        x = jnp.concatenate([load_half(c, s).astype(BF16) for s in range(CMP_STRIDE)], axis=1)
        y = jnp.dot(x, wbig_ref[c], preferred_element_type=F32)
        halves.append(y[:, :LANE] + pltpu.roll(y[:, LANE:], n_chunk - 1, axis=0))
    return jnp.concatenate(halves, axis=1) + pe_row


def _compress_prompt_kernel(k_ref, v_ref, wbig_ref, pe_ref, wpe_ref, out_ref, *, n_chunk):
    load = lambda c, s: (k_ref, v_ref)[c][0, pl.ds(s, n_chunk, stride=CMP_STRIDE), :]
    out_ref[0] = _compress_rows(load, wbig_ref, pe_ref, wpe_ref, n_chunk)


def _compress_prompt(kv4, wbig, pe8, wpe):
    b, t, _ = kv4.shape
    n_chunk = t // CMP_STRIDE
    return pl.pallas_call(
        functools.partial(_compress_prompt_kernel, n_chunk=n_chunk),
        grid=(b,),
        in_specs=[pl.BlockSpec((1, t, LANE), lambda i: (i, 0, 0)),
                  pl.BlockSpec((1, t, LANE), lambda i: (i, 0, 1)),
                  pl.BlockSpec(wbig.shape, lambda i: (0, 0, 0)),
                  pl.BlockSpec(pe8.shape, lambda i: (0, 0)),
                  pl.BlockSpec(wpe.shape, lambda i: (0, 0))],
        out_specs=pl.BlockSpec((1, n_chunk, KV_W), lambda i: (i, 0, 0)),
        out_shape=jax.ShapeDtypeStruct((b, n_chunk, KV_W), F32),
        compiler_params=_params(("arbitrary",)),
        name="compress_prompt",
    )(kv4, kv4, wbig, pe8, wpe)


def _stack_group_queries(q, g):
    lane = lax.broadcasted_iota(I32, (q.shape[0], LANE), 1)
    parts = []
    for hh in range(GROUP):
        h = GROUP * g + hh
        ch = q[:, (h // 2) * LANE:(h // 2 + 1) * LANE]
        if h % 2 != g:
            ch = pltpu.roll(ch, HEAD_DIM, axis=1)
        parts.append(jnp.where((lane >> 6) == g, ch, 0.0))
    return jnp.concatenate(parts, axis=0).astype(BF16)


def _place_heads(o_g, weights, g, outs):
    lane = lax.broadcasted_iota(I32, (QBLK, LANE), 1)
    for hh in range(GROUP):
        h = GROUP * g + hh
        rs = slice(hh * QBLK, (hh + 1) * QBLK)
        piece = sum(w_[:, 0:1] * o_[rs] for w_, o_ in zip(weights(h), o_g))
        if h % 2 != g:
            piece = pltpu.roll(piece, HEAD_DIM, axis=1)
        piece = jnp.where((lane >> 6) == (h % 2), piece, 0.0)
        outs[h // 2] = piece if outs[h // 2] is None else outs[h // 2] + piece


def _nsa_dense_kernel(q_ref, gt_ref, kvc_ref, kw_ref, vw_ref, part_ref, sel_ref, *, t_len):
    i = pl.program_id(1)
    t0 = i * QBLK
    nc = t_len // CMP_STRIDE
    ns = t_len // SEL_BLOCK
    wl = min(WINDOW + QBLK, t_len)
    rows = GROUP * QBLK
    q = q_ref[0] * (HEAD_DIM ** -0.5)
    gt = gt_ref[0]
    pos_q = t0 + lax.broadcasted_iota(I32, (QBLK, 1), 0)

    oj = lax.broadcasted_iota(I32, (ns, nc), 0)
    on = lax.broadcasted_iota(I32, (ns, nc), 1)
    ovl_t = jnp.where((on * CMP_STRIDE + CMP_LEN - 1 >= oj * SEL_BLOCK)
                      & (on * CMP_STRIDE <= oj * SEL_BLOCK + SEL_BLOCK - 1), 1.0, 0.0).astype(BF16)
    jb = lax.broadcasted_iota(I32, (ns, QBLK), 0)
    jq = (t0 + lax.broadcasted_iota(I32, (ns, QBLK), 1)) >> 6
    forced = (jb == 0) | (jb == jq) | (jb == jq - 1)

    kc_all = kvc_ref[0, :, 0:LANE].astype(BF16)
    vc_all = kvc_ref[0, :, LANE:2 * LANE].astype(BF16)
    ws = pl.multiple_of(jnp.maximum(t0 - WINDOW, 0), QBLK)
    kwc = kw_ref[0, :, pl.ds(ws, wl)].astype(BF16)
    vwc = vw_ref[0, :, pl.ds(ws, wl)].astype(BF16)

    ncol = lax.broadcasted_iota(I32, (1, nc), 1)
    bias_c = jnp.where((ncol * CMP_STRIDE + CMP_LEN - 1) <= pos_q, 0.0, NEG)
    bias_c = jnp.concatenate([bias_c] * GROUP, axis=0)
    dpos = pos_q - (ws + lax.broadcasted_iota(I32, (1, wl), 1))
    bias_w = jnp.where((dpos >= 0) & (dpos <= WINDOW), 0.0, NEG)
    bias_w = jnp.concatenate([bias_w] * GROUP, axis=0)

    def softmax_parts(s):
        m = jnp.max(s, axis=-1, keepdims=True)
        e = jnp.exp(s - m)
        return e, jnp.where(m > 0.5 * NEG, 1.0 / jnp.sum(e, axis=-1, keepdims=True), 0.0)

    outs = [None] * (NSA_WIDTH // LANE)
    for g in range(KV_HEADS):
        qg = _stack_group_queries(q, g)

        e_c, inv_c = softmax_parts(_dot_nt(qg, kc_all) + bias_c)
        o_c = _dot(e_c, vc_all) * inv_c

        p_c = e_c * inv_c
        psum = p_c[0:QBLK] + p_c[QBLK:2 * QBLK] + p_c[2 * QBLK:3 * QBLK] + p_c[3 * QBLK:4 * QBLK]
        imp_t = sum(lax.dot_general(ovl_t, part, (((1,), (1,)), ((), ())), preferred_element_type=F32)
                    for part in _split3(psum))
        x = jnp.where(forced, jnp.inf, jnp.where(jb > jq, -jnp.inf, imp_t))
        cnt = jnp.zeros((ns, QBLK), I32)
        for jp in range(ns):
            row = x[jp:jp + 1, :]
            beats = (row > x) | ((row == x) & (jb > jp))
            cnt = cnt + jnp.where(beats, 1, 0)
        sel_ref[0, 0, g * ns:(g + 1) * ns, :] = jnp.where(cnt < N_SELECT, 1.0, 0.0)

        e_w, inv_w = softmax_parts(_dot(qg, kwc) + bias_w)
        o_w = _dot_nt(e_w, vwc) * inv_w

        _place_heads((o_c, o_w), lambda h: (gt[:, 3 * h:3 * h + 1], gt[:, 3 * h + 2:3 * h + 3]), g, outs)
    part_ref[0] = jnp.concatenate(outs, axis=1)


def _nsa_select_kernel(q_ref, gt_ref, sel_ref, part_ref, ks_ref, vs_ref, o_ref, *, t_len, kc):
    i = pl.program_id(1)
    t0 = i * QBLK
    ns = t_len // SEL_BLOCK
    rows = GROUP * QBLK
    gt = gt_ref[0]
    n_kchunk = (t0 + QBLK + kc - 1) // kc
    blk_per_chunk = kc // SEL_BLOCK
    part = part_ref[0]
    outs = [part[:, c * LANE:(c + 1) * LANE] for c in range(NSA_WIDTH // LANE)]
    q_t = (q_ref[0] * (HEAD_DIM ** -0.5)).T
    zero_half = jnp.zeros((HEAD_DIM, QBLK), F32)
    key_row = lax.broadcasted_iota(I32, (kc, QBLK), 0)
    qry_pos = t0 + lax.broadcasted_iota(I32, (kc, QBLK), 1)
    for g in range(KV_HEADS):
        heads = []
        for hh in range(GROUP):
            q_h = q_t[(GROUP * g + hh) * HEAD_DIM:(GROUP * g + hh + 1) * HEAD_DIM]
            heads.append(jnp.concatenate([q_h, zero_half] if g == 0 else [zero_half, q_h], axis=0))
        qg_t = jnp.concatenate(heads, axis=1).astype(BF16)

        def sel_chunk(c, carry):
            m_prev, l_prev, acc = carry
            k0 = pl.multiple_of(c * kc, kc)
            s = _dot(ks_ref[0, pl.ds(k0, kc), :], qg_t)
            j0 = pl.multiple_of(g * ns + c * blk_per_chunk, blk_per_chunk)
            chosen = sel_ref[0, 0, pl.ds(j0, blk_per_chunk), :]
            chosen = jnp.concatenate([jnp.broadcast_to(chosen[j:j + 1], (SEL_BLOCK, QBLK))
                                      for j in range(blk_per_chunk)], axis=0)
            bias = jnp.where((chosen > 0.5) & (k0 + key_row <= qry_pos), 0.0, NEG)
            s = s + jnp.concatenate([bias] * GROUP, axis=1)
            m_new = jnp.maximum(m_prev, jnp.max(s, axis=0, keepdims=True))
            alpha = jnp.exp(m_prev - m_new)
            p = jnp.exp(s - m_new)
            l_new = alpha * l_prev + jnp.sum(p, axis=0, keepdims=True)
            return m_new, l_new, alpha * acc + _dot(vs_ref[0, :, pl.ds(k0, kc)], p)

        _, l_s, acc_s = lax.fori_loop(
            0, n_kchunk, sel_chunk,
            (jnp.full((1, rows), NEG, F32), jnp.zeros((1, rows), F32), jnp.zeros((LANE, rows), F32)))
        o_t = acc_s / l_s
        o_s = jnp.concatenate([o_t[:, hh * QBLK:(hh + 1) * QBLK].T for hh in range(GROUP)], axis=0)
        _place_heads((o_s,), lambda h: (gt[:, 3 * h + 1:3 * h + 2],), g, outs)
    o_ref[0] = jnp.concatenate(outs, axis=1)


def _nsa_prompt(q3, gt3, kvc, ks_tok, kv4t, kvwt, kc):
    b, t, _ = q3.shape
    assert t % kc == 0 and t % QBLK == 0
    nq = t // QBLK
    ns = t // SEL_BLOCK
    slab = lambda c: pl.BlockSpec((1, LANE, t), lambda i, j: (i, c, 0))
    tile = lambda w_: pl.BlockSpec((1, QBLK, w_), lambda i, j: (i, j, 0))
    sel_spec = pl.BlockSpec((1, 1, KV_HEADS * ns, QBLK), lambda i, j: (i, j, 0, 0))
    part, sel = pl.pallas_call(
        functools.partial(_nsa_dense_kernel, t_len=t),
        grid=(b, nq),
        in_specs=[tile(NSA_WIDTH), tile(GATE_PAD),
                  pl.BlockSpec((1, t // CMP_STRIDE, KV_W), lambda i, j: (i, 0, 0)),
                  slab(0), slab(1)],
        out_specs=[tile(NSA_WIDTH), sel_spec],
        out_shape=[jax.ShapeDtypeStruct((b, t, NSA_WIDTH), F32),
                   jax.ShapeDtypeStruct((b, nq, KV_HEADS * ns, QBLK), F32)],
        compiler_params=_params(("arbitrary", "arbitrary")),
        name="nsa_dense",
    )(q3, gt3, kvc, kvwt, kvwt)
    return pl.pallas_call(
        functools.partial(_nsa_select_kernel, t_len=t, kc=kc),
        grid=(b, nq),
        in_specs=[tile(NSA_WIDTH), tile(GATE_PAD), sel_spec, tile(NSA_WIDTH),
                  pl.BlockSpec((1, t, LANE), lambda i, j: (i, 0, 0)), slab(3)],
        out_specs=tile(NSA_WIDTH),
        out_shape=jax.ShapeDtypeStruct((b, t, NSA_WIDTH), F32),
        compiler_params=_params(("arbitrary", "arbitrary")),
        name="nsa_select",
    )(q3, gt3, sel, part, ks_tok, kv4t)


def _xattn_prompt_kernel(x_ref, yp_ref, yn_ref, wo_ref, g_ref, wq_ref, mem_ref, wxo_ref, o_ref):
    y = jnp.concatenate([yp_ref[0], yn_ref[0]], axis=1)
    h1 = x_ref[0] + _dot(y, wo_ref[...])
    qx = _dot(_rms(h1, g_ref[...]), wq_ref[...]) * (X_HEAD_DIM ** -0.5)
    xw = X_HEADS * X_HEAD_DIM
    outs = []
    for h in range(X_HEADS):
        cs = slice(h * X_HEAD_DIM, (h + 1) * X_HEAD_DIM)
        s = _dot_nt(qx[:, cs], mem_ref[0, :, cs])
        e = jnp.exp(s - jnp.max(s, axis=-1, keepdims=True))
        p = e / jnp.sum(e, axis=-1, keepdims=True)
        outs.append(_dot(p, mem_ref[0, :, xw + h * X_HEAD_DIM:xw + (h + 1) * X_HEAD_DIM]))
    o_ref[0] = h1 + _dot(jnp.concatenate(outs, axis=1), wxo_ref[...])


def _xattn_prompt(x3, yp, yn, w_out, g_x, w_xq, mem_kv, w_xo, tm):
    b, t, d = x3.shape
    m = mem_kv.shape[1]
    full = lambda a: pl.BlockSpec(a.shape, lambda i, j: (0,) * a.ndim)
    return pl.pallas_call(
        _xattn_prompt_kernel,
        grid=(b, t // tm),
        in_specs=[pl.BlockSpec((1, tm, d), lambda i, j: (i, j, 0)),
                  pl.BlockSpec((1, tm, POOL_WIDTH), lambda i, j: (i, j, 0)),
                  pl.BlockSpec((1, tm, NSA_WIDTH), lambda i, j: (i, j, 0)),
                  full(w_out), full(g_x), full(w_xq),
                  pl.BlockSpec((1, m, mem_kv.shape[2]), lambda i, j: (i, 0, 0)),
                  full(w_xo)],
        out_specs=pl.BlockSpec((1, tm, d), lambda i, j: (i, j, 0)),
        out_shape=jax.ShapeDtypeStruct((b, t, d), F32),
        compiler_params=_params(("arbitrary", "arbitrary")),
        name="xattn_prompt",
    )(x3, yp, yn, w_out, g_x, w_xq, mem_kv, w_xo)


def _ffn_core(j, nj, h_ref, gf_ref, wg_ref, wv_ref, cw_ref, cb_ref, wd_ref, gfin_ref, y_ref,
              xn_ref, acc_ref, prev_rows):
    @pl.when(j == 0)
    def _():
        xn_ref[...] = _rms(h_ref[...], gf_ref[...]).astype(BF16)
        acc_ref[...] = jnp.zeros(acc_ref.shape, F32)

    xn = xn_ref[...]
    up_g = jnp.dot(xn, wg_ref[...], preferred_element_type=F32)
    up_v = jnp.dot(xn, wv_ref[...], preferred_element_type=F32)
    g1, g2, v1, v2 = prev_rows(up_g, up_v)
    cg = cb_ref[0:1, :] + cw_ref[0, 0:1, :] * g2 + cw_ref[1, 0:1, :] * g1 + cw_ref[2, 0:1, :] * up_g
    cv = cb_ref[1:2, :] + cw_ref[0, 1:2, :] * v2 + cw_ref[1, 1:2, :] * v1 + cw_ref[2, 1:2, :] * up_v
    act = cg / (1.0 + jnp.exp(-cg)) * cv
    acc_ref[...] += _dot(act, wd_ref[...])

    @pl.when(j == nj - 1)
    def _():
        y_ref[...] = _rms(h_ref[...] + acc_ref[...], gfin_ref[...])

    return up_g, up_v


def _ffn_prompt_kernel(h_ref, gf_ref, wg_ref, wv_ref, cw_ref, cb_ref, wd_ref, gfin_ref,
                       y_ref, st_ref, xn_ref, acc_ref, carry_ref, *, tiles_per_seq, nj, tm):
    r = pl.program_id(0)
    j = pl.program_id(1)
    row = lax.broadcasted_iota(I32, (tm, 1), 0)

    @pl.when((r % tiles_per_seq) == 0)
    def _():
        carry_ref[j] = jnp.zeros(carry_ref.shape[1:], F32)

    def prev_rows(up_g, up_v):
        res = []
        for half, up in enumerate((up_g, up_v)):
            prev = carry_ref[j, half]
            m1 = jnp.where(row == 0, prev[7:8], pltpu.roll(up, 1, axis=0))
            m2 = jnp.where(row == 0, prev[6:7], jnp.where(row == 1, prev[7:8], pltpu.roll(up, 2, axis=0)))
            res += [m1, m2]
        return res

    up_g, up_v = _ffn_core(j, nj, h_ref, gf_ref, wg_ref, wv_ref, cw_ref, cb_ref, wd_ref, gfin_ref,
                           y_ref, xn_ref, acc_ref, prev_rows)
    for half, up in enumerate((up_g, up_v)):
        carry_ref[j, half] = up[tm - 8:tm]
        st_ref[0, half] = up[tm - 8:tm]


def _ffn_specs(d, tf, nj):
    return [pl.BlockSpec((1, d), lambda r, j: (0, 0)),
            pl.BlockSpec((d, tf), lambda r, j: (0, j)),
            pl.BlockSpec((d, tf), lambda r, j: (0, nj + j)),
            pl.BlockSpec((3, 2, tf), lambda r, j: (0, 0, j)),
            pl.BlockSpec((2, tf), lambda r, j: (0, j)),
            pl.BlockSpec((tf, d), lambda r, j: (j, 0)),
            pl.BlockSpec((1, d), lambda r, j: (0, 0))]


def _ffn_prompt(h2, g_ffn, w_up, cw3, cb2, w_down, g_final, batch, tm, tf):
    n, d = h2.shape
    nj = D_FF // tf
    tps = n // batch // tm
    return pl.pallas_call(
        functools.partial(_ffn_prompt_kernel, tiles_per_seq=tps, nj=nj, tm=tm),
        grid=(n // tm, nj),
        in_specs=[pl.BlockSpec((tm, d), lambda r, j: (r, 0))] + _ffn_specs(d, tf, nj),
        out_specs=[pl.BlockSpec((tm, d), lambda r, j: (r, 0)),
                   pl.BlockSpec((1, 2, 8, tf), lambda r, j: (r, 0, 0, j))],
        out_shape=[jax.ShapeDtypeStruct((n, d), F32),
                   jax.ShapeDtypeStruct((n // tm, 2, 8, D_FF), F32)],
        scratch_shapes=[pltpu.VMEM((tm, d), BF16), pltpu.VMEM((tm, d), F32),
                        pltpu.VMEM((nj, 2, 8, tf), F32)],
        compiler_params=_params(("arbitrary", "arbitrary")),
        name="ffn_prompt",
    )(h2, g_ffn, w_up, w_up, cw3, cb2, w_down, g_final)


def _ffn_sample_kernel(h_ref, gf_ref, wg_ref, wv_ref, cw_ref, cb_ref, wd_ref, gfin_ref, sg_ref, sv_ref,
                       y_ref, up_ref, xn_ref, acc_ref, *, nj):
    j = pl.program_id(1)
    prev_rows = lambda up_g, up_v: (sg_ref[1], sg_ref[0], sv_ref[1], sv_ref[0])
    up_g, up_v = _ffn_core(j, nj, h_ref, gf_ref, wg_ref, wv_ref, cw_ref, cb_ref, wd_ref, gfin_ref,
                           y_ref, xn_ref, acc_ref, prev_rows)
    up_ref[0] = up_g
    up_ref[1] = up_v


def _ffn_sample(h2, g_ffn, w_up, cw3, cb2, w_down, g_final, st_t, tf):
    n, d = h2.shape
    nj = D_FF // tf
    return pl.pallas_call(
        functools.partial(_ffn_sample_kernel, nj=nj),
        grid=(1, nj),
        in_specs=[pl.BlockSpec((n, d), lambda r, j: (0, 0))] + _ffn_specs(d, tf, nj)
        + [pl.BlockSpec((2, n, tf), lambda r, j: (0, 0, j)),
           pl.BlockSpec((2, n, tf), lambda r, j: (0, 0, nj + j))],
        out_specs=[pl.BlockSpec((n, d), lambda r, j: (0, 0)),
                   pl.BlockSpec((2, n, tf), lambda r, j: (0, 0, j))],
        out_shape=[jax.ShapeDtypeStruct((n, d), F32),
                   jax.ShapeDtypeStruct((2, n, D_FF), F32)],
        scratch_shapes=[pltpu.VMEM((n, d), BF16), pltpu.VMEM((n, d), F32)],
        compiler_params=_params(("arbitrary", "arbitrary")),
        name="ffn_sample",
    )(h2, g_ffn, w_up, w_up, cw3, cb2, w_down, g_final, st_t, st_t)


def _head_rows(qrow):
    rowi = lax.broadcasted_iota(I32, (N_HEADS, LANE), 0)
    lane = lax.broadcasted_iota(I32, (N_HEADS, LANE), 1)
    x = jnp.zeros((N_HEADS, LANE), F32)
    for c in range(NSA_WIDTH // LANE):
        x = jnp.where((rowi >> 1) == c, jnp.broadcast_to(qrow[:, c * LANE:(c + 1) * LANE], (N_HEADS, LANE)), x)
    x = jnp.where((rowi & 1) != (rowi >> 2), pltpu.roll(x, HEAD_DIM, axis=1), x)
    return jnp.where((lane >> 6) == (rowi >> 2), x, 0.0)


def _page_copy(cache_ref, raw_ref, sem_ref, phys, p, slot):
    return pltpu.make_async_copy(cache_ref.at[phys, pl.ds(0, KV_W), :], raw_ref.at[slot, p], sem_ref.at[slot, p])


def _nsa_sample_cmp_kernel(pt_ref, q_ref, cache_ref, wbig_ref, pe_ref, wpe_ref, oc_ref, imp_ref,
                           raw_ref, tok_ref, sem_ref, *, n_pages, n_batch, page):
    b = pl.program_id(0)
    slot = b % 2
    n_tok = n_pages * page
    n_chunk = n_tok // CMP_STRIDE

    def start_all(bb, sl):
        def body(p, _):
            _page_copy(cache_ref, raw_ref, sem_ref, pt_ref[bb * n_pages + p], p, sl).start()
            return 0
        lax.fori_loop(0, n_pages, body, 0)

    @pl.when(b == 0)
    def _():
        start_all(0, 0)

    @pl.when(b + 1 < n_batch)
    def _():
        start_all(b + 1, 1 - slot)

    chunks_per_page = page // CMP_STRIDE

    def to_token_major(pg, _):
        for pp in range(PAGE_UNROLL):
            _page_copy(cache_ref, raw_ref, sem_ref, 0, pg * PAGE_UNROLL + pp, slot).wait()
        for pp in range(PAGE_UNROLL):
            p = pg * PAGE_UNROLL + pp
            xt = raw_ref[slot, p]
            for half in range(2):
                x = xt[half * LANE:(half + 1) * LANE].T
                for nl in range(chunks_per_page):
                    r0 = pl.multiple_of((p * chunks_per_page + nl) * CHUNK_PITCH, 8)
                    tok_ref[half, pl.ds(r0, CMP_STRIDE), :] = x[nl * CMP_STRIDE:(nl + 1) * CMP_STRIDE]
        return 0
    lax.fori_loop(0, n_pages // PAGE_UNROLL, to_token_major, 0)

    load = lambda c, s: tok_ref[c, pl.ds(s, n_chunk, stride=CHUNK_PITCH), :]
    kvc = _compress_rows(load, wbig_ref, pe_ref, wpe_ref, n_chunk)
    q8 = _head_rows(q_ref[0] * (HEAD_DIM ** -0.5))
    s = _dot_nt(q8, kvc[:, 0:LANE])
    ncol = lax.broadcasted_iota(I32, (1, n_chunk), 1)
    p = _masked_softmax(s, (ncol * CMP_STRIDE + CMP_LEN - 1) <= n_tok)
    oc_ref[0] = _dot(p, kvc[:, LANE:2 * LANE])
    rowi = lax.broadcasted_iota(I32, (N_HEADS, n_chunk), 0)
    psum = jnp.where(rowi == 0, jnp.sum(p[0:GROUP], axis=0, keepdims=True),
                     jnp.where(rowi == 1, jnp.sum(p[GROUP:2 * GROUP], axis=0, keepdims=True), 0.0))
    ns_pad = imp_ref.shape[2]
    on = lax.broadcasted_iota(I32, (n_chunk, ns_pad), 0)
    oj = lax.broadcasted_iota(I32, (n_chunk, ns_pad), 1)
    ovl = jnp.where((on * CMP_STRIDE + CMP_LEN - 1 >= oj * SEL_BLOCK)
                    & (on * CMP_STRIDE <= oj * SEL_BLOCK + SEL_BLOCK - 1), 1.0, 0.0).astype(BF16)
    imp_ref[0] = sum(jnp.dot(part, ovl, preferred_element_type=F32) for part in _split3(psum))


def _nsa_sample_cmp(pt_flat, q3, cache3, wbig, pe8, wpe, n_pages, ns_pad):
    nb = q3.shape[0]
    page = cache3.shape[2]
    assert page == LANE
    n_tok = n_pages * page
    gs = pltpu.PrefetchScalarGridSpec(
        num_scalar_prefetch=1, grid=(nb,),
        in_specs=[pl.BlockSpec((1, 1, NSA_WIDTH), lambda i, pt: (i, 0, 0)),
                  pl.BlockSpec(memory_space=pl.ANY),
                  pl.BlockSpec(wbig.shape, lambda i, pt: (0, 0, 0)),
                  pl.BlockSpec(pe8.shape, lambda i, pt: (0, 0)),
                  pl.BlockSpec(wpe.shape, lambda i, pt: (0, 0))],
        out_specs=[pl.BlockSpec((1, N_HEADS, LANE), lambda i, pt: (i, 0, 0)),
                   pl.BlockSpec((1, N_HEADS, ns_pad), lambda i, pt: (i, 0, 0))],
        scratch_shapes=[pltpu.VMEM((2, n_pages, KV_W, page), F32), pltpu.VMEM((2, n_tok // CMP_STRIDE * CHUNK_PITCH, LANE), F32),
                        pltpu.SemaphoreType.DMA((2, n_pages))])
    return pl.pallas_call(
        functools.partial(_nsa_sample_cmp_kernel, n_pages=n_pages, n_batch=nb, page=page),
        grid_spec=gs,
        out_shape=[jax.ShapeDtypeStruct((nb, N_HEADS, LANE), F32),
                   jax.ShapeDtypeStruct((nb, N_HEADS, ns_pad), F32)],
        compiler_params=_params(("arbitrary",)),
        name="nsa_sample_cmp",
    )(pt_flat, q3, cache3, wbig, pe8, wpe)


def _topk_sample_kernel(imp_ref, idx_ref, *, jq, k_past):
    x = imp_ref[...]
    col = lax.broadcasted_iota(I32, x.shape, 1)
    x = jnp.where((col == 0) | (col == jq - 1), jnp.inf, jnp.where(col >= jq, -jnp.inf, x))
    lane = lax.broadcasted_iota(I32, idx_ref.shape, 1)
    out = jnp.zeros(idx_ref.shape, I32)
    big = jnp.int32(1 << 30)
    for k in range(k_past):
        m = jnp.max(x, axis=-1, keepdims=True)
        idx = jnp.min(jnp.where(x == m, col, big), axis=-1, keepdims=True)
        out = jnp.where(lane == k, idx, out)
        x = jnp.where(col == idx, -jnp.inf, x)
    idx_ref[...] = out


def _topk_sample(imp2, jq, k_past):
    n, w = imp2.shape
    return pl.pallas_call(
        functools.partial(_topk_sample_kernel, jq=jq, k_past=k_past),
        grid=(1,),
        in_specs=[pl.BlockSpec((n, w), lambda i: (0, 0))],
        out_specs=pl.BlockSpec((n, LANE), lambda i: (0, 0)),
        out_shape=jax.ShapeDtypeStruct((n, LANE), I32),
        compiler_params=_params(("arbitrary",)),
        name="topk_sample",
    )(imp2)


def _blk_copy(cache_ref, buf_ref, sem_ref, phys, kv, g, r, slot):
    row0 = KV_W + kv * LANE + g * HEAD_DIM
    return pltpu.make_async_copy(cache_ref.at[phys, pl.ds(row0, HEAD_DIM), :],
                                 buf_ref.at[slot, kv, pl.ds(g * HEAD_DIM, HEAD_DIM), pl.ds(r * LANE, LANE)],
                                 sem_ref.at[slot])


def _nsa_sample_sel_kernel(idx_ref, pt_ref, q_ref, gt_ref, oc_ref, win_ref, kvn4_ref, kvnw_ref, cache_ref,
                           o_ref, wout_ref, buf_ref, sem_ref, *, n_pages, n_batch, page, k_past):
    b = pl.program_id(0)
    slot = b % 2
    per_page = page // SEL_BLOCK
    nl = k_past * page

    def for_blocks(bb, fn):
        def body(r, _):
            for g in range(KV_HEADS):
                j = idx_ref[(bb * KV_HEADS + g) * k_past + r]
                phys = pt_ref[bb * n_pages + j // per_page]
                for kv in range(2):
                    fn(phys, kv, g, r)
            return 0
        lax.fori_loop(0, k_past, body, 0)

    def start_all(bb, sl):
        for_blocks(bb, lambda phys, kv, g, r: _blk_copy(cache_ref, buf_ref, sem_ref, phys, kv, g, r, sl).start())

    @pl.when(b == 0)
    def _():
        start_all(0, 0)

    @pl.when(b + 1 < n_batch)
    def _():
        start_all(b + 1, 1 - slot)

    def wait_body(r, _):
        for g in range(KV_HEADS):
            for kv in range(2):
                _blk_copy(cache_ref, buf_ref, sem_ref, 0, kv, g, r, slot).wait()
        return 0
    lax.fori_loop(0, k_past, wait_body, 0)

    q8 = _head_rows(q_ref[0] * (HEAD_DIM ** -0.5))
    rowi = lax.broadcasted_iota(I32, (N_HEADS, 1), 0)
    lane = lax.broadcasted_iota(I32, (N_HEADS, LANE), 1)
    own = (lane >> 6) == (rowi >> 2)
    mine = jnp.where(lax.broadcasted_iota(I32, (N_HEADS, n_batch), 1) == b, 0.0, -jnp.inf)

    def attend(k_t, v_t, bias):
        s = _dot(q8, k_t) + bias
        e = jnp.exp(s - jnp.max(s, axis=-1, keepdims=True))
        return _dot_nt(e, v_t) / jnp.sum(e, axis=-1, keepdims=True)

    lane_s = lax.broadcasted_iota(I32, (1, nl), 1)
    halves = []
    for g in range(KV_HEADS):
        hv = jnp.zeros((1, nl), I32)
        for r in range(k_past):
            hv = jnp.where((lane_s // page) == r, idx_ref[(b * KV_HEADS + g) * k_past + r] % per_page, hv)
        halves.append(hv)
    vis = ((lane_s % page) // SEL_BLOCK) == jnp.where(rowi < GROUP, halves[0], halves[1])
    o_s = attend(jnp.concatenate([buf_ref[slot, 0], kvn4_ref[0, KV_W:KV_W + LANE]], axis=1),
                 jnp.concatenate([buf_ref[slot, 1], kvn4_ref[0, KV_W + LANE:2 * KV_W]], axis=1),
                 jnp.concatenate([jnp.where(vis, 0.0, -jnp.inf), mine], axis=1))

    cw = win_ref[0]
    wlen = cw.shape[1]
    o_w = attend(jnp.concatenate([cw[0:LANE], kvnw_ref[0, 0:LANE]], axis=1),
                 jnp.concatenate([cw[LANE:2 * LANE], kvnw_ref[0, LANE:2 * LANE]], axis=1),
                 jnp.concatenate([jnp.zeros((N_HEADS, wlen), F32), mine], axis=1))
    lane_n = lax.broadcasted_iota(I32, (KV_W, n_batch), 1)
    new_col = jnp.sum(jnp.where(lane_n == b, kvnw_ref[0], 0.0), axis=1, keepdims=True)
    lane_w = lax.broadcasted_iota(I32, (KV_W, wlen), 1)
    wout_ref[0] = jnp.where(lane_w == wlen - 1, new_col, pltpu.roll(cw, wlen - 1, axis=1))

    gt = jnp.broadcast_to(gt_ref[0], (N_HEADS, GATE_PAD))
    gate = lambda br: jnp.sum(jnp.where(lane == rowi * N_BRANCH + br, gt, 0.0), axis=-1, keepdims=True)
    o = gate(0) * oc_ref[0] + gate(1) * o_s + gate(2) * o_w
    o_ref[0] = jnp.where(own, o, 0.0)


def _nsa_sample_sel(idx_flat, pt_flat, q3, gt3, oc, win_t, kvn4_t, kvnw_t, cache3, n_pages, k_past):
    nb = q3.shape[0]
    page = cache3.shape[2]
    wlen = win_t.shape[2]
    assert page == LANE
    per = lambda *blk: pl.BlockSpec((1,) + blk, lambda i, a, c: (i, 0, 0))
    whole = lambda arr: pl.BlockSpec(arr.shape, lambda i, a, c: (0, 0, 0))
    gs = pltpu.PrefetchScalarGridSpec(
        num_scalar_prefetch=2, grid=(nb,),
        in_specs=[per(1, NSA_WIDTH), per(1, GATE_PAD), per(N_HEADS, LANE), per(KV_W, wlen),
                  whole(kvn4_t), whole(kvnw_t), pl.BlockSpec(memory_space=pl.ANY)],
        out_specs=[per(N_HEADS, LANE), per(KV_W, wlen)],
        scratch_shapes=[pltpu.VMEM((2, 2, LANE, k_past * page), F32), pltpu.SemaphoreType.DMA((2,))])
    return pl.pallas_call(
        functools.partial(_nsa_sample_sel_kernel, n_pages=n_pages, n_batch=nb, page=page, k_past=k_past),
        grid_spec=gs,
        out_shape=[jax.ShapeDtypeStruct((nb, N_HEADS, LANE), F32),
                   jax.ShapeDtypeStruct((nb, KV_W, wlen), F32)],
        compiler_params=_params(("arbitrary",)),
        name="nsa_sample_sel",
    )(idx_flat, pt_flat, q3, gt3, oc, win_t, kvn4_t, kvnw_t, cache3)


def _mix_sample_kernel(x_ref, yp_ref, yn_ref, wo_ref, g_ref, wq_ref, h1_ref, qx_ref):
    y = jnp.concatenate([yp_ref[...], yn_ref[...]], axis=1)
    h1 = x_ref[...] + _dot(y, wo_ref[...])
    h1_ref[...] = h1
    qx_ref[...] = _dot(_rms(h1, g_ref[...]), wq_ref[...]) * (X_HEAD_DIM ** -0.5)


def _mix_sample(x, yp, yn, w_out, g_x, w_xq):
    n, d = x.shape
    full = lambda a: pl.BlockSpec(a.shape, lambda i: (0,) * a.ndim)
    args = (x, yp, yn, w_out, g_x, w_xq)
    return pl.pallas_call(
        _mix_sample_kernel, grid=(1,),
        in_specs=[full(a) for a in args],
        out_specs=[pl.BlockSpec((n, d), lambda i: (0, 0))] * 2,
        out_shape=[jax.ShapeDtypeStruct((n, d), F32)] * 2,
        compiler_params=_params(("arbitrary",)),
        name="mix_sample",
    )(*args)


def _xattn_sample_kernel(qx_ref, mem_ref, o_ref):
    n_mem = mem_ref.shape[1]
    n = n_mem * 8
    q = qx_ref[0]
    order = [(half, h) for half in range(X_HEAD_DIM // LANE) for h in range(X_HEADS)]
    col = lambda half, h: slice(h * X_HEAD_DIM + half * LANE, h * X_HEAD_DIM + (half + 1) * LANE)
    qt = jnp.concatenate([q[:, col(half, h)] for half, h in order], axis=0)
    x = mem_ref[0]
    a = _dot_nt(qt, x[:, 0].reshape(n, LANE))
    rowi = lax.broadcasted_iota(I32, (8, n), 0)
    lane = lax.broadcasted_iota(I32, (8, n), 1)
    own = (lane & 7) == rowi
    a = jnp.where(own, a, 0.0)
    other = pltpu.roll(a, X_HEADS, axis=0)
    s = a + jnp.where(rowi < X_HEADS, pltpu.roll(other, n - X_HEADS, axis=1), pltpu.roll(other, X_HEADS, axis=1))
    s = jnp.where(own, s, -jnp.inf)
    e = jnp.exp(s - jnp.max(s, axis=-1, keepdims=True))
    p = e / jnp.sum(e, axis=-1, keepdims=True)
    o8 = _dot(p, x[:, 1].reshape(n, LANE))
    o_ref[0] = jnp.concatenate([o8[half * X_HEADS + h:half * X_HEADS + h + 1]
                                for h in range(X_HEADS) for half in range(X_HEAD_DIM // LANE)], axis=1)


def _xattn_sample(qx3, mem5):
    nb, _, xw = qx3.shape
    return pl.pallas_call(
        _xattn_sample_kernel, grid=(nb,),
        in_specs=[pl.BlockSpec((1, 1, xw), lambda i: (i, 0, 0)),
                  pl.BlockSpec((1,) + mem5.shape[1:], lambda i: (i, 0, 0, 0, 0))],
        out_specs=pl.BlockSpec((1, 1, xw), lambda i: (i, 0, 0)),
        out_shape=jax.ShapeDtypeStruct((nb, 1, xw), F32),
        compiler_params=_params(("arbitrary",)),
        name="xattn_sample",
    )(qx3, mem5)


def _resid_proj_kernel(h_ref, o_ref, w_ref, y_ref):
    y_ref[...] = h_ref[...] + _dot(o_ref[...], w_ref[...])


def _resid_proj(h, o, w):
    n, d = h.shape
    full = lambda a: pl.BlockSpec(a.shape, lambda i: (0,) * a.ndim)
    return pl.pallas_call(
        _resid_proj_kernel, grid=(1,),
        in_specs=[full(h), full(o), full(w)],
        out_specs=pl.BlockSpec((n, d), lambda i: (0, 0)),
        out_shape=jax.ShapeDtypeStruct((n, d), F32),
        compiler_params=_params(("arbitrary",)),
        name="resid_proj",
    )(h, o, w)


def _prep_weights(g_mix, w_in, w_pool, pool_scale, w_cmp, pe_cmp, w_out, g_xattn, g_mem, w_xq, w_xkv,
                  w_xo, g_ffn, w_up, conv_w, conv_b, w_down, g_final):
    o_kv = POOL_WIDTH + NSA_WIDTH
    o_gate = o_kv + N_BRANCH * KV_W
    w_tok = jnp.concatenate([w_in[:, :o_kv + KV_W + LANE], jnp.pad(w_in[:, o_gate:], ((0, 0), (0, GATE_PAD - N_GATE)))],
                            axis=1)
    r = CMP_LEN // CMP_STRIDE
    w5 = w_cmp.reshape(2, r, CMP_STRIDE, HEAD_DIM, HEAD_DIM)
    eye = jnp.eye(2, dtype=F32)
    wbig = jnp.einsum('cisde,gG->csgdiGe', w5, eye).reshape(2, CMP_STRIDE * LANE, r * LANE)
    wpe = jnp.einsum('cke,cC->ckCe', w_cmp.reshape(2, CMP_LEN * HEAD_DIM, HEAD_DIM), eye)
    wpe = jnp.broadcast_to(wpe[:, :, :, None, :], (2, CMP_LEN * HEAD_DIM, 2, KV_HEADS, HEAD_DIM))
    row = lambda v: v.reshape(1, -1)
    return dict(
        g_mix=row(g_mix), w_tok=w_tok.astype(BF16), w_feat=w_in[:, o_kv:o_gate].T.astype(BF16),
        w_pool=w_pool.astype(BF16), pool_scale=row(pool_scale),
        wbig=wbig.astype(BF16), wpe=wpe.reshape(2 * CMP_LEN * HEAD_DIM, KV_W).astype(BF16),
        pe8=jnp.broadcast_to(pe_cmp.reshape(1, -1), (8, 2 * CMP_LEN * HEAD_DIM)),
        w_out=w_out.astype(BF16), g_xattn=row(g_xattn), g_mem=row(g_mem), w_xq=w_xq.astype(BF16),
        w_xkv=w_xkv.astype(BF16), w_xo=w_xo.astype(BF16), g_ffn=row(g_ffn), w_up=w_up.astype(BF16),
        cw3=conv_w.reshape(3, 2, D_FF), cb2=conv_b.reshape(2, D_FF), w_down=w_down.astype(BF16),
        g_final=row(g_final))


def _token_major(a_t, lead):
    b, _, t = a_t.shape
    nd = len(lead)
    return jnp.transpose(a_t.reshape((b,) + lead + (t,)), (0, nd + 1) + tuple(range(1, nd + 1)))


def _feature_major(a, nfeat):
    nd = a.ndim
    return jnp.transpose(a, (0,) + tuple(range(2, nd)) + (1,)).reshape(a.shape[0], nfeat, a.shape[1])


def _prompt_layer(x, mem, w, *, tm=512, tf=1408, kc=512):
    b, t, d = x.shape
    n = b * t
    u, q, kvc_tok, ks_tok, gt, kv4t, kvwt = _mix_in(x, w['g_mix'], w['w_tok'], w['w_feat'], tm, "mix_in_prompt")
    r3 = lambda a: a.reshape(b, t, a.shape[-1])
    u, q, kvc_tok, ks_tok, gt = map(r3, (u, q, kvc_tok, ks_tok, gt))
    y_pool = _pool_prompt(u, w['w_pool'], w['pool_scale'], tm)
    kvc = _compress_prompt(kvc_tok, w['wbig'], w['pe8'], w['wpe'])
    y_nsa = _nsa_prompt(q, gt, kvc, ks_tok, kv4t, kvwt, kc)
    m = mem.shape[1]
    (mem_kv,) = _rms_proj(mem.reshape(b * m, d), w['g_mem'], w['w_xkv'], (2 * X_HEADS * X_HEAD_DIM,), m,
                          name="mem_project")
    mem_kv = mem_kv.reshape(b, m, 2 * X_HEADS * X_HEAD_DIM)
    h2 = _xattn_prompt(x, y_pool, y_nsa, w['w_out'], w['g_xattn'], w['w_xq'], mem_kv, w['w_xo'], tm)
    y, st = _ffn_prompt(h2.reshape(n, d), w['g_ffn'], w['w_up'], w['cw3'], w['cb2'], w['w_down'],
                        w['g_final'], b, tm, tf)
    wlen = min(WINDOW, t)
    return (y.reshape(b, t, d),
            _token_major(kv4t, (4, KV_HEADS, HEAD_DIM)),
            _token_major(kvwt[:, :, t - wlen:], (2, KV_HEADS, HEAD_DIM)),
            u[:, t - POOL_STATE:],
            jnp.swapaxes(st[t // tm - 1::t // tm, :, 6:8], 1, 2).reshape(b, 2, 2 * D_FF),
            mem_kv.reshape(b, m, 2, X_HEADS, X_HEAD_DIM))


def _sample_layer(x, cache_kv, page_table, cache_win, state_pool, state_ffn, cache_mem, w, *, tf=1408):
    nb, _, d = x.shape
    n_phys, page = cache_kv.shape[:2]
    n_pages = page_table.shape[1]
    past = n_pages * page
    u, q, _, _, gt, kv4t, kvwt = _mix_in(x.reshape(1, nb, d), w['g_mix'], w['w_tok'], w['w_feat'], nb, "mix_in_sample")
    y_pool = _pool_sample(jnp.swapaxes(state_pool, 0, 1), u, w['w_pool'], w['pool_scale'], past)
    cache3 = _feature_major(cache_kv, 2 * KV_W)
    pt_flat = page_table.reshape(-1)
    jq = past // SEL_BLOCK
    ns_pad = -(-(jq + 1) // LANE) * LANE
    k_past = min(N_SELECT, jq + 1) - 1
    q3 = q.reshape(nb, 1, NSA_WIDTH)
    oc, imp = _nsa_sample_cmp(pt_flat, q3, cache3, w['wbig'], w['pe8'], w['wpe'], n_pages, ns_pad)
    idx = _topk_sample(imp[:, :KV_HEADS].reshape(nb * KV_HEADS, ns_pad), jq, k_past)
    idx_flat = idx[:, :k_past].reshape(-1)
    o8, win_out = _nsa_sample_sel(idx_flat, pt_flat, q3, gt.reshape(nb, 1, GATE_PAD), oc,
                                  _feature_major(cache_win, KV_W), kv4t, kvwt, cache3, n_pages, k_past)
    y_nsa = jnp.concatenate([o8[:, :GROUP, :HEAD_DIM].reshape(nb, GROUP * HEAD_DIM),
                             o8[:, GROUP:, HEAD_DIM:].reshape(nb, GROUP * HEAD_DIM)], axis=1)
    h1, qx = _mix_sample(x.reshape(nb, d), y_pool, y_nsa, w['w_out'], w['g_xattn'], w['w_xq'])
    m = cache_mem.shape[1]
    halves = X_HEAD_DIM // LANE
    mem5 = jnp.transpose(cache_mem.reshape(nb, m, 2, X_HEADS, halves, LANE), (0, 1, 2, 4, 3, 5))
    ox = _xattn_sample(qx.reshape(nb, 1, d), mem5.reshape(nb, m, 2, halves * X_HEADS, LANE))
    h2 = _resid_proj(h1, ox.reshape(nb, d), w['w_xo'])
    y, up2 = _ffn_sample(h2, w['g_ffn'], w['w_up'], w['cw3'], w['cb2'], w['w_down'], w['g_final'],
                         jnp.swapaxes(state_ffn, 0, 1), tf)
    up = jnp.swapaxes(up2, 0, 1).reshape(nb, 1, 2 * D_FF)
    return (y.reshape(nb, 1, d),
            jnp.swapaxes(_token_major(kv4t, (4, KV_HEADS, HEAD_DIM)), 0, 1),
            _token_major(win_out, (2, KV_HEADS, HEAD_DIM)),
            jnp.concatenate([state_pool[:, 1:], u[:, None, :]], axis=1),
            jnp.concatenate([state_ffn[:, 1:], up], axis=1),
            )


def kernel(x_prompt, x_sample, mem_prompt, cache_kv, page_table, cache_win, state_pool, state_ffn, cache_mem, g_mix, w_in, w_pool, pool_scale, w_cmp, pe_cmp, w_out, g_xattn, g_mem, w_xq, w_xkv, w_xo, g_ffn, w_up, conv_w, conv_b, w_down, g_final):
    assert g_mix.shape[0] == 1, "single layer"
    w = _prep_weights(g_mix[0], w_in[0], w_pool[0], pool_scale[0], w_cmp[0], pe_cmp[0], w_out[0], g_xattn[0],
                      g_mem[0], w_xq[0], w_xkv[0], w_xo[0], g_ffn[0], w_up[0], conv_w[0], conv_b[0], w_down[0],
                      g_final)
    yp, kv_p, win_p, pool_p, ffn_p, mem_p = _prompt_layer(x_prompt, mem_prompt, w)
    ys, kv_s, win_s, pool_s, ffn_s = _sample_layer(x_sample, cache_kv[0], page_table, cache_win[0],
                                                    state_pool[0], state_ffn[0], cache_mem[0], w)
    lead = lambda a: a[None]
    return (yp, ys, lead(kv_p), lead(kv_s), lead(win_p), lead(win_s), lead(pool_p), lead(pool_s),
            lead(ffn_p), lead(ffn_s), lead(mem_p))
```

```python
import functools

import jax
import jax.numpy as jnp
from jax import lax
from jax.experimental import pallas as pl
from jax.experimental.pallas import tpu as pltpu

F32 = jnp.float32
BF16 = jnp.bfloat16
I32 = jnp.int32

D_MODEL = 1024
POOL_WIDTH = 512
POOL_WINDOWS = (2, 4, 8, 16)
POOL_GC = 128
POOL_STATE = 15
NSA_WIDTH = 512
HEAD_DIM = 64
N_HEADS = 8
KV_HEADS = 2
GROUP = 4
N_BRANCH = 3
CMP_LEN = 32
CMP_STRIDE = 16
SEL_BLOCK = 64
N_SELECT = 16
WINDOW = 512
QBLK = 128
KV_W = 2 * KV_HEADS * HEAD_DIM
N_GATE = N_HEADS * N_BRANCH
LANE = 128
GATE_PAD = LANE
X_HEADS = 4
X_HEAD_DIM = 256
D_FF = 2816
EPS = 1e-6
NEG = -1e30
VMEM_LIMIT = 56 * 1024 * 1024
CHUNK_PITCH = 24
PAGE_UNROLL = 16


def _dot(a, b):
    return jnp.dot(a.astype(BF16), b.astype(BF16), preferred_element_type=F32)


def _dot_nt(a, b):
    return lax.dot_general(a.astype(BF16), b.astype(BF16), (((1,), (1,)), ((), ())),
                           preferred_element_type=F32)


def _rms(x, g):
    return x * lax.rsqrt(jnp.mean(x * x, axis=-1, keepdims=True) + EPS) * g


def _split3(x):
    p0 = x.astype(BF16)
    r = x - p0.astype(F32)
    p1 = r.astype(BF16)
    p2 = (r - p1.astype(F32)).astype(BF16)
    return p0, p1, p2


def _masked_softmax(s, mask):
    s = jnp.where(mask, s, -jnp.inf)
    m = jnp.max(s, axis=-1, keepdims=True)
    m = jnp.where(m == -jnp.inf, 0.0, m)
    e = jnp.exp(s - m)
    return e / jnp.maximum(jnp.sum(e, axis=-1, keepdims=True), 1e-30)


def _params(sem):
    return pltpu.CompilerParams(dimension_semantics=sem, vmem_limit_bytes=VMEM_LIMIT)


def _rms_proj_kernel(x_ref, g_ref, w_ref, *out_refs, widths, sigmoid_last):
    xn = _rms(x_ref[...], g_ref[...]).astype(BF16)
    off = 0
    for k, (o_ref, wd) in enumerate(zip(out_refs, widths)):
        y = jnp.dot(xn, w_ref[:, off:off + wd], preferred_element_type=F32)
        if sigmoid_last and k == len(widths) - 1:
            y = 1.0 / (1.0 + jnp.exp(-y))
        o_ref[...] = y
        off += wd


def _rms_proj(x, g, w, widths, tm, sigmoid_last=False, name="rms_proj"):
    n, d = x.shape
    ntot = w.shape[1]
    assert sum(widths) == ntot and n % tm == 0
    return pl.pallas_call(
        functools.partial(_rms_proj_kernel, widths=tuple(widths), sigmoid_last=sigmoid_last),
        grid=(n // tm,),
        in_specs=[pl.BlockSpec((tm, d), lambda i: (i, 0)),
                  pl.BlockSpec((1, d), lambda i: (0, 0)),
                  pl.BlockSpec((d, ntot), lambda i: (0, 0))],
        out_specs=[pl.BlockSpec((tm, wd), lambda i: (i, 0)) for wd in widths],
        out_shape=[jax.ShapeDtypeStruct((n, wd), F32) for wd in widths],
        compiler_params=_params(("arbitrary",)),
        name=name,
    )(x, g, w)


def _mix_in_kernel(x_ref, g_ref, wt_ref, wf_ref, u_ref, q_ref, kc_ref, ks_ref, gt_ref, kv4_ref, kvw_ref):
    xn = _rms(x_ref[...], g_ref[...]).astype(BF16)
    off = 0
    for o_ref in (u_ref, q_ref, kc_ref, ks_ref, gt_ref):
        wd = o_ref.shape[-1]
        y = jnp.dot(xn, wt_ref[:, off:off + wd], preferred_element_type=F32)
        o_ref[...] = 1.0 / (1.0 + jnp.exp(-y)) if o_ref is gt_ref else y
        off += wd
    kv_t = lax.dot_general(wf_ref[...], xn, (((1,), (1,)), ((), ())), preferred_element_type=F32)
    kv4_ref[0] = kv_t[0:2 * KV_W]
    kvw_ref[0] = kv_t[2 * KV_W:3 * KV_W]


def _mix_in(x3, g, w_tok, w_feat, tm, name):
    b, t, d = x3.shape
    n = b * t
    tps = t // tm
    widths = (POOL_WIDTH, NSA_WIDTH, KV_W, LANE, GATE_PAD)
    assert sum(widths) == w_tok.shape[1] and t % tm == 0
    return pl.pallas_call(
        _mix_in_kernel,
        grid=(n // tm,),
        in_specs=[pl.BlockSpec((tm, d), lambda i: (i, 0)),
                  pl.BlockSpec((1, d), lambda i: (0, 0)),
                  pl.BlockSpec(w_tok.shape, lambda i: (0, 0)),
                  pl.BlockSpec(w_feat.shape, lambda i: (0, 0))],
        out_specs=[pl.BlockSpec((tm, wd), lambda i: (i, 0)) for wd in widths]
        + [pl.BlockSpec((1, 2 * KV_W, tm), lambda i: (i // tps, 0, i % tps)),
           pl.BlockSpec((1, KV_W, tm), lambda i: (i // tps, 0, i % tps))],
        out_shape=[jax.ShapeDtypeStruct((n, wd), F32) for wd in widths]
        + [jax.ShapeDtypeStruct((b, 2 * KV_W, t), F32), jax.ShapeDtypeStruct((b, KV_W, t), F32)],
        compiler_params=_params(("arbitrary",)),
        name=name,
    )(x3.reshape(n, d), g, w_tok, w_feat)


def _pool_tail(s_list, u, cnts, w_ref, sc_ref, y_ref, lead):
    for gi in range(len(POOL_WINDOWS)):
        cols = slice(gi * POOL_GC, (gi + 1) * POOL_GC)
        d = s_list[gi] / cnts[gi] - u[:, cols]
        y = _dot(d, w_ref[gi]) * sc_ref[:, cols]
        if lead:
            y_ref[0, :, cols] = y
        else:
            y_ref[:, cols] = y


def _pool_prompt_kernel(u_ref, w_ref, sc_ref, y_ref, ext_ref, *, tm):
    t = pl.program_id(1)
    hist = 16

    @pl.when(t == 0)
    def _():
        ext_ref[0:hist, :] = jnp.zeros((hist, POOL_WIDTH), F32)

    @pl.when(t > 0)
    def _():
        ext_ref[0:hist, :] = ext_ref[tm:tm + hist, :]

    u = u_ref[0]
    ext_ref[hist:hist + tm, :] = u
    pos = t * tm + lax.broadcasted_iota(I32, (tm, 1), 0)
    s_list, cnts = [], []
    for gi, w in enumerate(POOL_WINDOWS):
        cols = slice(gi * POOL_GC, (gi + 1) * POOL_GC)
        s = u[:, cols]
        for k in range(1, w):
            s = s + ext_ref[hist - k:hist - k + tm, cols]
        s_list.append(s)
        cnts.append(jnp.minimum(pos + 1, w).astype(F32))
    _pool_tail(s_list, u, cnts, w_ref, sc_ref, y_ref, True)


def _pool_prompt(u3, w_pool, pool_scale, tm):
    b, t, _ = u3.shape
    return pl.pallas_call(
        functools.partial(_pool_prompt_kernel, tm=tm),
        grid=(b, t // tm),
        in_specs=[pl.BlockSpec((1, tm, POOL_WIDTH), lambda i, j: (i, j, 0)),
                  pl.BlockSpec((4, POOL_GC, POOL_GC), lambda i, j: (0, 0, 0)),
                  pl.BlockSpec((1, POOL_WIDTH), lambda i, j: (0, 0))],
        out_specs=pl.BlockSpec((1, tm, POOL_WIDTH), lambda i, j: (i, j, 0)),
        out_shape=jax.ShapeDtypeStruct((b, t, POOL_WIDTH), F32),
        scratch_shapes=[pltpu.VMEM((tm + 16, POOL_WIDTH), F32)],
        compiler_params=_params(("arbitrary", "arbitrary")),
        name="pool_prompt",
    )(u3, w_pool, pool_scale)


def _pool_sample_kernel(st_ref, u_ref, w_ref, sc_ref, y_ref, *, pos0):
    u = u_ref[...]
    s_list, cnts = [], []
    for gi, w in enumerate(POOL_WINDOWS):
        cols = slice(gi * POOL_GC, (gi + 1) * POOL_GC)
        s = u[:, cols]
        for k in range(1, w):
            s = s + st_ref[POOL_STATE - k, :, cols]
        s_list.append(s)
        cnts.append(float(min(pos0 + 1, w)))
    _pool_tail(s_list, u, cnts, w_ref, sc_ref, y_ref, False)


def _pool_sample(st_t, u, w_pool, pool_scale, pos0):
    n = u.shape[0]
    return pl.pallas_call(
        functools.partial(_pool_sample_kernel, pos0=pos0),
        grid=(1,),
        in_specs=[pl.BlockSpec((POOL_STATE, n, POOL_WIDTH), lambda i: (0, 0, 0)),
                  pl.BlockSpec((n, POOL_WIDTH), lambda i: (0, 0)),
                  pl.BlockSpec((4, POOL_GC, POOL_GC), lambda i: (0, 0, 0)),
                  pl.BlockSpec((1, POOL_WIDTH), lambda i: (0, 0))],
        out_specs=pl.BlockSpec((n, POOL_WIDTH), lambda i: (0, 0)),
        out_shape=jax.ShapeDtypeStruct((n, POOL_WIDTH), F32),
        compiler_params=_params(("arbitrary",)),
        name="pool_sample",
    )(st_t, u, w_pool, pool_scale)


def _compress_rows(load_half, wbig_ref, pe_ref, wpe_ref, n_chunk):
    pe_row = _dot(pe_ref[...], wpe_ref[...])[0:1]
    halves = []
    for c in range(2):
        x = jnp.concatenate([load_half(c, s).astype(BF16) for s in range(CMP_STRIDE)], axis=1)
        y = jnp.dot(x, wbig_ref[c], preferred_element_type=F32)
        halves.append(y[:, :LANE] + pltpu.roll(y[:, LANE:], n_chunk - 1, axis=0))
    return jnp.concatenate(halves, axis=1) + pe_row


def _compress_prompt_kernel(k_ref, v_ref, wbig_ref, pe_ref, wpe_ref, out_ref, *, n_chunk):
    load = lambda c, s: (k_ref, v_ref)[c][0, pl.ds(s, n_chunk, stride=CMP_STRIDE), :]
    out_ref[0] = _compress_rows(load, wbig_ref, pe_ref, wpe_ref, n_chunk)


def _compress_prompt(kv4, wbig, pe8, wpe):
    b, t, _ = kv4.shape
    n_chunk = t // CMP_STRIDE
    return pl.pallas_call(
        functools.partial(_compress_prompt_kernel, n_chunk=n_chunk),
        grid=(b,),
        in_specs=[pl.BlockSpec((1, t, LANE), lambda i: (i, 0, 0)),
                  pl.BlockSpec((1, t, LANE), lambda i: (i, 0, 1)),
                  pl.BlockSpec(wbig.shape, lambda i: (0, 0, 0)),
                  pl.BlockSpec(pe8.shape, lambda i: (0, 0)),
                  pl.BlockSpec(wpe.shape, lambda i: (0, 0))],
        out_specs=pl.BlockSpec((1, n_chunk, KV_W), lambda i: (i, 0, 0)),
        out_shape=jax.ShapeDtypeStruct((b, n_chunk, KV_W), F32),
        compiler_params=_params(("arbitrary",)),
        name="compress_prompt",
    )(kv4, kv4, wbig, pe8, wpe)


def _stack_group_queries(q, g):
    lane = lax.broadcasted_iota(I32, (q.shape[0], LANE), 1)
    parts = []
    for hh in range(GROUP):
        h = GROUP * g + hh
        ch = q[:, (h // 2) * LANE:(h // 2 + 1) * LANE]
        if h % 2 != g:
            ch = pltpu.roll(ch, HEAD_DIM, axis=1)
        parts.append(jnp.where((lane >> 6) == g, ch, 0.0))
    return jnp.concatenate(parts, axis=0).astype(BF16)


def _place_heads(o_g, weights, g, outs):
    lane = lax.broadcasted_iota(I32, (QBLK, LANE), 1)
    for hh in range(GROUP):
        h = GROUP * g + hh
        rs = slice(hh * QBLK, (hh + 1) * QBLK)
        piece = sum(w_[:, 0:1] * o_[rs] for w_, o_ in zip(weights(h), o_g))
        if h % 2 != g:
            piece = pltpu.roll(piece, HEAD_DIM, axis=1)
        piece = jnp.where((lane >> 6) == (h % 2), piece, 0.0)
        outs[h // 2] = piece if outs[h // 2] is None else outs[h // 2] + piece


def _nsa_dense_kernel(q_ref, gt_ref, kvc_ref, kw_ref, vw_ref, part_ref, sel_ref, *, t_len):
    i = pl.program_id(1)
    t0 = i * QBLK
    nc = t_len // CMP_STRIDE
    ns = t_len // SEL_BLOCK
    wl = min(WINDOW + QBLK, t_len)
    q = q_ref[0] * (HEAD_DIM ** -0.5)
    gt = gt_ref[0]
    pos_q = t0 + lax.broadcasted_iota(I32, (QBLK, 1), 0)

    oj = lax.broadcasted_iota(I32, (ns, nc), 0)
    on = lax.broadcasted_iota(I32, (ns, nc), 1)
    ovl_t = jnp.where((on * CMP_STRIDE + CMP_LEN - 1 >= oj * SEL_BLOCK)
                      & (on * CMP_STRIDE <= oj * SEL_BLOCK + SEL_BLOCK - 1), 1.0, 0.0).astype(BF16)
    jb = lax.broadcasted_iota(I32, (ns, QBLK), 0)
    jq = (t0 + lax.broadcasted_iota(I32, (ns, QBLK), 1)) >> 6
    forced = (jb == 0) | (jb == jq) | (jb == jq - 1)

    kc_all = kvc_ref[0, :, 0:LANE].astype(BF16)
    vc_all = kvc_ref[0, :, LANE:2 * LANE].astype(BF16)
    ws = pl.multiple_of(jnp.maximum(t0 - WINDOW, 0), QBLK)
    kwc = kw_ref[0, :, pl.ds(ws, wl)].astype(BF16)
    vwc = vw_ref[0, :, pl.ds(ws, wl)].astype(BF16)

    ncol = lax.broadcasted_iota(I32, (1, nc), 1)
    bias_c = jnp.where((ncol * CMP_STRIDE + CMP_LEN - 1) <= pos_q, 0.0, NEG)
    bias_c = jnp.concatenate([bias_c] * GROUP, axis=0)
    dpos = pos_q - (ws + lax.broadcasted_iota(I32, (1, wl), 1))
    bias_w = jnp.where((dpos >= 0) & (dpos <= WINDOW), 0.0, NEG)
    bias_w = jnp.concatenate([bias_w] * GROUP, axis=0)

    def softmax_parts(s):
        m = jnp.max(s, axis=-1, keepdims=True)
        e = jnp.exp(s - m)
        return e, jnp.where(m > 0.5 * NEG, 1.0 / jnp.sum(e, axis=-1, keepdims=True), 0.0)

    outs = [None] * (NSA_WIDTH // LANE)
    for g in range(KV_HEADS):
        qg = _stack_group_queries(q, g)

        e_c, inv_c = softmax_parts(_dot_nt(qg, kc_all) + bias_c)
        o_c = _dot(e_c, vc_all) * inv_c

        p_c = e_c * inv_c
        psum = p_c[0:QBLK] + p_c[QBLK:2 * QBLK] + p_c[2 * QBLK:3 * QBLK] + p_c[3 * QBLK:4 * QBLK]
        imp_t = sum(lax.dot_general(ovl_t, part, (((1,), (1,)), ((), ())), preferred_element_type=F32)
                    for part in _split3(psum))
        x = jnp.where(forced, jnp.inf, jnp.where(jb > jq, -jnp.inf, imp_t))
        cnt = jnp.zeros((ns, QBLK), I32)
        for jp in range(ns):
            row = x[jp:jp + 1, :]
            beats = (row > x) | ((row == x) & (jb > jp))
            cnt = cnt + jnp.where(beats, 1, 0)
        sel_ref[0, 0, g * ns:(g + 1) * ns, :] = jnp.where(cnt < N_SELECT, 1.0, 0.0)

        e_w, inv_w = softmax_parts(_dot(qg, kwc) + bias_w)
        o_w = _dot_nt(e_w, vwc) * inv_w

        _place_heads((o_c, o_w), lambda h: (gt[:, 3 * h:3 * h + 1], gt[:, 3 * h + 2:3 * h + 3]), g, outs)
    part_ref[0] = jnp.concatenate(outs, axis=1)


def _nsa_select_kernel(q_ref, gt_ref, sel_ref, part_ref, ks_ref, vs_ref, o_ref, *, t_len, kc):
    i = pl.program_id(1)
    t0 = i * QBLK
    ns = t_len // SEL_BLOCK
    rows = GROUP * QBLK
    gt = gt_ref[0]
    n_kchunk = (t0 + QBLK + kc - 1) // kc
    blk_per_chunk = kc // SEL_BLOCK
    gt_t = gt.T
    blocks = []
    q_t =(q_ref[0] * (HEAD_DIM ** -0.5)).T
    zero_half = jnp.zeros((HEAD_DIM, QBLK), F32)
    key_row = lax.broadcasted_iota(I32, (kc, QBLK), 0)
    qry_pos = t0 + lax.broadcasted_iota(I32, (kc, QBLK), 1)
    for g in range(KV_HEADS):
        heads = []
        for hh in range(GROUP):
            q_h = q_t[(GROUP * g + hh) * HEAD_DIM:(GROUP * g + hh + 1) * HEAD_DIM]
            heads.append(jnp.concatenate([q_h, zero_half] if g == 0 else [zero_half, q_h], axis=0))
        qg_t = jnp.concatenate(heads, axis=1).astype(BF16)

        def sel_chunk(c, carry):
            m_prev, l_prev, acc = carry
            k0 = pl.multiple_of(c * kc, kc)
            s = _dot(ks_ref[0, pl.ds(k0, kc), :], qg_t)
            j0 = pl.multiple_of(g * ns + c * blk_per_chunk, blk_per_chunk)
            chosen = sel_ref[0, 0, pl.ds(j0, blk_per_chunk), :]
            chosen = jnp.concatenate([jnp.broadcast_to(chosen[j:j + 1], (SEL_BLOCK, QBLK))
                                      for j in range(blk_per_chunk)], axis=0)
            bias = jnp.where((chosen > 0.5) & (k0 + key_row <= qry_pos), 0.0, NEG)
            s = s + jnp.concatenate([bias] * GROUP, axis=1)
            m_new = jnp.maximum(m_prev, jnp.max(s, axis=0, keepdims=True))
            alpha = jnp.exp(m_prev - m_new)
            p = jnp.exp(s - m_new)
            l_new = alpha * l_prev + jnp.sum(p, axis=0, keepdims=True)
            return m_new, l_new, alpha * acc + _dot(vs_ref[0, :, pl.ds(k0, kc)], p)

        _, l_s, acc_s = lax.fori_loop(
            0, n_kchunk, sel_chunk,
            (jnp.full((1, rows), NEG, F32), jnp.zeros((1, rows), F32), jnp.zeros((LANE, rows), F32)))
        o_t = acc_s[g * HEAD_DIM:(g + 1) * HEAD_DIM] / l_s
        for hh in range(GROUP):
            h = GROUP * g + hh
            blocks.append(o_t[:, hh * QBLK:(hh + 1) * QBLK] * gt_t[3 * h + 1:3 * h + 2])
    o_ref[0] = part_ref[0] + jnp.concatenate(blocks, axis=0).T


def _nsa_prompt(q3, gt3, kvc, ks_tok, kv4t, kvwt, kc):
    b, t, _ = q3.shape
    assert t % kc == 0 and t % QBLK == 0
    nq = t // QBLK
    ns = t // SEL_BLOCK
    slab = lambda c: pl.BlockSpec((1, LANE, t), lambda i, j: (i, c, 0))
    tile = lambda w_: pl.BlockSpec((1, QBLK, w_), lambda i, j: (i, j, 0))
    sel_spec = pl.BlockSpec((1, 1, KV_HEADS * ns, QBLK), lambda i, j: (i, j, 0, 0))
    part, sel = pl.pallas_call(
        functools.partial(_nsa_dense_kernel, t_len=t),
        grid=(b, nq),
        in_specs=[tile(NSA_WIDTH), tile(GATE_PAD),
                  pl.BlockSpec((1, t // CMP_STRIDE, KV_W), lambda i, j: (i, 0, 0)),
                  slab(0), slab(1)],
        out_specs=[tile(NSA_WIDTH), sel_spec],
        out_shape=[jax.ShapeDtypeStruct((b, t, NSA_WIDTH), F32),
                   jax.ShapeDtypeStruct((b, nq, KV_HEADS * ns, QBLK), F32)],
        compiler_params=_params(("arbitrary", "arbitrary")),
        name="nsa_dense",
    )(q3, gt3, kvc, kvwt, kvwt)
    return pl.pallas_call(
        functools.partial(_nsa_select_kernel, t_len=t, kc=kc),
        grid=(b, nq),
        in_specs=[tile(NSA_WIDTH), tile(GATE_PAD), sel_spec, tile(NSA_WIDTH),
                  pl.BlockSpec((1, t, LANE), lambda i, j: (i, 0, 0)), slab(3)],
        out_specs=tile(NSA_WIDTH),
        out_shape=jax.ShapeDtypeStruct((b, t, NSA_WIDTH), F32),
        compiler_params=_params(("arbitrary", "arbitrary")),
        name="nsa_select",
    )(q3, gt3, sel, part, ks_tok, kv4t)


def _xattn_prompt_kernel(x_ref, yp_ref, yn_ref, wo_ref, g_ref, wq_ref, mem_ref, wxo_ref, o_ref):
    y = jnp.concatenate([yp_ref[0], yn_ref[0]], axis=1)
    h1 = x_ref[0] + _dot(y, wo_ref[...])
    qx = _dot(_rms(h1, g_ref[...]), wq_ref[...]) * (X_HEAD_DIM ** -0.5)
    xw = X_HEADS * X_HEAD_DIM
    outs = []
    for h in range(X_HEADS):
        cs = slice(h * X_HEAD_DIM, (h + 1) * X_HEAD_DIM)
        s = _dot_nt(qx[:, cs], mem_ref[0, :, cs])
        e = jnp.exp(s - jnp.max(s, axis=-1, keepdims=True))
        p = e / jnp.sum(e, axis=-1, keepdims=True)
        outs.append(_dot(p, mem_ref[0, :, xw + h * X_HEAD_DIM:xw + (h + 1) * X_HEAD_DIM]))
    o_ref[0] = h1 + _dot(jnp.concatenate(outs, axis=1), wxo_ref[...])


def _xattn_prompt(x3, yp, yn, w_out, g_x, w_xq, mem_kv, w_xo, tm):
    b, t, d = x3.shape
    m = mem_kv.shape[1]
    full = lambda a: pl.BlockSpec(a.shape, lambda i, j: (0,) * a.ndim)
    return pl.pallas_call(
        _xattn_prompt_kernel,
        grid=(b, t // tm),
        in_specs=[pl.BlockSpec((1, tm, d), lambda i, j: (i, j, 0)),
                  pl.BlockSpec((1, tm, POOL_WIDTH), lambda i, j: (i, j, 0)),
                  pl.BlockSpec((1, tm, NSA_WIDTH), lambda i, j: (i, j, 0)),
                  full(w_out), full(g_x), full(w_xq),
                  pl.BlockSpec((1, m, mem_kv.shape[2]), lambda i, j: (i, 0, 0)),
                  full(w_xo)],
        out_specs=pl.BlockSpec((1, tm, d), lambda i, j: (i, j, 0)),
        out_shape=jax.ShapeDtypeStruct((b, t, d), F32),
        compiler_params=_params(("arbitrary", "arbitrary")),
        name="xattn_prompt",
    )(x3, yp, yn, w_out, g_x, w_xq, mem_kv, w_xo)


def _ffn_core(j, nj, h_ref, gf_ref, wg_ref, wv_ref, cw_ref, cb_ref, wd_ref, gfin_ref, y_ref,
              xn_ref, acc_ref, prev_rows):
    @pl.when(j == 0)
    def _():
        xn_ref[...] = _rms(h_ref[...], gf_ref[...]).astype(BF16)
        acc_ref[...] = jnp.zeros(acc_ref.shape, F32)

    xn = xn_ref[...]
    up_g = jnp.dot(xn, wg_ref[...], preferred_element_type=F32)
    up_v = jnp.dot(xn, wv_ref[...], preferred_element_type=F32)
    g1, g2, v1, v2 = prev_rows(up_g, up_v)
    cg = cb_ref[0:1, :] + cw_ref[0, 0:1, :] * g2 + cw_ref[1, 0:1, :] * g1 + cw_ref[2, 0:1, :] * up_g
    cv = cb_ref[1:2, :] + cw_ref[0, 1:2, :] * v2 + cw_ref[1, 1:2, :] * v1 + cw_ref[2, 1:2, :] * up_v
    act = cg / (1.0 + jnp.exp(-cg)) * cv
    acc_ref[...] += _dot(act, wd_ref[...])

    @pl.when(j == nj - 1)
    def _():
        y_ref[...] = _rms(h_ref[...] + acc_ref[...], gfin_ref[...])

    return up_g, up_v


def _ffn_prompt_kernel(h_ref, gf_ref, wg_ref, wv_ref, cw_ref, cb_ref, wd_ref, gfin_ref,
                       y_ref, st_ref, xn_ref, acc_ref, carry_ref, *, tiles_per_seq, nj, tm):
    r = pl.program_id(0)
    j = pl.program_id(1)
    row = lax.broadcasted_iota(I32, (tm, 1), 0)

    @pl.when((r % tiles_per_seq) == 0)
    def _():
        carry_ref[j] = jnp.zeros(carry_ref.shape[1:], F32)

    def prev_rows(up_g, up_v):
        res = []
        for half, up in enumerate((up_g, up_v)):
            prev = carry_ref[j, half]
            m1 = jnp.where(row == 0, prev[7:8], pltpu.roll(up, 1, axis=0))
            m2 = jnp.where(row == 0, prev[6:7], jnp.where(row == 1, prev[7:8], pltpu.roll(up, 2, axis=0)))
            res += [m1, m2]
        return res

    up_g, up_v = _ffn_core(j, nj, h_ref, gf_ref, wg_ref, wv_ref, cw_ref, cb_ref, wd_ref, gfin_ref,
                           y_ref, xn_ref, acc_ref, prev_rows)
    for half, up in enumerate((up_g, up_v)):
        carry_ref[j, half] = up[tm - 8:tm]
        st_ref[0, half] = up[tm - 8:tm]


def _ffn_specs(d, tf, nj):
    return [pl.BlockSpec((1, d), lambda r, j: (0, 0)),
            pl.BlockSpec((d, tf), lambda r, j: (0, j)),
            pl.BlockSpec((d, tf), lambda r, j: (0, nj + j)),
            pl.BlockSpec((3, 2, tf), lambda r, j: (0, 0, j)),
            pl.BlockSpec((2, tf), lambda r, j: (0, j)),
            pl.BlockSpec((tf, d), lambda r, j: (j, 0)),
            pl.BlockSpec((1, d), lambda r, j: (0, 0))]


def _ffn_prompt(h2, g_ffn, w_up, cw3, cb2, w_down, g_final, batch, tm, tf):
    n, d = h2.shape
    nj = D_FF // tf
    tps = n // batch // tm
    return pl.pallas_call(
        functools.partial(_ffn_prompt_kernel, tiles_per_seq=tps, nj=nj, tm=tm),
        grid=(n // tm, nj),
        in_specs=[pl.BlockSpec((tm, d), lambda r, j: (r, 0))] + _ffn_specs(d, tf, nj),
        out_specs=[pl.BlockSpec((tm, d), lambda r, j: (r, 0)),
                   pl.BlockSpec((1, 2, 8, tf), lambda r, j: (r, 0, 0, j))],
        out_shape=[jax.ShapeDtypeStruct((n, d), F32),
                   jax.ShapeDtypeStruct((n // tm, 2, 8, D_FF), F32)],
        scratch_shapes=[pltpu.VMEM((tm, d), BF16), pltpu.VMEM((tm, d), F32),
                        pltpu.VMEM((nj, 2, 8, tf), F32)],
        compiler_params=_params(("arbitrary", "arbitrary")),
        name="ffn_prompt",
    )(h2, g_ffn, w_up, w_up, cw3, cb2, w_down, g_final)


def _ffn_sample_kernel(h_ref, gf_ref, wg_ref, wv_ref, cw_ref, cb_ref, wd_ref, gfin_ref, sg_ref, sv_ref,
                       y_ref, up_ref, xn_ref, acc_ref, *, nj):
    j = pl.program_id(1)
    prev_rows = lambda up_g, up_v: (sg_ref[1], sg_ref[0], sv_ref[1], sv_ref[0])
    up_g, up_v = _ffn_core(j, nj, h_ref, gf_ref, wg_ref, wv_ref, cw_ref, cb_ref, wd_ref, gfin_ref,
                           y_ref, xn_ref, acc_ref, prev_rows)
    up_ref[0] = up_g
    up_ref[1] = up_v


def _ffn_sample(h2, g_ffn, w_up, cw3, cb2, w_down, g_final, st_t, tf):
    n, d = h2.shape
    nj = D_FF // tf
    return pl.pallas_call(
        functools.partial(_ffn_sample_kernel, nj=nj),
        grid=(1, nj),
        in_specs=[pl.BlockSpec((n, d), lambda r, j: (0, 0))] + _ffn_specs(d, tf, nj)
        + [pl.BlockSpec((2, n, tf), lambda r, j: (0, 0, j)),
           pl.BlockSpec((2, n, tf), lambda r, j: (0, 0, nj + j))],
        out_specs=[pl.BlockSpec((n, d), lambda r, j: (0, 0)),
                   pl.BlockSpec((2, n, tf), lambda r, j: (0, 0, j))],
        out_shape=[jax.ShapeDtypeStruct((n, d), F32),
                   jax.ShapeDtypeStruct((2, n, D_FF), F32)],
        scratch_shapes=[pltpu.VMEM((n, d), BF16), pltpu.VMEM((n, d), F32)],
        compiler_params=_params(("arbitrary", "arbitrary")),
        name="ffn_sample",
    )(h2, g_ffn, w_up, w_up, cw3, cb2, w_down, g_final, st_t, st_t)


def _head_rows(qrow):
    rowi = lax.broadcasted_iota(I32, (N_HEADS, LANE), 0)
    lane = lax.broadcasted_iota(I32, (N_HEADS, LANE), 1)
    x = jnp.zeros((N_HEADS, LANE), F32)
    for c in range(NSA_WIDTH // LANE):
        x = jnp.where((rowi >> 1) == c, jnp.broadcast_to(qrow[:, c * LANE:(c + 1) * LANE], (N_HEADS, LANE)), x)
    x = jnp.where((rowi & 1) != (rowi >> 2), pltpu.roll(x, HEAD_DIM, axis=1), x)
    return jnp.where((lane >> 6) == (rowi >> 2), x, 0.0)


def _page_copy(cache_ref, raw_ref, sem_ref, phys, p, slot):
    return pltpu.make_async_copy(cache_ref.at[phys, pl.ds(0, KV_W), :], raw_ref.at[slot, p], sem_ref.at[slot, p])


def _nsa_sample_cmp_kernel(pt_ref, q_ref, cache_ref, wbig_ref, pe_ref, wpe_ref, oc_ref, imp_ref,
                           raw_ref, tok_ref, sem_ref, *, n_pages, n_batch, page):
    b = pl.program_id(0)
    slot = b % 2
    n_tok = n_pages * page
    n_chunk = n_tok // CMP_STRIDE

    def start_all(bb, sl):
        def body(p, _):
            _page_copy(cache_ref, raw_ref, sem_ref, pt_ref[bb * n_pages + p], p, sl).start()
            return 0
        lax.fori_loop(0, n_pages, body, 0)

    @pl.when(b == 0)
    def _():
        start_all(0, 0)

    @pl.when(b + 1 < n_batch)
    def _():
        start_all(b + 1, 1 - slot)

    chunks_per_page = page // CMP_STRIDE

    def to_token_major(pg, _):
        for pp in range(PAGE_UNROLL):
            _page_copy(cache_ref, raw_ref, sem_ref, 0, pg * PAGE_UNROLL + pp, slot).wait()
        for pp in range(PAGE_UNROLL):
            p = pg * PAGE_UNROLL + pp
            xt = raw_ref[slot, p]
            for half in range(2):
                x = xt[half * LANE:(half + 1) * LANE].T
                for nl in range(chunks_per_page):
                    r0 = pl.multiple_of((p * chunks_per_page + nl) * CHUNK_PITCH, 8)
                    tok_ref[half, pl.ds(r0, CMP_STRIDE), :] = x[nl * CMP_STRIDE:(nl + 1) * CMP_STRIDE]
        return 0
    lax.fori_loop(0, n_pages // PAGE_UNROLL, to_token_major, 0)

    load = lambda c, s: tok_ref[c, pl.ds(s, n_chunk, stride=CHUNK_PITCH), :]
    kvc = _compress_rows(load, wbig_ref, pe_ref, wpe_ref, n_chunk)
    q8 = _head_rows(q_ref[0] * (HEAD_DIM ** -0.5))
    s = _dot_nt(q8, kvc[:, 0:LANE])
    ncol = lax.broadcasted_iota(I32, (1, n_chunk), 1)
    p = _masked_softmax(s, (ncol * CMP_STRIDE + CMP_LEN - 1) <= n_tok)
    oc_ref[0] = _dot(p, kvc[:, LANE:2 * LANE])
    rowi = lax.broadcasted_iota(I32, (N_HEADS, n_chunk), 0)
    psum = jnp.where(rowi == 0, jnp.sum(p[0:GROUP], axis=0, keepdims=True),
                     jnp.where(rowi == 1, jnp.sum(p[GROUP:2 * GROUP], axis=0, keepdims=True), 0.0))
    ns_pad = imp_ref.shape[2]
    on = lax.broadcasted_iota(I32, (n_chunk, ns_pad), 0)
    oj = lax.broadcasted_iota(I32, (n_chunk, ns_pad), 1)
    ovl = jnp.where((on * CMP_STRIDE + CMP_LEN - 1 >= oj * SEL_BLOCK)
                    & (on * CMP_STRIDE <= oj * SEL_BLOCK + SEL_BLOCK - 1), 1.0, 0.0).astype(BF16)
    imp_ref[0] = sum(jnp.dot(part, ovl, preferred_element_type=F32) for part in _split3(psum))


def _nsa_sample_cmp(pt_flat, q3, cache3, wbig, pe8, wpe, n_pages, ns_pad):
    nb = q3.shape[0]
    page = cache3.shape[2]
    assert page == LANE
    n_tok = n_pages * page
    gs = pltpu.PrefetchScalarGridSpec(
        num_scalar_prefetch=1, grid=(nb,),
        in_specs=[pl.BlockSpec((1, 1, NSA_WIDTH), lambda i, pt: (i, 0, 0)),
                  pl.BlockSpec(memory_space=pl.ANY),
                  pl.BlockSpec(wbig.shape, lambda i, pt: (0, 0, 0)),
                  pl.BlockSpec(pe8.shape, lambda i, pt: (0, 0)),
                  pl.BlockSpec(wpe.shape, lambda i, pt: (0, 0))],
        out_specs=[pl.BlockSpec((1, N_HEADS, LANE), lambda i, pt: (i, 0, 0)),
                   pl.BlockSpec((1, N_HEADS, ns_pad), lambda i, pt: (i, 0, 0))],
        scratch_shapes=[pltpu.VMEM((2, n_pages, KV_W, page), F32), pltpu.VMEM((2, n_tok // CMP_STRIDE * CHUNK_PITCH, LANE), F32),
                        pltpu.SemaphoreType.DMA((2, n_pages))])
    return pl.pallas_call(
        functools.partial(_nsa_sample_cmp_kernel, n_pages=n_pages, n_batch=nb, page=page),
        grid_spec=gs,
        out_shape=[jax.ShapeDtypeStruct((nb, N_HEADS, LANE), F32),
                   jax.ShapeDtypeStruct((nb, N_HEADS, ns_pad), F32)],
        compiler_params=_params(("arbitrary",)),
        name="nsa_sample_cmp",
    )(pt_flat, q3, cache3, wbig, pe8, wpe)


def _topk_sample_kernel(imp_ref, idx_ref, *, jq, k_past):
    x = imp_ref[...]
    col = lax.broadcasted_iota(I32, x.shape, 1)
    x = jnp.where((col == 0) | (col == jq - 1), jnp.inf, jnp.where(col >= jq, -jnp.inf, x))
    lane = lax.broadcasted_iota(I32, idx_ref.shape, 1)
    out = jnp.zeros(idx_ref.shape, I32)
    big = jnp.int32(1 << 30)
    for k in range(k_past):
        m = jnp.max(x, axis=-1, keepdims=True)
        idx = jnp.min(jnp.where(x == m, col, big), axis=-1, keepdims=True)
        out = jnp.where(lane == k, idx, out)
        x = jnp.where(col == idx, -jnp.inf, x)
    idx_ref[...] = out


def _topk_sample(imp2, jq, k_past):
    n, w = imp2.shape
    return pl.pallas_call(
        functools.partial(_topk_sample_kernel, jq=jq, k_past=k_past),
        grid=(1,),
        in_specs=[pl.BlockSpec((n, w), lambda i: (0, 0))],
        out_specs=pl.BlockSpec((n, LANE), lambda i: (0, 0)),
        out_shape=jax.ShapeDtypeStruct((n, LANE), I32),
        compiler_params=_params(("arbitrary",)),
        name="topk_sample",
    )(imp2)


def _blk_copy(cache_ref, buf_ref, sem_ref, phys, kv, g, r, slot):
    row0 = KV_W + kv * LANE + g * HEAD_DIM
    return pltpu.make_async_copy(cache_ref.at[phys, pl.ds(row0, HEAD_DIM), :],
                                 buf_ref.at[slot, kv, pl.ds(g * HEAD_DIM, HEAD_DIM), pl.ds(r * LANE, LANE)],
                                 sem_ref.at[slot])


def _nsa_sample_sel_kernel(idx_ref, pt_ref, q_ref, gt_ref, oc_ref, win_ref, kvn4_ref, kvnw_ref, cache_ref,
                           o_ref, wout_ref, buf_ref, sem_ref, *, n_pages, n_batch, page, k_past):
    b = pl.program_id(0)
    slot = b % 2
    per_page = page // SEL_BLOCK
    nl = k_past * page

    def for_blocks(bb, fn):
        def body(r, _):
            for g in range(KV_HEADS):
                j = idx_ref[(bb * KV_HEADS + g) * k_past + r]
                phys = pt_ref[bb * n_pages + j // per_page]
                for kv in range(2):
                    fn(phys, kv, g, r)
            return 0
        lax.fori_loop(0, k_past, body, 0)

    def start_all(bb, sl):
        for_blocks(bb, lambda phys, kv, g, r: _blk_copy(cache_ref, buf_ref, sem_ref, phys, kv, g, r, sl).start())

    @pl.when(b == 0)
    def _():
        start_all(0, 0)

    @pl.when(b + 1 < n_batch)
    def _():
        start_all(b + 1, 1 - slot)

    def wait_body(r, _):
        for g in range(KV_HEADS):
            for kv in range(2):
                _blk_copy(cache_ref, buf_ref, sem_ref, 0, kv, g, r, slot).wait()
        return 0
    lax.fori_loop(0, k_past, wait_body, 0)

    q8 = _head_rows(q_ref[0] * (HEAD_DIM ** -0.5))
    rowi = lax.broadcasted_iota(I32, (N_HEADS, 1), 0)
    lane = lax.broadcasted_iota(I32, (N_HEADS, LANE), 1)
    own = (lane >> 6) == (rowi >> 2)
    mine = jnp.where(lax.broadcasted_iota(I32, (N_HEADS, n_batch), 1) == b, 0.0, -jnp.inf)

    def attend(k_t, v_t, bias):
        s = _dot(q8, k_t) + bias
        e = jnp.exp(s - jnp.max(s, axis=-1, keepdims=True))
        return _dot_nt(e, v_t) / jnp.sum(e, axis=-1, keepdims=True)

    lane_s = lax.broadcasted_iota(I32, (1, nl), 1)
    halves = []
    for g in range(KV_HEADS):
        hv = jnp.zeros((1, nl), I32)
        for r in range(k_past):
            hv = jnp.where((lane_s // page) == r, idx_ref[(b * KV_HEADS + g) * k_past + r] % per_page, hv)
        halves.append(hv)
    vis = ((lane_s % page) // SEL_BLOCK) == jnp.where(rowi < GROUP, halves[0], halves[1])
    o_s = attend(jnp.concatenate([buf_ref[slot, 0], kvn4_ref[0, KV_W:KV_W + LANE]], axis=1),
                 jnp.concatenate([buf_ref[slot, 1], kvn4_ref[0, KV_W + LANE:2 * KV_W]], axis=1),
                 jnp.concatenate([jnp.where(vis, 0.0, -jnp.inf), mine], axis=1))

    cw = win_ref[0]
    wlen = cw.shape[1]
    o_w = attend(jnp.concatenate([cw[0:LANE], kvnw_ref[0, 0:LANE]], axis=1),
                 jnp.concatenate([cw[LANE:2 * LANE], kvnw_ref[0, LANE:2 * LANE]], axis=1),
                 jnp.concatenate([jnp.zeros((N_HEADS, wlen), F32), mine], axis=1))
    lane_n = lax.broadcasted_iota(I32, (KV_W, n_batch), 1)
    new_col = jnp.sum(jnp.where(lane_n == b, kvnw_ref[0], 0.0), axis=1, keepdims=True)
    lane_w = lax.broadcasted_iota(I32, (KV_W, wlen), 1)
    wout_ref[0] = jnp.where(lane_w == wlen - 1, new_col, pltpu.roll(cw, wlen - 1, axis=1))

    gt = jnp.broadcast_to(gt_ref[0], (N_HEADS, GATE_PAD))
    gate = lambda br: jnp.sum(jnp.where(lane == rowi * N_BRANCH + br, gt, 0.0), axis=-1, keepdims=True)
    o = gate(0) * oc_ref[0] + gate(1) * o_s + gate(2) * o_w
    o_ref[0] = jnp.where(own, o, 0.0)


def _nsa_sample_sel(idx_flat, pt_flat, q3, gt3, oc, win_t, kvn4_t, kvnw_t, cache3, n_pages, k_past):
    nb = q3.shape[0]
    page = cache3.shape[2]
    wlen = win_t.shape[2]
    assert page == LANE
    per = lambda *blk: pl.BlockSpec((1,) + blk, lambda i, a, c: (i, 0, 0))
    whole = lambda arr: pl.BlockSpec(arr.shape, lambda i, a, c: (0, 0, 0))
    gs = pltpu.PrefetchScalarGridSpec(
        num_scalar_prefetch=2, grid=(nb,),
        in_specs=[per(1, NSA_WIDTH), per(1, GATE_PAD), per(N_HEADS, LANE), per(KV_W, wlen),
                  whole(kvn4_t), whole(kvnw_t), pl.BlockSpec(memory_space=pl.ANY)],
        out_specs=[per(N_HEADS, LANE), per(KV_W, wlen)],
        scratch_shapes=[pltpu.VMEM((2, 2, LANE, k_past * page), F32), pltpu.SemaphoreType.DMA((2,))])
    return pl.pallas_call(
        functools.partial(_nsa_sample_sel_kernel, n_pages=n_pages, n_batch=nb, page=page, k_past=k_past),
        grid_spec=gs,
        out_shape=[jax.ShapeDtypeStruct((nb, N_HEADS, LANE), F32),
                   jax.ShapeDtypeStruct((nb, KV_W, wlen), F32)],
        compiler_params=_params(("arbitrary",)),
        name="nsa_sample_sel",
    )(idx_flat, pt_flat, q3, gt3, oc, win_t, kvn4_t, kvnw_t, cache3)


def _mix_sample_kernel(x_ref, yp_ref, yn_ref, wo_ref, g_ref, wq_ref, h1_ref, qx_ref):
    y = jnp.concatenate([yp_ref[...], yn_ref[...]], axis=1)
    h1 = x_ref[...] + _dot(y, wo_ref[...])
    h1_ref[...] = h1
    qx_ref[...] = _dot(_rms(h1, g_ref[...]), wq_ref[...]) * (X_HEAD_DIM ** -0.5)


def _mix_sample(x, yp, yn, w_out, g_x, w_xq):
    n, d = x.shape
    full = lambda a: pl.BlockSpec(a.shape, lambda i: (0,) * a.ndim)
    args = (x, yp, yn, w_out, g_x, w_xq)
    return pl.pallas_call(
        _mix_sample_kernel, grid=(1,),
        in_specs=[full(a) for a in args],
        out_specs=[pl.BlockSpec((n, d), lambda i: (0, 0))] * 2,
        out_shape=[jax.ShapeDtypeStruct((n, d), F32)] * 2,
        compiler_params=_params(("arbitrary",)),
        name="mix_sample",
    )(*args)


def _xattn_sample_kernel(qx_ref, mem_ref, o_ref):
    n_mem = mem_ref.shape[1]
    n = n_mem * 8
    q = qx_ref[0]
    order = [(half, h) for half in range(X_HEAD_DIM // LANE) for h in range(X_HEADS)]
    col = lambda half, h: slice(h * X_HEAD_DIM + half * LANE, h * X_HEAD_DIM + (half + 1) * LANE)
    qt = jnp.concatenate([q[:, col(half, h)] for half, h in order], axis=0)
    x = mem_ref[0]
    a = _dot_nt(qt, x[:, 0].reshape(n, LANE))
    rowi = lax.broadcasted_iota(I32, (8, n), 0)
    lane = lax.broadcasted_iota(I32, (8, n), 1)
    own = (lane & 7) == rowi
    a = jnp.where(own, a, 0.0)
    other = pltpu.roll(a, X_HEADS, axis=0)
    s = a + jnp.where(rowi < X_HEADS, pltpu.roll(other, n - X_HEADS, axis=1), pltpu.roll(other, X_HEADS, axis=1))
    s = jnp.where(own, s, -jnp.inf)
    e = jnp.exp(s - jnp.max(s, axis=-1, keepdims=True))
    p = e / jnp.sum(e, axis=-1, keepdims=True)
    o8 = _dot(p, x[:, 1].reshape(n, LANE))
    o_ref[0] = jnp.concatenate([o8[half * X_HEADS + h:half * X_HEADS + h + 1]
                                for h in range(X_HEADS) for half in range(X_HEAD_DIM // LANE)], axis=1)


def _xattn_sample(qx3, mem5):
    nb, _, xw = qx3.shape
    return pl.pallas_call(
        _xattn_sample_kernel, grid=(nb,),
        in_specs=[pl.BlockSpec((1, 1, xw), lambda i: (i, 0, 0)),
                  pl.BlockSpec((1,) + mem5.shape[1:], lambda i: (i, 0, 0, 0, 0))],
        out_specs=pl.BlockSpec((1, 1, xw), lambda i: (i, 0, 0)),
        out_shape=jax.ShapeDtypeStruct((nb, 1, xw), F32),
        compiler_params=_params(("arbitrary",)),
        name="xattn_sample",
    )(qx3, mem5)


def _resid_proj_kernel(h_ref, o_ref, w_ref, y_ref):
    y_ref[...] = h_ref[...] + _dot(o_ref[...], w_ref[...])


def _resid_proj(h, o, w):
    n, d = h.shape
    full = lambda a: pl.BlockSpec(a.shape, lambda i: (0,) * a.ndim)
    return pl.pallas_call(
        _resid_proj_kernel, grid=(1,),
        in_specs=[full(h), full(o), full(w)],
        out_specs=pl.BlockSpec((n, d), lambda i: (0, 0)),
        out_shape=jax.ShapeDtypeStruct((n, d), F32),
        compiler_params=_params(("arbitrary",)),
        name="resid_proj",
    )(h, o, w)


def _prep_weights(g_mix, w_in, w_pool, pool_scale, w_cmp, pe_cmp, w_out, g_xattn, g_mem, w_xq, w_xkv,
                  w_xo, g_ffn, w_up, conv_w, conv_b, w_down, g_final):
    o_kv = POOL_WIDTH + NSA_WIDTH
    o_gate = o_kv + N_BRANCH * KV_W
    w_tok = jnp.concatenate([w_in[:, :o_kv + KV_W + LANE], jnp.pad(w_in[:, o_gate:], ((0, 0), (0, GATE_PAD - N_GATE)))],
                            axis=1)
    r = CMP_LEN // CMP_STRIDE
    w5 = w_cmp.reshape(2, r, CMP_STRIDE, HEAD_DIM, HEAD_DIM)
    eye = jnp.eye(2, dtype=F32)
    wbig = jnp.einsum('cisde,gG->csgdiGe', w5, eye).reshape(2, CMP_STRIDE * LANE, r * LANE)
    wpe = jnp.einsum('cke,cC->ckCe', w_cmp.reshape(2, CMP_LEN * HEAD_DIM, HEAD_DIM), eye)
    wpe = jnp.broadcast_to(wpe[:, :, :, None, :], (2, CMP_LEN * HEAD_DIM, 2, KV_HEADS, HEAD_DIM))
    row = lambda v: v.reshape(1, -1)
    return dict(
        g_mix=row(g_mix), w_tok=w_tok.astype(BF16), w_feat=w_in[:, o_kv:o_gate].T.astype(BF16),
        w_pool=w_pool.astype(BF16), pool_scale=row(pool_scale),
        wbig=wbig.astype(BF16), wpe=wpe.reshape(2 * CMP_LEN * HEAD_DIM, KV_W).astype(BF16),
        pe8=jnp.broadcast_to(pe_cmp.reshape(1, -1), (8, 2 * CMP_LEN * HEAD_DIM)),
        w_out=w_out.astype(BF16), g_xattn=row(g_xattn), g_mem=row(g_mem), w_xq=w_xq.astype(BF16),
        w_xkv=w_xkv.astype(BF16), w_xo=w_xo.astype(BF16), g_ffn=row(g_ffn), w_up=w_up.astype(BF16),
        cw3=conv_w.reshape(3, 2, D_FF), cb2=conv_b.reshape(2, D_FF), w_down=w_down.astype(BF16),
        g_final=row(g_final))


def _token_major(a_t, lead):
    b, _, t = a_t.shape
    nd = len(lead)
    return jnp.transpose(a_t.reshape((b,) + lead + (t,)), (0, nd + 1) + tuple(range(1, nd + 1)))


def _feature_major(a, nfeat):
    nd = a.ndim
    return jnp.transpose(a, (0,) + tuple(range(2, nd)) + (1,)).reshape(a.shape[0], nfeat, a.shape[1])


def _prompt_layer(x, mem, w, *, tm=512, tm_x=1024, tf=D_FF, kc=512):
    b, t, d = x.shape
    n = b * t
    u, q, kvc_tok, ks_tok, gt, kv4t, kvwt = _mix_in(x, w['g_mix'], w['w_tok'], w['w_feat'], tm, "mix_in_prompt")
    r3 = lambda a: a.reshape(b, t, a.shape[-1])
    u, q, kvc_tok, ks_tok, gt = map(r3, (u, q, kvc_tok, ks_tok, gt))
    y_pool = _pool_prompt(u, w['w_pool'], w['pool_scale'], tm)
    kvc = _compress_prompt(kvc_tok, w['wbig'], w['pe8'], w['wpe'])
    y_nsa = _nsa_prompt(q, gt, kvc, ks_tok, kv4t, kvwt, kc)
    m = mem.shape[1]
    (mem_kv,) = _rms_proj(mem.reshape(b * m, d), w['g_mem'], w['w_xkv'], (2 * X_HEADS * X_HEAD_DIM,), m,
                          name="mem_project")
    mem_kv = mem_kv.reshape(b, m, 2 * X_HEADS * X_HEAD_DIM)
    h2 = _xattn_prompt(x, y_pool, y_nsa, w['w_out'], w['g_xattn'], w['w_xq'], mem_kv, w['w_xo'], tm_x)
    y, st = _ffn_prompt(h2.reshape(n, d), w['g_ffn'], w['w_up'], w['cw3'], w['cb2'], w['w_down'],
                        w['g_final'], b, tm, tf)
    wlen = min(WINDOW, t)
    return (y.reshape(b, t, d),
            _token_major(kv4t, (4, KV_HEADS, HEAD_DIM)),
            _token_major(kvwt[:, :, t - wlen:], (2, KV_HEADS, HEAD_DIM)),
            u[:, t - POOL_STATE:],
            jnp.swapaxes(st[t // tm - 1::t // tm, :, 6:8], 1, 2).reshape(b, 2, 2 * D_FF),
            mem_kv.reshape(b, m, 2, X_HEADS, X_HEAD_DIM))


def _sample_layer(x, cache_kv, page_table, cache_win, state_pool, state_ffn, cache_mem, w, *, tf=1408):
    nb, _, d = x.shape
    n_phys, page = cache_kv.shape[:2]
    n_pages = page_table.shape[1]
    past = n_pages * page
    u, q, _, _, gt, kv4t, kvwt = _mix_in(x.reshape(1, nb, d), w['g_mix'], w['w_tok'], w['w_feat'], nb, "mix_in_sample")
    y_pool = _pool_sample(jnp.swapaxes(state_pool, 0, 1), u, w['w_pool'], w['pool_scale'], past)
    cache3 = _feature_major(cache_kv, 2 * KV_W)
    pt_flat = page_table.reshape(-1)
    jq = past // SEL_BLOCK
    ns_pad = -(-(jq + 1) // LANE) * LANE
    k_past = min(N_SELECT, jq + 1) - 1
    q3 = q.reshape(nb, 1, NSA_WIDTH)
    oc, imp = _nsa_sample_cmp(pt_flat, q3, cache3, w['wbig'], w['pe8'], w['wpe'], n_pages, ns_pad)
    idx = _topk_sample(imp[:, :KV_HEADS].reshape(nb * KV_HEADS, ns_pad), jq, k_past)
    idx_flat = idx[:, :k_past].reshape(-1)
    o8, win_out = _nsa_sample_sel(idx_flat, pt_flat, q3, gt.reshape(nb, 1, GATE_PAD), oc,
                                  _feature_major(cache_win, KV_W), kv4t, kvwt, cache3, n_pages, k_past)
    y_nsa = jnp.concatenate([o8[:, :GROUP, :HEAD_DIM].reshape(nb, GROUP * HEAD_DIM),
                             o8[:, GROUP:, HEAD_DIM:].reshape(nb, GROUP * HEAD_DIM)], axis=1)
    h1, qx = _mix_sample(x.reshape(nb, d), y_pool, y_nsa, w['w_out'], w['g_xattn'], w['w_xq'])
    m = cache_mem.shape[1]
    halves = X_HEAD_DIM // LANE
    mem5 = jnp.transpose(cache_mem.reshape(nb, m, 2, X_HEADS, halves, LANE), (0, 1, 2, 4, 3, 5))
    ox = _xattn_sample(qx.reshape(nb, 1, d), mem5.reshape(nb, m, 2, halves * X_HEADS, LANE))
    h2 = _resid_proj(h1, ox.reshape(nb, d), w['w_xo'])
    y, up2 = _ffn_sample(h2, w['g_ffn'], w['w_up'], w['cw3'], w['cb2'], w['w_down'], w['g_final'],
                         jnp.swapaxes(state_ffn, 0, 1), tf)
    up = jnp.swapaxes(up2, 0, 1).reshape(nb, 1, 2 * D_FF)
    return (y.reshape(nb, 1, d),
            jnp.swapaxes(_token_major(kv4t, (4, KV_HEADS, HEAD_DIM)), 0, 1),
            _token_major(win_out, (2, KV_HEADS, HEAD_DIM)),
            jnp.concatenate([state_pool[:, 1:], u[:, None, :]], axis=1),
            jnp.concatenate([state_ffn[:, 1:], up], axis=1),
            )


def kernel(x_prompt, x_sample, mem_prompt, cache_kv, page_table, cache_win, state_pool, state_ffn, cache_mem, g_mix, w_in, w_pool, pool_scale, w_cmp, pe_cmp, w_out, g_xattn, g_mem, w_xq, w_xkv, w_xo, g_ffn, w_up, conv_w, conv_b, w_down, g_final):
    assert g_mix.shape[0] == 1, "single layer"
    w = _prep_weights(g_mix[0], w_in[0], w_pool[0], pool_scale[0], w_cmp[0], pe_cmp[0], w_out[0], g_xattn[0],
                      g_mem[0], w_xq[0], w_xkv[0], w_xo[0], g_ffn[0], w_up[0], conv_w[0], conv_b[0], w_down[0],
                      g_final)
    yp, kv_p, win_p, pool_p, ffn_p, mem_p = _prompt_layer(x_prompt, mem_prompt, w)
    ys, kv_s, win_s, pool_s, ffn_s = _sample_layer(x_sample, cache_kv[0], page_table, cache_win[0],
                                                    state_pool[0], state_ffn[0], cache_mem[0], w)
    lead = lambda a: a[None]
    return (yp, ys, lead(kv_p), lead(kv_s), lead(win_p), lead(win_s), lead(pool_p), lead(pool_s),
            lead(ffn_p), lead(ffn_s), lead(mem_p))
```

```python
import functools

import jax
import jax.numpy as jnp
from jax import lax
from jax.experimental import pallas as pl
from jax.experimental.pallas import tpu as pltpu

F32 = jnp.float32
BF16 = jnp.bfloat16
I32 = jnp.int32

D_MODEL = 1024
POOL_WIDTH = 512
POOL_WINDOWS = (2, 4, 8, 16)
POOL_GC = 128
POOL_STATE = 15
NSA_WIDTH = 512
HEAD_DIM = 64
N_HEADS = 8
KV_HEADS = 2
GROUP = 4
N_BRANCH = 3
CMP_LEN = 32
CMP_STRIDE = 16
SEL_BLOCK = 64
N_SELECT = 16
WINDOW = 512
QBLK = 128
KV_W = 2 * KV_HEADS * HEAD_DIM
N_GATE = N_HEADS * N_BRANCH
LANE = 128
GATE_PAD = LANE
X_HEADS = 4
X_HEAD_DIM = 256
D_FF = 2816
EPS = 1e-6
NEG = -1e30
VMEM_LIMIT = 56 * 1024 * 1024
CHUNK_PITCH = 24
PAGE_UNROLL = 16


def _dot(a, b):
    return jnp.dot(a.astype(BF16), b.astype(BF16), preferred_element_type=F32)


def _dot_nt(a, b):
    return lax.dot_general(a.astype(BF16), b.astype(BF16), (((1,), (1,)), ((), ())),
                           preferred_element_type=F32)


def _rms(x, g):
    return x * lax.rsqrt(jnp.mean(x * x, axis=-1, keepdims=True) + EPS) * g


def _split3(x):
    p0 = x.astype(BF16)
    r = x - p0.astype(F32)
    p1 = r.astype(BF16)
    p2 = (r - p1.astype(F32)).astype(BF16)
    return p0, p1, p2


def _masked_softmax(s, mask):
    s = jnp.where(mask, s, -jnp.inf)
    m = jnp.max(s, axis=-1, keepdims=True)
    m = jnp.where(m == -jnp.inf, 0.0, m)
    e = jnp.exp(s - m)
    return e / jnp.maximum(jnp.sum(e, axis=-1, keepdims=True), 1e-30)


def _params(sem):
    return pltpu.CompilerParams(dimension_semantics=sem, vmem_limit_bytes=VMEM_LIMIT)


def _rms_proj_kernel(x_ref, g_ref, w_ref, *out_refs, widths, sigmoid_last):
    xn = _rms(x_ref[...], g_ref[...]).astype(BF16)
    off = 0
    for k, (o_ref, wd) in enumerate(zip(out_refs, widths)):
        y = jnp.dot(xn, w_ref[:, off:off + wd], preferred_element_type=F32)
        if sigmoid_last and k == len(widths) - 1:
            y = 1.0 / (1.0 + jnp.exp(-y))
        o_ref[...] = y
        off += wd


def _rms_proj(x, g, w, widths, tm, sigmoid_last=False, name="rms_proj"):
    n, d = x.shape
    ntot = w.shape[1]
    assert sum(widths) == ntot and n % tm == 0
    return pl.pallas_call(
        functools.partial(_rms_proj_kernel, widths=tuple(widths), sigmoid_last=sigmoid_last),
        grid=(n // tm,),
        in_specs=[pl.BlockSpec((tm, d), lambda i: (i, 0)),
                  pl.BlockSpec((1, d), lambda i: (0, 0)),
                  pl.BlockSpec((d, ntot), lambda i: (0, 0))],
        out_specs=[pl.BlockSpec((tm, wd), lambda i: (i, 0)) for wd in widths],
        out_shape=[jax.ShapeDtypeStruct((n, wd), F32) for wd in widths],
        compiler_params=_params(("arbitrary",)),
        name=name,
    )(x, g, w)


def _mix_in_kernel(x_ref, g_ref, wt_ref, wf_ref, u_ref, q_ref, kc_ref, ks_ref, gt_ref, kv4_ref, kvw_ref):
    xn = _rms(x_ref[...], g_ref[...]).astype(BF16)
    off = 0
    for o_ref in (u_ref, q_ref, kc_ref, ks_ref, gt_ref):
        wd = o_ref.shape[-1]
        y = jnp.dot(xn, wt_ref[:, off:off + wd], preferred_element_type=F32)
        o_ref[...] = 1.0 / (1.0 + jnp.exp(-y)) if o_ref is gt_ref else y
        off += wd
    kv_t = lax.dot_general(wf_ref[...], xn, (((1,), (1,)), ((), ())), preferred_element_type=F32)
    kv4_ref[0] = kv_t[0:2 * KV_W]
    kvw_ref[0] = kv_t[2 * KV_W:3 * KV_W]


def _mix_in(x3, g, w_tok, w_feat, tm, name):
    b, t, d = x3.shape
    n = b * t
    tps = t // tm
    widths = (POOL_WIDTH, NSA_WIDTH, KV_W, LANE, GATE_PAD)
    assert sum(widths) == w_tok.shape[1] and t % tm == 0
    return pl.pallas_call(
        _mix_in_kernel,
        grid=(n // tm,),
        in_specs=[pl.BlockSpec((tm, d), lambda i: (i, 0)),
                  pl.BlockSpec((1, d), lambda i: (0, 0)),
                  pl.BlockSpec(w_tok.shape, lambda i: (0, 0)),
                  pl.BlockSpec(w_feat.shape, lambda i: (0, 0))],
        out_specs=[pl.BlockSpec((tm, wd), lambda i: (i, 0)) for wd in widths]
        + [pl.BlockSpec((1, 2 * KV_W, tm), lambda i: (i // tps, 0, i % tps)),
           pl.BlockSpec((1, KV_W, tm), lambda i: (i // tps, 0, i % tps))],
        out_shape=[jax.ShapeDtypeStruct((n, wd), F32) for wd in widths]
        + [jax.ShapeDtypeStruct((b, 2 * KV_W, t), F32), jax.ShapeDtypeStruct((b, KV_W, t), F32)],
        compiler_params=_params(("arbitrary",)),
        name=name,
    )(x3.reshape(n, d), g, w_tok, w_feat)


def _pool_tail(s_list, u, cnts, w_ref, sc_ref, y_ref, lead):
    for gi in range(len(POOL_WINDOWS)):
        cols = slice(gi * POOL_GC, (gi + 1) * POOL_GC)
        d = s_list[gi] / cnts[gi] - u[:, cols]
        y = _dot(d, w_ref[gi]) * sc_ref[:, cols]
        if lead:
            y_ref[0, :, cols] = y
        else:
            y_ref[:, cols] = y


def _pool_prompt_kernel(u_ref, w_ref, sc_ref, y_ref, ext_ref, *, tm):
    t = pl.program_id(1)
    hist = 16

    @pl.when(t == 0)
    def _():
        ext_ref[0:hist, :] = jnp.zeros((hist, POOL_WIDTH), F32)

    @pl.when(t > 0)
    def _():
        ext_ref[0:hist, :] = ext_ref[tm:tm + hist, :]

    u = u_ref[0]
    ext_ref[hist:hist + tm, :] = u
    pos = t * tm + lax.broadcasted_iota(I32, (tm, 1), 0)
    s_list, cnts = [], []
    for gi, w in enumerate(POOL_WINDOWS):
        cols = slice(gi * POOL_GC, (gi + 1) * POOL_GC)
        s = u[:, cols]
        for k in range(1, w):
            s = s + ext_ref[hist - k:hist - k + tm, cols]
        s_list.append(s)
        cnts.append(jnp.minimum(pos + 1, w).astype(F32))
    _pool_tail(s_list, u, cnts, w_ref, sc_ref, y_ref, True)


def _pool_prompt(u3, w_pool, pool_scale, tm):
    b, t, _ = u3.shape
    return pl.pallas_call(
        functools.partial(_pool_prompt_kernel, tm=tm),
        grid=(b, t // tm),
        in_specs=[pl.BlockSpec((1, tm, POOL_WIDTH), lambda i, j: (i, j, 0)),
                  pl.BlockSpec((4, POOL_GC, POOL_GC), lambda i, j: (0, 0, 0)),
                  pl.BlockSpec((1, POOL_WIDTH), lambda i, j: (0, 0))],
        out_specs=pl.BlockSpec((1, tm, POOL_WIDTH), lambda i, j: (i, j, 0)),
        out_shape=jax.ShapeDtypeStruct((b, t, POOL_WIDTH), F32),
        scratch_shapes=[pltpu.VMEM((tm + 16, POOL_WIDTH), F32)],
        compiler_params=_params(("arbitrary", "arbitrary")),
        name="pool_prompt",
    )(u3, w_pool, pool_scale)


def _pool_sample_kernel(st_ref, u_ref, w_ref, sc_ref, y_ref, *, pos0):
    u = u_ref[...]
    s_list, cnts = [], []
    for gi, w in enumerate(POOL_WINDOWS):
        cols = slice(gi * POOL_GC, (gi + 1) * POOL_GC)
        s = u[:, cols]
        for k in range(1, w):
            s = s + st_ref[POOL_STATE - k, :, cols]
        s_list.append(s)
        cnts.append(float(min(pos0 + 1, w)))
    _pool_tail(s_list, u, cnts, w_ref, sc_ref, y_ref, False)


def _pool_sample(st_t, u, w_pool, pool_scale, pos0):
    n = u.shape[0]
    return pl.pallas_call(
        functools.partial(_pool_sample_kernel, pos0=pos0),
        grid=(1,),
        in_specs=[pl.BlockSpec((POOL_STATE, n, POOL_WIDTH), lambda i: (0, 0, 0)),
                  pl.BlockSpec((n, POOL_WIDTH), lambda i: (0, 0)),
                  pl.BlockSpec((4, POOL_GC, POOL_GC), lambda i: (0, 0, 0)),
                  pl.BlockSpec((1, POOL_WIDTH), lambda i: (0, 0))],
        out_specs=pl.BlockSpec((n, POOL_WIDTH), lambda i: (0, 0)),
        out_shape=jax.ShapeDtypeStruct((n, POOL_WIDTH), F32),
        compiler_params=_params(("arbitrary",)),
        name="pool_sample",
    )(st_t, u, w_pool, pool_scale)


def _compress_rows(load_half, wbig_ref, pe_ref, wpe_ref, n_chunk):
    pe_row = _dot(pe_ref[...], wpe_ref[...])[0:1]
    halves = []
    for c in range(2):
        x = jnp.concatenate([load_half(c, s).astype(BF16) for s in range(CMP_STRIDE)], axis=1)
        y = jnp.dot(x, wbig_ref[c], preferred_element_type=F32)
        halves.append(y[:, :LANE] + pltpu.roll(y[:, LANE:], n_chunk - 1, axis=0))
    return jnp.concatenate(halves, axis=1) + pe_row


def _compress_prompt_kernel(k_ref, v_ref, wbig_ref, pe_ref, wpe_ref, out_ref, *, n_chunk):
    load = lambda c, s: (k_ref, v_ref)[c][0, pl.ds(s, n_chunk, stride=CMP_STRIDE), :]
    out_ref[0] = _compress_rows(load, wbig_ref, pe_ref, wpe_ref, n_chunk)


def _compress_prompt(kv4, wbig, pe8, wpe):
    b, t, _ = kv4.shape
    n_chunk = t // CMP_STRIDE
    return pl.pallas_call(
        functools.partial(_compress_prompt_kernel, n_chunk=n_chunk),
        grid=(b,),
        in_specs=[pl.BlockSpec((1, t, LANE), lambda i: (i, 0, 0)),
                  pl.BlockSpec((1, t, LANE), lambda i: (i, 0, 1)),
                  pl.BlockSpec(wbig.shape, lambda i: (0, 0, 0)),
                  pl.BlockSpec(pe8.shape, lambda i: (0, 0)),
                  pl.BlockSpec(wpe.shape, lambda i: (0, 0))],
        out_specs=pl.BlockSpec((1, n_chunk, KV_W), lambda i: (i, 0, 0)),
        out_shape=jax.ShapeDtypeStruct((b, n_chunk, KV_W), F32),
        compiler_params=_params(("arbitrary",)),
        name="compress_prompt",
    )(kv4, kv4, wbig, pe8, wpe)


def _stack_group_queries(q, g):
    lane = lax.broadcasted_iota(I32, (q.shape[0], LANE), 1)
    parts = []
    for hh in range(GROUP):
        h = GROUP * g + hh
        ch = q[:, (h // 2) * LANE:(h // 2 + 1) * LANE]
        if h % 2 != g:
            ch = pltpu.roll(ch, HEAD_DIM, axis=1)
        parts.append(jnp.where((lane >> 6) == g, ch, 0.0))
    return jnp.concatenate(parts, axis=0).astype(BF16)


def _place_heads(o_g, weights, g, outs):
    lane = lax.broadcasted_iota(I32, (QBLK, LANE), 1)
    for hh in range(GROUP):
        h = GROUP * g + hh
        rs = slice(hh * QBLK, (hh + 1) * QBLK)
        piece = sum(w_[:, 0:1] * o_[rs] for w_, o_ in zip(weights(h), o_g))
        if h % 2 != g:
            piece = pltpu.roll(piece, HEAD_DIM, axis=1)
        piece = jnp.where((lane >> 6) == (h % 2), piece, 0.0)
        outs[h // 2] = piece if outs[h // 2] is None else outs[h // 2] + piece


def _nsa_dense_kernel(q_ref, gt_ref, kvc_ref, kw_ref, vw_ref, part_ref, sel_ref, *, t_len):
    i = pl.program_id(1)
    t0 = i * QBLK
    nc = t_len // CMP_STRIDE
    ns = t_len // SEL_BLOCK
    wl = min(WINDOW + QBLK, t_len)
    q = q_ref[0] * (HEAD_DIM ** -0.5)
    gt = gt_ref[0]
    pos_q = t0 + lax.broadcasted_iota(I32, (QBLK, 1), 0)

    oj = lax.broadcasted_iota(I32, (ns, nc), 0)
    on = lax.broadcasted_iota(I32, (ns, nc), 1)
    ovl_t = jnp.where((on * CMP_STRIDE + CMP_LEN - 1 >= oj * SEL_BLOCK)
                      & (on * CMP_STRIDE <= oj * SEL_BLOCK + SEL_BLOCK - 1), 1.0, 0.0).astype(BF16)
    jb = lax.broadcasted_iota(I32, (ns, QBLK), 0)
    jq = (t0 + lax.broadcasted_iota(I32, (ns, QBLK), 1)) >> 6
    forced = (jb == 0) | (jb == jq) | (jb == jq - 1)

    kc_all = kvc_ref[0, :, 0:LANE].astype(BF16)
    vc_all = kvc_ref[0, :, LANE:2 * LANE].astype(BF16)
    ws = pl.multiple_of(jnp.maximum(t0 - WINDOW, 0), QBLK)
    kwc = kw_ref[0, :, pl.ds(ws, wl)].astype(BF16)
    vwc = vw_ref[0, :, pl.ds(ws, wl)].astype(BF16)

    ncol = lax.broadcasted_iota(I32, (1, nc), 1)
    bias_c = jnp.where((ncol * CMP_STRIDE + CMP_LEN - 1) <= pos_q, 0.0, NEG)
    bias_c = jnp.concatenate([bias_c] * GROUP, axis=0)
    dpos = pos_q - (ws + lax.broadcasted_iota(I32, (1, wl), 1))
    bias_w = jnp.where((dpos >= 0) & (dpos <= WINDOW), 0.0, NEG)
    bias_w = jnp.concatenate([bias_w] * GROUP, axis=0)

    def softmax_parts(s):
        m = jnp.max(s, axis=-1, keepdims=True)
        e = jnp.exp(s - m)
        return e, jnp.where(m > 0.5 * NEG, 1.0 / jnp.sum(e, axis=-1, keepdims=True), 0.0)

    outs = [None] * (NSA_WIDTH // LANE)
    for g in range(KV_HEADS):
        qg = _stack_group_queries(q, g)

        e_c, inv_c = softmax_parts(_dot_nt(qg, kc_all) + bias_c)
        o_c = _dot(e_c, vc_all) * inv_c

        p_c = e_c * inv_c
        psum = p_c[0:QBLK] + p_c[QBLK:2 * QBLK] + p_c[2 * QBLK:3 * QBLK] + p_c[3 * QBLK:4 * QBLK]
        imp_t = sum(lax.dot_general(ovl_t, part, (((1,), (1,)), ((), ())), preferred_element_type=F32)
                    for part in _split3(psum))
        x = jnp.where(forced, jnp.inf, jnp.where(jb > jq, -jnp.inf, imp_t))
        cnt = jnp.zeros((ns, QBLK), I32)
        for jp in range(ns):
            row = x[jp:jp + 1, :]
            beats = (row > x) | ((row == x) & (jb > jp))
            cnt = cnt + jnp.where(beats, 1, 0)
        sel_ref[0, 0, g * ns:(g + 1) * ns, :] = jnp.where(cnt < N_SELECT, 1.0, 0.0)

        e_w, inv_w = softmax_parts(_dot(qg, kwc) + bias_w)
        o_w = _dot_nt(e_w, vwc) * inv_w

        _place_heads((o_c, o_w), lambda h: (gt[:, 3 * h:3 * h + 1], gt[:, 3 * h + 2:3 * h + 3]), g, outs)
    part_ref[0] = jnp.concatenate(outs, axis=1)


def _nsa_select_kernel(q_ref, gt_ref, sel_ref, part_ref, ks_ref, vs_ref, o_ref, *, t_len, kc):
    i = pl.program_id(1)
    t0 = i * QBLK
    ns = t_len // SEL_BLOCK
    rows = GROUP * QBLK
    gt = gt_ref[0]
    kcs = kc
    n_small = (t0 + QBLK + kcs[-1] - 1) // kcs[-1]
    gt_t = gt.T
    blocks = []
    q_t =(q_ref[0] * (HEAD_DIM ** -0.5)).T
    zero_half = jnp.zeros((HEAD_DIM, QBLK), F32)
    for g in range(KV_HEADS):
        heads = []
        for hh in range(GROUP):
            q_h = q_t[(GROUP * g + hh) * HEAD_DIM:(GROUP * g + hh + 1) * HEAD_DIM]
            heads.append(jnp.concatenate([q_h, zero_half] if g == 0 else [zero_half, q_h], axis=0))
        qg_t = jnp.concatenate(heads, axis=1).astype(BF16)

        def sel_chunk(kcz, base, c, carry):
            m_prev, l_prev, acc = carry
            k0 = pl.multiple_of(base + c * kcz, kcs[-1])
            nblk = kcz // SEL_BLOCK
            s = _dot(ks_ref[0, pl.ds(k0, kcz), :], qg_t)
            j0 = pl.multiple_of(g * ns + k0 // SEL_BLOCK, kcs[-1] // SEL_BLOCK)
            chosen = sel_ref[0, 0, pl.ds(j0, nblk), :]
            chosen = jnp.concatenate([jnp.broadcast_to(chosen[j:j + 1], (SEL_BLOCK, QBLK))
                                      for j in range(nblk)], axis=0)
            key_pos = k0 + lax.broadcasted_iota(I32, (kcz, QBLK), 0)
            qry_pos = t0 + lax.broadcasted_iota(I32, (kcz, QBLK), 1)
            bias = jnp.where((chosen > 0.5) & (key_pos <= qry_pos), 0.0, NEG)
            s = s + jnp.concatenate([bias] * GROUP, axis=1)
            m_new = jnp.maximum(m_prev, jnp.max(s, axis=0, keepdims=True))
            alpha = jnp.exp(m_prev - m_new)
            p = jnp.exp(s - m_new)
            l_new = alpha * l_prev + jnp.sum(p, axis=0, keepdims=True)
            return m_new, l_new, alpha * acc + _dot(vs_ref[0, :, pl.ds(k0, kcz)], p)

        carry = (jnp.full((1, rows), NEG, F32), jnp.zeros((1, rows), F32), jnp.zeros((LANE, rows), F32))
        done = 0
        for kcz in kcs:
            count = (n_small * kcs[-1] - done) // kcz
            carry = lax.fori_loop(0, count, functools.partial(sel_chunk, kcz, done), carry)
            done = done + count * kcz
        _, l_s, acc_s = carry
        o_t = acc_s[g * HEAD_DIM:(g + 1) * HEAD_DIM] / l_s
        for hh in range(GROUP):
            h = GROUP * g + hh
            blocks.append(o_t[:, hh * QBLK:(hh + 1) * QBLK] * gt_t[3 * h + 1:3 * h + 2])
    o_ref[0] = part_ref[0] + jnp.concatenate(blocks, axis=0).T


def _nsa_prompt(q3, gt3, kvc, ks_tok, kv4t, kvwt, kc):
    b, t, _ = q3.shape
    kc = tuple(k for k in kc if k <= t)
    assert all(t % k == 0 for k in kc) and t % QBLK == 0
    nq = t // QBLK
    ns = t // SEL_BLOCK
    slab = lambda c: pl.BlockSpec((1, LANE, t), lambda i, j: (i, c, 0))
    tile = lambda w_: pl.BlockSpec((1, QBLK, w_), lambda i, j: (i, j, 0))
    sel_spec = pl.BlockSpec((1, 1, KV_HEADS * ns, QBLK), lambda i, j: (i, j, 0, 0))
    part, sel = pl.pallas_call(
        functools.partial(_nsa_dense_kernel, t_len=t),
        grid=(b, nq),
        in_specs=[tile(NSA_WIDTH), tile(GATE_PAD),
                  pl.BlockSpec((1, t // CMP_STRIDE, KV_W), lambda i, j: (i, 0, 0)),
                  slab(0), slab(1)],
        out_specs=[tile(NSA_WIDTH), sel_spec],
        out_shape=[jax.ShapeDtypeStruct((b, t, NSA_WIDTH), F32),
                   jax.ShapeDtypeStruct((b, nq, KV_HEADS * ns, QBLK), F32)],
        compiler_params=_params(("arbitrary", "arbitrary")),
        name="nsa_dense",
    )(q3, gt3, kvc, kvwt, kvwt)
    return pl.pallas_call(
        functools.partial(_nsa_select_kernel, t_len=t, kc=kc),
        grid=(b, nq),
        in_specs=[tile(NSA_WIDTH), tile(GATE_PAD), sel_spec, tile(NSA_WIDTH),
                  pl.BlockSpec((1, t, LANE), lambda i, j: (i, 0, 0)), slab(3)],
        out_specs=tile(NSA_WIDTH),
        out_shape=jax.ShapeDtypeStruct((b, t, NSA_WIDTH), F32),
        compiler_params=_params(("arbitrary", "arbitrary")),
        name="nsa_select",
    )(q3, gt3, sel, part, ks_tok, kv4t)


def _xattn_prompt_kernel(x_ref, yp_ref, yn_ref, wo_ref, g_ref, wq_ref, mem_ref, wxo_ref, o_ref):
    y = jnp.concatenate([yp_ref[0], yn_ref[0]], axis=1)
    h1 = x_ref[0] + _dot(y, wo_ref[...])
    qx = _dot(_rms(h1, g_ref[...]), wq_ref[...]) * (X_HEAD_DIM ** -0.5)
    xw = X_HEADS * X_HEAD_DIM
    outs = []
    for h in range(X_HEADS):
        cs = slice(h * X_HEAD_DIM, (h + 1) * X_HEAD_DIM)
        s = _dot_nt(qx[:, cs], mem_ref[0, :, cs])
        e = jnp.exp(s - jnp.max(s, axis=-1, keepdims=True))
        p = e / jnp.sum(e, axis=-1, keepdims=True)
        outs.append(_dot(p, mem_ref[0, :, xw + h * X_HEAD_DIM:xw + (h + 1) * X_HEAD_DIM]))
    o_ref[0] = h1 + _dot(jnp.concatenate(outs, axis=1), wxo_ref[...])


def _xattn_prompt(x3, yp, yn, w_out, g_x, w_xq, mem_kv, w_xo, tm):
    b, t, d = x3.shape
    m = mem_kv.shape[1]
    full = lambda a: pl.BlockSpec(a.shape, lambda i, j: (0,) * a.ndim)
    return pl.pallas_call(
        _xattn_prompt_kernel,
        grid=(b, t // tm),
        in_specs=[pl.BlockSpec((1, tm, d), lambda i, j: (i, j, 0)),
                  pl.BlockSpec((1, tm, POOL_WIDTH), lambda i, j: (i, j, 0)),
                  pl.BlockSpec((1, tm, NSA_WIDTH), lambda i, j: (i, j, 0)),
                  full(w_out), full(g_x), full(w_xq),
                  pl.BlockSpec((1, m, mem_kv.shape[2]), lambda i, j: (i, 0, 0)),
                  full(w_xo)],
        out_specs=pl.BlockSpec((1, tm, d), lambda i, j: (i, j, 0)),
        out_shape=jax.ShapeDtypeStruct((b, t, d), F32),
        compiler_params=_params(("arbitrary", "arbitrary")),
        name="xattn_prompt",
    )(x3, yp, yn, w_out, g_x, w_xq, mem_kv, w_xo)


def _ffn_core(j, nj, h_ref, gf_ref, wg_ref, wv_ref, cw_ref, cb_ref, wd_ref, gfin_ref, y_ref,
              xn_ref, acc_ref, prev_rows):
    @pl.when(j == 0)
    def _():
        xn_ref[...] = _rms(h_ref[...], gf_ref[...]).astype(BF16)
        acc_ref[...] = jnp.zeros(acc_ref.shape, F32)

    xn = xn_ref[...]
    up_g = jnp.dot(xn, wg_ref[...], preferred_element_type=F32)
    up_v = jnp.dot(xn, wv_ref[...], preferred_element_type=F32)
    g1, g2, v1, v2 = prev_rows(up_g, up_v)
    cg = cb_ref[0:1, :] + cw_ref[0, 0:1, :] * g2 + cw_ref[1, 0:1, :] * g1 + cw_ref[2, 0:1, :] * up_g
    cv = cb_ref[1:2, :] + cw_ref[0, 1:2, :] * v2 + cw_ref[1, 1:2, :] * v1 + cw_ref[2, 1:2, :] * up_v
    act = cg / (1.0 + jnp.exp(-cg)) * cv
    acc_ref[...] += _dot(act, wd_ref[...])

    @pl.when(j == nj - 1)
    def _():
        y_ref[...] = _rms(h_ref[...] + acc_ref[...], gfin_ref[...])

    return up_g, up_v


def _ffn_prompt_kernel(h_ref, gf_ref, wg_ref, wv_ref, cw_ref, cb_ref, wd_ref, gfin_ref,
                       y_ref, st_ref, xn_ref, acc_ref, carry_ref, *, tiles_per_seq, nj, tm):
    r = pl.program_id(0)
    j = pl.program_id(1)
    row = lax.broadcasted_iota(I32, (tm, 1), 0)

    @pl.when((r % tiles_per_seq) == 0)
    def _():
        carry_ref[j] = jnp.zeros(carry_ref.shape[1:], F32)

    def prev_rows(up_g, up_v):
        res = []
        for half, up in enumerate((up_g, up_v)):
            prev = carry_ref[j, half]
            m1 = jnp.where(row == 0, prev[7:8], pltpu.roll(up, 1, axis=0))
            m2 = jnp.where(row == 0, prev[6:7], jnp.where(row == 1, prev[7:8], pltpu.roll(up, 2, axis=0)))
            res += [m1, m2]
        return res

    up_g, up_v = _ffn_core(j, nj, h_ref, gf_ref, wg_ref, wv_ref, cw_ref, cb_ref, wd_ref, gfin_ref,
                           y_ref, xn_ref, acc_ref, prev_rows)
    for half, up in enumerate((up_g, up_v)):
        carry_ref[j, half] = up[tm - 8:tm]
        st_ref[0, half] = up[tm - 8:tm]


def _ffn_specs(d, tf, nj):
    return [pl.BlockSpec((1, d), lambda r, j: (0, 0)),
            pl.BlockSpec((d, tf), lambda r, j: (0, j)),
            pl.BlockSpec((d, tf), lambda r, j: (0, nj + j)),
            pl.BlockSpec((3, 2, tf), lambda r, j: (0, 0, j)),
            pl.BlockSpec((2, tf), lambda r, j: (0, j)),
            pl.BlockSpec((tf, d), lambda r, j: (j, 0)),
            pl.BlockSpec((1, d), lambda r, j: (0, 0))]


def _ffn_prompt(h2, g_ffn, w_up, cw3, cb2, w_down, g_final, batch, tm, tf):
    n, d = h2.shape
    nj = D_FF // tf
    tps = n // batch // tm
    return pl.pallas_call(
        functools.partial(_ffn_prompt_kernel, tiles_per_seq=tps, nj=nj, tm=tm),
        grid=(n // tm, nj),
        in_specs=[pl.BlockSpec((tm, d), lambda r, j: (r, 0))] + _ffn_specs(d, tf, nj),
        out_specs=[pl.BlockSpec((tm, d), lambda r, j: (r, 0)),
                   pl.BlockSpec((1, 2, 8, tf), lambda r, j: (r, 0, 0, j))],
        out_shape=[jax.ShapeDtypeStruct((n, d), F32),
                   jax.ShapeDtypeStruct((n // tm, 2, 8, D_FF), F32)],
        scratch_shapes=[pltpu.VMEM((tm, d), BF16), pltpu.VMEM((tm, d), F32),
                        pltpu.VMEM((nj, 2, 8, tf), F32)],
        compiler_params=_params(("arbitrary", "arbitrary")),
        name="ffn_prompt",
    )(h2, g_ffn, w_up, w_up, cw3, cb2, w_down, g_final)


def _ffn_sample_kernel(h_ref, gf_ref, wg_ref, wv_ref, cw_ref, cb_ref, wd_ref, gfin_ref, sg_ref, sv_ref,
                       y_ref, up_ref, xn_ref, acc_ref, *, nj):
    j = pl.program_id(1)
    prev_rows = lambda up_g, up_v: (sg_ref[1], sg_ref[0], sv_ref[1], sv_ref[0])
    up_g, up_v = _ffn_core(j, nj, h_ref, gf_ref, wg_ref, wv_ref, cw_ref, cb_ref, wd_ref, gfin_ref,
                           y_ref, xn_ref, acc_ref, prev_rows)
    up_ref[0] = up_g
    up_ref[1] = up_v


def _ffn_sample(h2, g_ffn, w_up, cw3, cb2, w_down, g_final, st_t, tf):
    n, d = h2.shape
    nj = D_FF // tf
    return pl.pallas_call(
        functools.partial(_ffn_sample_kernel, nj=nj),
        grid=(1, nj),
        in_specs=[pl.BlockSpec((n, d), lambda r, j: (0, 0))] + _ffn_specs(d, tf, nj)
        + [pl.BlockSpec((2, n, tf), lambda r, j: (0, 0, j)),
           pl.BlockSpec((2, n, tf), lambda r, j: (0, 0, nj + j))],
        out_specs=[pl.BlockSpec((n, d), lambda r, j: (0, 0)),
                   pl.BlockSpec((2, n, tf), lambda r, j: (0, 0, j))],
        out_shape=[jax.ShapeDtypeStruct((n, d), F32),
                   jax.ShapeDtypeStruct((2, n, D_FF), F32)],
        scratch_shapes=[pltpu.VMEM((n, d), BF16), pltpu.VMEM((n, d), F32)],
        compiler_params=_params(("arbitrary", "arbitrary")),
        name="ffn_sample",
    )(h2, g_ffn, w_up, w_up, cw3, cb2, w_down, g_final, st_t, st_t)


def _head_rows(qrow):
    rowi = lax.broadcasted_iota(I32, (N_HEADS, LANE), 0)
    lane = lax.broadcasted_iota(I32, (N_HEADS, LANE), 1)
    x = jnp.zeros((N_HEADS, LANE), F32)
    for c in range(NSA_WIDTH // LANE):
        x = jnp.where((rowi >> 1) == c, jnp.broadcast_to(qrow[:, c * LANE:(c + 1) * LANE], (N_HEADS, LANE)), x)
    x = jnp.where((rowi & 1) != (rowi >> 2), pltpu.roll(x, HEAD_DIM, axis=1), x)
    return jnp.where((lane >> 6) == (rowi >> 2), x, 0.0)


def _page_copy(cache_ref, raw_ref, sem_ref, phys, p, slot):
    return pltpu.make_async_copy(cache_ref.at[phys, pl.ds(0, KV_W), :], raw_ref.at[slot, p], sem_ref.at[slot, p])


def _nsa_sample_cmp_kernel(pt_ref, q_ref, cache_ref, wbig_ref, pe_ref, wpe_ref, oc_ref, imp_ref,
                           raw_ref, tok_ref, sem_ref, *, n_pages, n_batch, page):
    b = pl.program_id(0)
    slot = b % 2
    n_tok = n_pages * page
    n_chunk = n_tok // CMP_STRIDE

    def start_all(bb, sl):
        def body(p, _):
            _page_copy(cache_ref, raw_ref, sem_ref, pt_ref[bb * n_pages + p], p, sl).start()
            return 0
        lax.fori_loop(0, n_pages, body, 0)

    @pl.when(b == 0)
    def _():
        start_all(0, 0)

    @pl.when(b + 1 < n_batch)
    def _():
        start_all(b + 1, 1 - slot)

    chunks_per_page = page // CMP_STRIDE

    def to_token_major(pg, _):
        for pp in range(PAGE_UNROLL):
            _page_copy(cache_ref, raw_ref, sem_ref, 0, pg * PAGE_UNROLL + pp, slot).wait()
        for pp in range(PAGE_UNROLL):
            p = pg * PAGE_UNROLL + pp
            xt = raw_ref[slot, p]
            for half in range(2):
                x = xt[half * LANE:(half + 1) * LANE].T
                for nl in range(chunks_per_page):
                    r0 = pl.multiple_of((p * chunks_per_page + nl) * CHUNK_PITCH, 8)
                    tok_ref[half, pl.ds(r0, CMP_STRIDE), :] = x[nl * CMP_STRIDE:(nl + 1) * CMP_STRIDE]
        return 0
    lax.fori_loop(0, n_pages // PAGE_UNROLL, to_token_major, 0)

    load = lambda c, s: tok_ref[c, pl.ds(s, n_chunk, stride=CHUNK_PITCH), :]
    kvc = _compress_rows(load, wbig_ref, pe_ref, wpe_ref, n_chunk)
    q8 = _head_rows(q_ref[0] * (HEAD_DIM ** -0.5))
    s = _dot_nt(q8, kvc[:, 0:LANE])
    ncol = lax.broadcasted_iota(I32, (1, n_chunk), 1)
    p = _masked_softmax(s, (ncol * CMP_STRIDE + CMP_LEN - 1) <= n_tok)
    oc_ref[0] = _dot(p, kvc[:, LANE:2 * LANE])
    rowi = lax.broadcasted_iota(I32, (N_HEADS, n_chunk), 0)
    psum = jnp.where(rowi == 0, jnp.sum(p[0:GROUP], axis=0, keepdims=True),
                     jnp.where(rowi == 1, jnp.sum(p[GROUP:2 * GROUP], axis=0, keepdims=True), 0.0))
    ns_pad = imp_ref.shape[2]
    on = lax.broadcasted_iota(I32, (n_chunk, ns_pad), 0)
    oj = lax.broadcasted_iota(I32, (n_chunk, ns_pad), 1)
    ovl = jnp.where((on * CMP_STRIDE + CMP_LEN - 1 >= oj * SEL_BLOCK)
                    & (on * CMP_STRIDE <= oj * SEL_BLOCK + SEL_BLOCK - 1), 1.0, 0.0).astype(BF16)
    imp_ref[0] = sum(jnp.dot(part, ovl, preferred_element_type=F32) for part in _split3(psum))


def _nsa_sample_cmp(pt_flat, q3, cache3, wbig, pe8, wpe, n_pages, ns_pad):
    nb = q3.shape[0]
    page = cache3.shape[2]
    assert page == LANE
    n_tok = n_pages * page
    gs = pltpu.PrefetchScalarGridSpec(
        num_scalar_prefetch=1, grid=(nb,),
        in_specs=[pl.BlockSpec((1, 1, NSA_WIDTH), lambda i, pt: (i, 0, 0)),
                  pl.BlockSpec(memory_space=pl.ANY),
                  pl.BlockSpec(wbig.shape, lambda i, pt: (0, 0, 0)),
                  pl.BlockSpec(pe8.shape, lambda i, pt: (0, 0)),
                  pl.BlockSpec(wpe.shape, lambda i, pt: (0, 0))],
        out_specs=[pl.BlockSpec((1, N_HEADS, LANE), lambda i, pt: (i, 0, 0)),
                   pl.BlockSpec((1, N_HEADS, ns_pad), lambda i, pt: (i, 0, 0))],
        scratch_shapes=[pltpu.VMEM((2, n_pages, KV_W, page), F32), pltpu.VMEM((2, n_tok // CMP_STRIDE * CHUNK_PITCH, LANE), F32),
                        pltpu.SemaphoreType.DMA((2, n_pages))])
    return pl.pallas_call(
        functools.partial(_nsa_sample_cmp_kernel, n_pages=n_pages, n_batch=nb, page=page),
        grid_spec=gs,
        out_shape=[jax.ShapeDtypeStruct((nb, N_HEADS, LANE), F32),
                   jax.ShapeDtypeStruct((nb, N_HEADS, ns_pad), F32)],
        compiler_params=_params(("arbitrary",)),
        name="nsa_sample_cmp",
    )(pt_flat, q3, cache3, wbig, pe8, wpe)


def _topk_sample_kernel(imp_ref, idx_ref, *, jq, k_past):
    x = imp_ref[...]
    col = lax.broadcasted_iota(I32, x.shape, 1)
    x = jnp.where((col == 0) | (col == jq - 1), jnp.inf, jnp.where(col >= jq, -jnp.inf, x))
    lane = lax.broadcasted_iota(I32, idx_ref.shape, 1)
    out = jnp.zeros(idx_ref.shape, I32)
    big = jnp.int32(1 << 30)
    for k in range(k_past):
        m = jnp.max(x, axis=-1, keepdims=True)
        idx = jnp.min(jnp.where(x == m, col, big), axis=-1, keepdims=True)
        out = jnp.where(lane == k, idx, out)
        x = jnp.where(col == idx, -jnp.inf, x)
    idx_ref[...] = out


def _topk_sample(imp2, jq, k_past):
    n, w = imp2.shape
    return pl.pallas_call(
        functools.partial(_topk_sample_kernel, jq=jq, k_past=k_past),
        grid=(1,),
        in_specs=[pl.BlockSpec((n, w), lambda i: (0, 0))],
        out_specs=pl.BlockSpec((n, LANE), lambda i: (0, 0)),
        out_shape=jax.ShapeDtypeStruct((n, LANE), I32),
        compiler_params=_params(("arbitrary",)),
        name="topk_sample",
    )(imp2)


def _blk_copy(cache_ref, buf_ref, sem_ref, phys, kv, g, r, slot):
    row0 = KV_W + kv * LANE + g * HEAD_DIM
    return pltpu.make_async_copy(cache_ref.at[phys, pl.ds(row0, HEAD_DIM), :],
                                 buf_ref.at[slot, kv, pl.ds(g * HEAD_DIM, HEAD_DIM), pl.ds(r * LANE, LANE)],
                                 sem_ref.at[slot])


def _nsa_sample_sel_kernel(idx_ref, pt_ref, q_ref, gt_ref, oc_ref, win_ref, kvn4_ref, kvnw_ref, cache_ref,
                           o_ref, wout_ref, buf_ref, sem_ref, *, n_pages, n_batch, page, k_past):
    b = pl.program_id(0)
    slot = b % 2
    per_page = page // SEL_BLOCK
    nl = k_past * page

    def for_blocks(bb, fn):
        def body(r, _):
            for g in range(KV_HEADS):
                j = idx_ref[(bb * KV_HEADS + g) * k_past + r]
                phys = pt_ref[bb * n_pages + j // per_page]
                for kv in range(2):
                    fn(phys, kv, g, r)
            return 0
        lax.fori_loop(0, k_past, body, 0)

    def start_all(bb, sl):
        for_blocks(bb, lambda phys, kv, g, r: _blk_copy(cache_ref, buf_ref, sem_ref, phys, kv, g, r, sl).start())

    @pl.when(b == 0)
    def _():
        start_all(0, 0)

    @pl.when(b + 1 < n_batch)
    def _():
        start_all(b + 1, 1 - slot)

    def wait_body(r, _):
        for g in range(KV_HEADS):
            for kv in range(2):
                _blk_copy(cache_ref, buf_ref, sem_ref, 0, kv, g, r, slot).wait()
        return 0
    lax.fori_loop(0, k_past, wait_body, 0)

    q8 = _head_rows(q_ref[0] * (HEAD_DIM ** -0.5))
    rowi = lax.broadcasted_iota(I32, (N_HEADS, 1), 0)
    lane = lax.broadcasted_iota(I32, (N_HEADS, LANE), 1)
    own = (lane >> 6) == (rowi >> 2)
    mine = jnp.where(lax.broadcasted_iota(I32, (N_HEADS, n_batch), 1) == b, 0.0, -jnp.inf)

    def attend(k_t, v_t, bias):
        s = _dot(q8, k_t) + bias
        e = jnp.exp(s - jnp.max(s, axis=-1, keepdims=True))
        return _dot_nt(e, v_t) / jnp.sum(e, axis=-1, keepdims=True)

    lane_s = lax.broadcasted_iota(I32, (1, nl), 1)
    halves = []
    for g in range(KV_HEADS):
        hv = jnp.zeros((1, nl), I32)
        for r in range(k_past):
            hv = jnp.where((lane_s // page) == r, idx_ref[(b * KV_HEADS + g) * k_past + r] % per_page, hv)
        halves.append(hv)
    vis = ((lane_s % page) // SEL_BLOCK) == jnp.where(rowi < GROUP, halves[0], halves[1])
    o_s = attend(jnp.concatenate([buf_ref[slot, 0], kvn4_ref[0, KV_W:KV_W + LANE]], axis=1),
                 jnp.concatenate([buf_ref[slot, 1], kvn4_ref[0, KV_W + LANE:2 * KV_W]], axis=1),
                 jnp.concatenate([jnp.where(vis, 0.0, -jnp.inf), mine], axis=1))

    cw = win_ref[0]
    wlen = cw.shape[1]
    o_w = attend(jnp.concatenate([cw[0:LANE], kvnw_ref[0, 0:LANE]], axis=1),
                 jnp.concatenate([cw[LANE:2 * LANE], kvnw_ref[0, LANE:2 * LANE]], axis=1),
                 jnp.concatenate([jnp.zeros((N_HEADS, wlen), F32), mine], axis=1))
    lane_n = lax.broadcasted_iota(I32, (KV_W, n_batch), 1)
    new_col = jnp.sum(jnp.where(lane_n == b, kvnw_ref[0], 0.0), axis=1, keepdims=True)
    lane_w = lax.broadcasted_iota(I32, (KV_W, wlen), 1)
    wout_ref[0] = jnp.where(lane_w == wlen - 1, new_col, pltpu.roll(cw, wlen - 1, axis=1))

    gt = jnp.broadcast_to(gt_ref[0], (N_HEADS, GATE_PAD))
    gate = lambda br: jnp.sum(jnp.where(lane == rowi * N_BRANCH + br, gt, 0.0), axis=-1, keepdims=True)
    o = gate(0) * oc_ref[0] + gate(1) * o_s + gate(2) * o_w
    o_ref[0] = jnp.where(own, o, 0.0)


def _nsa_sample_sel(idx_flat, pt_flat, q3, gt3, oc, win_t, kvn4_t, kvnw_t, cache3, n_pages, k_past):
    nb = q3.shape[0]
    page = cache3.shape[2]
    wlen = win_t.shape[2]
    assert page == LANE
    per = lambda *blk: pl.BlockSpec((1,) + blk, lambda i, a, c: (i, 0, 0))
    whole = lambda arr: pl.BlockSpec(arr.shape, lambda i, a, c: (0, 0, 0))
    gs = pltpu.PrefetchScalarGridSpec(
        num_scalar_prefetch=2, grid=(nb,),
        in_specs=[per(1, NSA_WIDTH), per(1, GATE_PAD), per(N_HEADS, LANE), per(KV_W, wlen),
                  whole(kvn4_t), whole(kvnw_t), pl.BlockSpec(memory_space=pl.ANY)],
        out_specs=[per(N_HEADS, LANE), per(KV_W, wlen)],
        scratch_shapes=[pltpu.VMEM((2, 2, LANE, k_past * page), F32), pltpu.SemaphoreType.DMA((2,))])
    return pl.pallas_call(
        functools.partial(_nsa_sample_sel_kernel, n_pages=n_pages, n_batch=nb, page=page, k_past=k_past),
        grid_spec=gs,
        out_shape=[jax.ShapeDtypeStruct((nb, N_HEADS, LANE), F32),
                   jax.ShapeDtypeStruct((nb, KV_W, wlen), F32)],
        compiler_params=_params(("arbitrary",)),
        name="nsa_sample_sel",
    )(idx_flat, pt_flat, q3, gt3, oc, win_t, kvn4_t, kvnw_t, cache3)


def _mix_sample_kernel(x_ref, yp_ref, yn_ref, wo_ref, g_ref, wq_ref, h1_ref, qx_ref):
    y = jnp.concatenate([yp_ref[...], yn_ref[...]], axis=1)
    h1 = x_ref[...] + _dot(y, wo_ref[...])
    h1_ref[...] = h1
    qx_ref[...] = _dot(_rms(h1, g_ref[...]), wq_ref[...]) * (X_HEAD_DIM ** -0.5)


def _mix_sample(x, yp, yn, w_out, g_x, w_xq):
    n, d = x.shape
    full = lambda a: pl.BlockSpec(a.shape, lambda i: (0,) * a.ndim)
    args = (x, yp, yn, w_out, g_x, w_xq)
    return pl.pallas_call(
        _mix_sample_kernel, grid=(1,),
        in_specs=[full(a) for a in args],
        out_specs=[pl.BlockSpec((n, d), lambda i: (0, 0))] * 2,
        out_shape=[jax.ShapeDtypeStruct((n, d), F32)] * 2,
        compiler_params=_params(("arbitrary",)),
        name="mix_sample",
    )(*args)


def _xattn_sample_kernel(qx_ref, mem_ref, o_ref):
    n_mem = mem_ref.shape[1]
    n = n_mem * 8
    q = qx_ref[0]
    order = [(half, h) for half in range(X_HEAD_DIM // LANE) for h in range(X_HEADS)]
    col = lambda half, h: slice(h * X_HEAD_DIM + half * LANE, h * X_HEAD_DIM + (half + 1) * LANE)
    qt = jnp.concatenate([q[:, col(half, h)] for half, h in order], axis=0)
    x = mem_ref[0]
    a = _dot_nt(qt, x[:, 0].reshape(n, LANE))
    rowi = lax.broadcasted_iota(I32, (8, n), 0)
    lane = lax.broadcasted_iota(I32, (8, n), 1)
    own = (lane & 7) == rowi
    a = jnp.where(own, a, 0.0)
    other = pltpu.roll(a, X_HEADS, axis=0)
    s = a + jnp.where(rowi < X_HEADS, pltpu.roll(other, n - X_HEADS, axis=1), pltpu.roll(other, X_HEADS, axis=1))
    s = jnp.where(own, s, -jnp.inf)
    e = jnp.exp(s - jnp.max(s, axis=-1, keepdims=True))
    p = e / jnp.sum(e, axis=-1, keepdims=True)
    o8 = _dot(p, x[:, 1].reshape(n, LANE))
    o_ref[0] = jnp.concatenate([o8[half * X_HEADS + h:half * X_HEADS + h + 1]
                                for h in range(X_HEADS) for half in range(X_HEAD_DIM // LANE)], axis=1)


def _xattn_sample(qx3, mem5):
    nb, _, xw = qx3.shape
    return pl.pallas_call(
        _xattn_sample_kernel, grid=(nb,),
        in_specs=[pl.BlockSpec((1, 1, xw), lambda i: (i, 0, 0)),
                  pl.BlockSpec((1,) + mem5.shape[1:], lambda i: (i, 0, 0, 0, 0))],
        out_specs=pl.BlockSpec((1, 1, xw), lambda i: (i, 0, 0)),
        out_shape=jax.ShapeDtypeStruct((nb, 1, xw), F32),
        compiler_params=_params(("arbitrary",)),
        name="xattn_sample",
    )(qx3, mem5)


def _resid_proj_kernel(h_ref, o_ref, w_ref, y_ref):
    y_ref[...] = h_ref[...] + _dot(o_ref[...], w_ref[...])


def _resid_proj(h, o, w):
    n, d = h.shape
    full = lambda a: pl.BlockSpec(a.shape, lambda i: (0,) * a.ndim)
    return pl.pallas_call(
        _resid_proj_kernel, grid=(1,),
        in_specs=[full(h), full(o), full(w)],
        out_specs=pl.BlockSpec((n, d), lambda i: (0, 0)),
        out_shape=jax.ShapeDtypeStruct((n, d), F32),
        compiler_params=_params(("arbitrary",)),
        name="resid_proj",
    )(h, o, w)


def _prep_weights(g_mix, w_in, w_pool, pool_scale, w_cmp, pe_cmp, w_out, g_xattn, g_mem, w_xq, w_xkv,
                  w_xo, g_ffn, w_up, conv_w, conv_b, w_down, g_final):
    o_kv = POOL_WIDTH + NSA_WIDTH
    o_gate = o_kv + N_BRANCH * KV_W
    w_tok = jnp.concatenate([w_in[:, :o_kv + KV_W + LANE], jnp.pad(w_in[:, o_gate:], ((0, 0), (0, GATE_PAD - N_GATE)))],
                            axis=1)
    r = CMP_LEN // CMP_STRIDE
    w5 = w_cmp.reshape(2, r, CMP_STRIDE, HEAD_DIM, HEAD_DIM)
    eye = jnp.eye(2, dtype=F32)
    wbig = jnp.einsum('cisde,gG->csgdiGe', w5, eye).reshape(2, CMP_STRIDE * LANE, r * LANE)
    wpe = jnp.einsum('cke,cC->ckCe', w_cmp.reshape(2, CMP_LEN * HEAD_DIM, HEAD_DIM), eye)
    wpe = jnp.broadcast_to(wpe[:, :, :, None, :], (2, CMP_LEN * HEAD_DIM, 2, KV_HEADS, HEAD_DIM))
    row = lambda v: v.reshape(1, -1)
    return dict(
        g_mix=row(g_mix), w_tok=w_tok.astype(BF16), w_feat=w_in[:, o_kv:o_gate].T.astype(BF16),
        w_pool=w_pool.astype(BF16), pool_scale=row(pool_scale),
        wbig=wbig.astype(BF16), wpe=wpe.reshape(2 * CMP_LEN * HEAD_DIM, KV_W).astype(BF16),
        pe8=jnp.broadcast_to(pe_cmp.reshape(1, -1), (8, 2 * CMP_LEN * HEAD_DIM)),
        w_out=w_out.astype(BF16), g_xattn=row(g_xattn), g_mem=row(g_mem), w_xq=w_xq.astype(BF16),
        w_xkv=w_xkv.astype(BF16), w_xo=w_xo.astype(BF16), g_ffn=row(g_ffn), w_up=w_up.astype(BF16),
        cw3=conv_w.reshape(3, 2, D_FF), cb2=conv_b.reshape(2, D_FF), w_down=w_down.astype(BF16),
        g_final=row(g_final))


def _token_major(a_t, lead):
    b, _, t = a_t.shape
    nd = len(lead)
    return jnp.transpose(a_t.reshape((b,) + lead + (t,)), (0, nd + 1) + tuple(range(1, nd + 1)))


def _feature_major(a, nfeat):
    nd = a.ndim
    return jnp.transpose(a, (0,) + tuple(range(2, nd)) + (1,)).reshape(a.shape[0], nfeat, a.shape[1])


def _prompt_layer(x, mem, w, *, tm=512, tm_x=1024, tf=D_FF, kc=(2048, 1024, 512)):
    b, t, d = x.shape
    n = b * t
    u, q, kvc_tok, ks_tok, gt, kv4t, kvwt = _mix_in(x, w['g_mix'], w['w_tok'], w['w_feat'], tm, "mix_in_prompt")
    r3 = lambda a: a.reshape(b, t, a.shape[-1])
    u, q, kvc_tok, ks_tok, gt = map(r3, (u, q, kvc_tok, ks_tok, gt))
    y_pool = _pool_prompt(u, w['w_pool'], w['pool_scale'], tm)
    kvc = _compress_prompt(kvc_tok, w['wbig'], w['pe8'], w['wpe'])
    y_nsa = _nsa_prompt(q, gt, kvc, ks_tok, kv4t, kvwt, kc)
    m = mem.shape[1]
    (mem_kv,) = _rms_proj(mem.reshape(b * m, d), w['g_mem'], w['w_xkv'], (2 * X_HEADS * X_HEAD_DIM,), m,
                          name="mem_project")
    mem_kv = mem_kv.reshape(b, m, 2 * X_HEADS * X_HEAD_DIM)
    h2 = _xattn_prompt(x, y_pool, y_nsa, w['w_out'], w['g_xattn'], w['w_xq'], mem_kv, w['w_xo'], tm_x)
    y, st = _ffn_prompt(h2.reshape(n, d), w['g_ffn'], w['w_up'], w['cw3'], w['cb2'], w['w_down'],
                        w['g_final'], b, tm, tf)
    wlen = min(WINDOW, t)
    return (y.reshape(b, t, d),
            _token_major(kv4t, (4, KV_HEADS, HEAD_DIM)),
            _token_major(kvwt[:, :, t - wlen:], (2, KV_HEADS, HEAD_DIM)),
            u[:, t - POOL_STATE:],
            jnp.swapaxes(st[t // tm - 1::t // tm, :, 6:8], 1, 2).reshape(b, 2, 2 * D_FF),
            mem_kv.reshape(b, m, 2, X_HEADS, X_HEAD_DIM))


def _sample_layer(x, cache_kv, page_table, cache_win, state_pool, state_ffn, cache_mem, w, *, tf=1408):
    nb, _, d = x.shape
    n_phys, page = cache_kv.shape[:2]
    n_pages = page_table.shape[1]
    past = n_pages * page
    u, q, _, _, gt, kv4t, kvwt = _mix_in(x.reshape(1, nb, d), w['g_mix'], w['w_tok'], w['w_feat'], nb, "mix_in_sample")
    y_pool = _pool_sample(jnp.swapaxes(state_pool, 0, 1), u, w['w_pool'], w['pool_scale'], past)
    cache3 = _feature_major(cache_kv, 2 * KV_W)
    pt_flat = page_table.reshape(-1)
    jq = past // SEL_BLOCK
    ns_pad = -(-(jq + 1) // LANE) * LANE
    k_past = min(N_SELECT, jq + 1) - 1
    q3 = q.reshape(nb, 1, NSA_WIDTH)
    oc, imp = _nsa_sample_cmp(pt_flat, q3, cache3, w['wbig'], w['pe8'], w['wpe'], n_pages, ns_pad)
    idx = _topk_sample(imp[:, :KV_HEADS].reshape(nb * KV_HEADS, ns_pad), jq, k_past)
    idx_flat = idx[:, :k_past].reshape(-1)
    o8, win_out = _nsa_sample_sel(idx_flat, pt_flat, q3, gt.reshape(nb, 1, GATE_PAD), oc,
                                  _feature_major(cache_win, KV_W), kv4t, kvwt, cache3, n_pages, k_past)
    y_nsa = jnp.concatenate([o8[:, :GROUP, :HEAD_DIM].reshape(nb, GROUP * HEAD_DIM),
                             o8[:, GROUP:, HEAD_DIM:].reshape(nb, GROUP * HEAD_DIM)], axis=1)
    h1, qx = _mix_sample(x.reshape(nb, d), y_pool, y_nsa, w['w_out'], w['g_xattn'], w['w_xq'])
    m = cache_mem.shape[1]
    halves = X_HEAD_DIM // LANE
    mem5 = jnp.transpose(cache_mem.reshape(nb, m, 2, X_HEADS, halves, LANE), (0, 1, 2, 4, 3, 5))
    ox = _xattn_sample(qx.reshape(nb, 1, d), mem5.reshape(nb, m, 2, halves * X_HEADS, LANE))
    h2 = _resid_proj(h1, ox.reshape(nb, d), w['w_xo'])
    y, up2 = _ffn_sample(h2, w['g_ffn'], w['w_up'], w['cw3'], w['cb2'], w['w_down'], w['g_final'],
                         jnp.swapaxes(state_ffn, 0, 1), tf)
    up = jnp.swapaxes(up2, 0, 1).reshape(nb, 1, 2 * D_FF)
    return (y.reshape(nb, 1, d),
            jnp.swapaxes(_token_major(kv4t, (4, KV_HEADS, HEAD_DIM)), 0, 1),
            _token_major(win_out, (2, KV_HEADS, HEAD_DIM)),
            jnp.concatenate([state_pool[:, 1:], u[:, None, :]], axis=1),
            jnp.concatenate([state_ffn[:, 1:], up], axis=1),
            )


def kernel(x_prompt, x_sample, mem_prompt, cache_kv, page_table, cache_win, state_pool, state_ffn, cache_mem, g_mix, w_in, w_pool, pool_scale, w_cmp, pe_cmp, w_out, g_xattn, g_mem, w_xq, w_xkv, w_xo, g_ffn, w_up, conv_w, conv_b, w_down, g_final):
    assert g_mix.shape[0] == 1, "single layer"
    w = _prep_weights(g_mix[0], w_in[0], w_pool[0], pool_scale[0], w_cmp[0], pe_cmp[0], w_out[0], g_xattn[0],
                      g_mem[0], w_xq[0], w_xkv[0], w_xo[0], g_ffn[0], w_up[0], conv_w[0], conv_b[0], w_down[0],
                      g_final)
    yp, kv_p, win_p, pool_p, ffn_p, mem_p = _prompt_layer(x_prompt, mem_prompt, w)
    ys, kv_s, win_s, pool_s, ffn_s = _sample_layer(x_sample, cache_kv[0], page_table, cache_win[0],
                                                    state_pool[0], state_ffn[0], cache_mem[0], w)
    lead = lambda a: a[None]
    return (yp, ys, lead(kv_p), lead(kv_s), lead(win_p), lead(win_s), lead(pool_p), lead(pool_s),
            lead(ffn_p), lead(ffn_s), lead(mem_p))
```

```python
import functools

import jax
import jax.numpy as jnp
from jax import lax
from jax.experimental import pallas as pl
from jax.experimental.pallas import tpu as pltpu

F32 = jnp.float32
BF16 = jnp.bfloat16
I32 = jnp.int32

D_MODEL = 1024
POOL_WIDTH = 512
POOL_WINDOWS = (2, 4, 8, 16)
POOL_GC = 128
POOL_STATE = 15
NSA_WIDTH = 512
HEAD_DIM = 64
N_HEADS = 8
KV_HEADS = 2
GROUP = 4
N_BRANCH = 3
CMP_LEN = 32
CMP_STRIDE = 16
SEL_BLOCK = 64
N_SELECT = 16
WINDOW = 512
QBLK = 128
KV_W = 2 * KV_HEADS * HEAD_DIM
N_GATE = N_HEADS * N_BRANCH
LANE = 128
GATE_PAD = LANE
X_HEADS = 4
X_HEAD_DIM = 256
D_FF = 2816
EPS = 1e-6
NEG = -1e30
VMEM_LIMIT = 56 * 1024 * 1024
CHUNK_PITCH = 24
PAGE_UNROLL = 16
SEL_TILES = 2
RANK_STEP = 16


def _dot(a, b):
    return jnp.dot(a.astype(BF16), b.astype(BF16), preferred_element_type=F32)


def _dot_nt(a, b):
    return lax.dot_general(a.astype(BF16), b.astype(BF16), (((1,), (1,)), ((), ())),
                           preferred_element_type=F32)


def _rms(x, g):
    return x * lax.rsqrt(jnp.mean(x * x, axis=-1, keepdims=True) + EPS) * g


def _split3(x):
    p0 = x.astype(BF16)
    r = x - p0.astype(F32)
    p1 = r.astype(BF16)
    p2 = (r - p1.astype(F32)).astype(BF16)
    return p0, p1, p2


def _masked_softmax(s, mask):
    s = jnp.where(mask, s, -jnp.inf)
    m = jnp.max(s, axis=-1, keepdims=True)
    m = jnp.where(m == -jnp.inf, 0.0, m)
    e = jnp.exp(s - m)
    return e / jnp.maximum(jnp.sum(e, axis=-1, keepdims=True), 1e-30)


def _params(sem):
    return pltpu.CompilerParams(dimension_semantics=sem, vmem_limit_bytes=VMEM_LIMIT)


def _rms_proj_kernel(x_ref, g_ref, w_ref, *out_refs, widths, sigmoid_last):
    xn = _rms(x_ref[...], g_ref[...]).astype(BF16)
    off = 0
    for k, (o_ref, wd) in enumerate(zip(out_refs, widths)):
        y = jnp.dot(xn, w_ref[:, off:off + wd], preferred_element_type=F32)
        if sigmoid_last and k == len(widths) - 1:
            y = 1.0 / (1.0 + jnp.exp(-y))
        o_ref[...] = y
        off += wd


def _rms_proj(x, g, w, widths, tm, sigmoid_last=False, name="rms_proj"):
    n, d = x.shape
    ntot = w.shape[1]
    assert sum(widths) == ntot and n % tm == 0
    return pl.pallas_call(
        functools.partial(_rms_proj_kernel, widths=tuple(widths), sigmoid_last=sigmoid_last),
        grid=(n // tm,),
        in_specs=[pl.BlockSpec((tm, d), lambda i: (i, 0)),
                  pl.BlockSpec((1, d), lambda i: (0, 0)),
                  pl.BlockSpec((d, ntot), lambda i: (0, 0))],
        out_specs=[pl.BlockSpec((tm, wd), lambda i: (i, 0)) for wd in widths],
        out_shape=[jax.ShapeDtypeStruct((n, wd), F32) for wd in widths],
        compiler_params=_params(("arbitrary",)),
        name=name,
    )(x, g, w)


def _mix_in_kernel(x_ref, g_ref, wt_ref, wf_ref, u_ref, q_ref, kc_ref, ks_ref, gt_ref, kv4_ref, kvw_ref):
    xn = _rms(x_ref[...], g_ref[...]).astype(BF16)
    off = 0
    for o_ref in (u_ref, q_ref, kc_ref, ks_ref, gt_ref):
        wd = o_ref.shape[-1]
        y = jnp.dot(xn, wt_ref[:, off:off + wd], preferred_element_type=F32)
        o_ref[...] = 1.0 / (1.0 + jnp.exp(-y)) if o_ref is gt_ref else y
        off += wd
    kv_t = lax.dot_general(wf_ref[...], xn, (((1,), (1,)), ((), ())), preferred_element_type=F32)
    kv4_ref[0] = kv_t[0:2 * KV_W]
    kvw_ref[0] = kv_t[2 * KV_W:3 * KV_W]


def _mix_in(x3, g, w_tok, w_feat, tm, name):
    b, t, d = x3.shape
    n = b * t
    tps = t // tm
    widths = (POOL_WIDTH, NSA_WIDTH, KV_W, LANE, GATE_PAD)
    assert sum(widths) == w_tok.shape[1] and t % tm == 0
    return pl.pallas_call(
        _mix_in_kernel,
        grid=(n // tm,),
        in_specs=[pl.BlockSpec((tm, d), lambda i: (i, 0)),
                  pl.BlockSpec((1, d), lambda i: (0, 0)),
                  pl.BlockSpec(w_tok.shape, lambda i: (0, 0)),
                  pl.BlockSpec(w_feat.shape, lambda i: (0, 0))],
        out_specs=[pl.BlockSpec((tm, wd), lambda i: (i, 0)) for wd in widths]
        + [pl.BlockSpec((1, 2 * KV_W, tm), lambda i: (i // tps, 0, i % tps)),
           pl.BlockSpec((1, KV_W, tm), lambda i: (i // tps, 0, i % tps))],
        out_shape=[jax.ShapeDtypeStruct((n, wd), F32) for wd in widths]
        + [jax.ShapeDtypeStruct((b, 2 * KV_W, t), F32), jax.ShapeDtypeStruct((b, KV_W, t), F32)],
        compiler_params=_params(("arbitrary",)),
        name=name,
    )(x3.reshape(n, d), g, w_tok, w_feat)


def _pool_tail(s_list, u, cnts, w_ref, sc_ref, y_ref, lead):
    for gi in range(len(POOL_WINDOWS)):
        cols = slice(gi * POOL_GC, (gi + 1) * POOL_GC)
        d = s_list[gi] / cnts[gi] - u[:, cols]
        y = _dot(d, w_ref[gi]) * sc_ref[:, cols]
        if lead:
            y_ref[0, :, cols] = y
        else:
            y_ref[:, cols] = y


def _pool_prompt_kernel(u_ref, w_ref, sc_ref, y_ref, ext_ref, *, tm):
    t = pl.program_id(1)
    hist = 16

    @pl.when(t == 0)
    def _():
        ext_ref[0:hist, :] = jnp.zeros((hist, POOL_WIDTH), F32)

    @pl.when(t > 0)
    def _():
        ext_ref[0:hist, :] = ext_ref[tm:tm + hist, :]

    u = u_ref[0]
    ext_ref[hist:hist + tm, :] = u
    pos = t * tm + lax.broadcasted_iota(I32, (tm, 1), 0)
    s_list, cnts = [], []
    for gi, w in enumerate(POOL_WINDOWS):
        cols = slice(gi * POOL_GC, (gi + 1) * POOL_GC)
        s = u[:, cols]
        for k in range(1, w):
            s = s + ext_ref[hist - k:hist - k + tm, cols]
        s_list.append(s)
        cnts.append(jnp.minimum(pos + 1, w).astype(F32))
    _pool_tail(s_list, u, cnts, w_ref, sc_ref, y_ref, True)


def _pool_prompt(u3, w_pool, pool_scale, tm):
    b, t, _ = u3.shape
    return pl.pallas_call(
        functools.partial(_pool_prompt_kernel, tm=tm),
        grid=(b, t // tm),
        in_specs=[pl.BlockSpec((1, tm, POOL_WIDTH), lambda i, j: (i, j, 0)),
                  pl.BlockSpec((4, POOL_GC, POOL_GC), lambda i, j: (0, 0, 0)),
                  pl.BlockSpec((1, POOL_WIDTH), lambda i, j: (0, 0))],
        out_specs=pl.BlockSpec((1, tm, POOL_WIDTH), lambda i, j: (i, j, 0)),
        out_shape=jax.ShapeDtypeStruct((b, t, POOL_WIDTH), F32),
        scratch_shapes=[pltpu.VMEM((tm + 16, POOL_WIDTH), F32)],
        compiler_params=_params(("arbitrary", "arbitrary")),
        name="pool_prompt",
    )(u3, w_pool, pool_scale)


def _pool_sample_kernel(st_ref, u_ref, w_ref, sc_ref, y_ref, *, pos0):
    u = u_ref[...]
    s_list, cnts = [], []
    for gi, w in enumerate(POOL_WINDOWS):
        cols = slice(gi * POOL_GC, (gi + 1) * POOL_GC)
        s = u[:, cols]
        for k in range(1, w):
            s = s + st_ref[POOL_STATE - k, :, cols]
        s_list.append(s)
        cnts.append(float(min(pos0 + 1, w)))
    _pool_tail(s_list, u, cnts, w_ref, sc_ref, y_ref, False)


def _pool_sample(st_t, u, w_pool, pool_scale, pos0):
    n = u.shape[0]
    return pl.pallas_call(
        functools.partial(_pool_sample_kernel, pos0=pos0),
        grid=(1,),
        in_specs=[pl.BlockSpec((POOL_STATE, n, POOL_WIDTH), lambda i: (0, 0, 0)),
                  pl.BlockSpec((n, POOL_WIDTH), lambda i: (0, 0)),
                  pl.BlockSpec((4, POOL_GC, POOL_GC), lambda i: (0, 0, 0)),
                  pl.BlockSpec((1, POOL_WIDTH), lambda i: (0, 0))],
        out_specs=pl.BlockSpec((n, POOL_WIDTH), lambda i: (0, 0)),
        out_shape=jax.ShapeDtypeStruct((n, POOL_WIDTH), F32),
        compiler_params=_params(("arbitrary",)),
        name="pool_sample",
    )(st_t, u, w_pool, pool_scale)


def _compress_rows(load_half, wbig_ref, pe_ref, wpe_ref, n_chunk):
    pe_row = _dot(pe_ref[...], wpe_ref[...])[0:1]
    halves = []
    for c in range(2):
        x = jnp.concatenate([load_half(c, s).astype(BF16) for s in range(CMP_STRIDE)], axis=1)
        y = jnp.dot(x, wbig_ref[c], preferred_element_type=F32)
        halves.append(y[:, :LANE] + pltpu.roll(y[:, LANE:], n_chunk - 1, axis=0))
    return jnp.concatenate(halves, axis=1) + pe_row


def _compress_prompt_kernel(k_ref, v_ref, wbig_ref, pe_ref, wpe_ref, out_ref, *, n_chunk):
    load = lambda c, s: (k_ref, v_ref)[c][0, pl.ds(s, n_chunk, stride=CMP_STRIDE), :]
    out_ref[0] = _compress_rows(load, wbig_ref, pe_ref, wpe_ref, n_chunk)


def _compress_prompt(kv4, wbig, pe8, wpe):
    b, t, _ = kv4.shape
    n_chunk = t // CMP_STRIDE
    return pl.pallas_call(
        functools.partial(_compress_prompt_kernel, n_chunk=n_chunk),
        grid=(b,),
        in_specs=[pl.BlockSpec((1, t, LANE), lambda i: (i, 0, 0)),
                  pl.BlockSpec((1, t, LANE), lambda i: (i, 0, 1)),
                  pl.BlockSpec(wbig.shape, lambda i: (0, 0, 0)),
                  pl.BlockSpec(pe8.shape, lambda i: (0, 0)),
                  pl.BlockSpec(wpe.shape, lambda i: (0, 0))],
        out_specs=pl.BlockSpec((1, n_chunk, KV_W), lambda i: (i, 0, 0)),
        out_shape=jax.ShapeDtypeStruct((b, n_chunk, KV_W), F32),
        compiler_params=_params(("arbitrary",)),
        name="compress_prompt",
    )(kv4, kv4, wbig, pe8, wpe)


def _stack_group_queries(q, g):
    lane = lax.broadcasted_iota(I32, (q.shape[0], LANE), 1)
    parts = []
    for hh in range(GROUP):
        h = GROUP * g + hh
        ch = q[:, (h // 2) * LANE:(h // 2 + 1) * LANE]
        if h % 2 != g:
            ch = pltpu.roll(ch, HEAD_DIM, axis=1)
        parts.append(jnp.where((lane >> 6) == g, ch, 0.0))
    return jnp.concatenate(parts, axis=0).astype(BF16)


def _place_heads(o_g, weights, g, outs):
    lane = lax.broadcasted_iota(I32, (QBLK, LANE), 1)
    for hh in range(GROUP):
        h = GROUP * g + hh
        rs = slice(hh * QBLK, (hh + 1) * QBLK)
        piece = sum(w_[:, 0:1] * o_[rs] for w_, o_ in zip(weights(h), o_g))
        if h % 2 != g:
            piece = pltpu.roll(piece, HEAD_DIM, axis=1)
        piece = jnp.where((lane >> 6) == (h % 2), piece, 0.0)
        outs[h // 2] = piece if outs[h // 2] is None else outs[h // 2] + piece


def _nsa_dense_kernel(q_ref, gt_ref, kvc_ref, kw_ref, vw_ref, part_ref, sel_ref, *, t_len):
    i = pl.program_id(1)
    t0 = i * QBLK
    nc = t_len // CMP_STRIDE
    ns = t_len // SEL_BLOCK
    wl = min(WINDOW + QBLK, t_len)
    q = q_ref[0] * (HEAD_DIM ** -0.5)
    gt = gt_ref[0]
    pos_q = t0 + lax.broadcasted_iota(I32, (QBLK, 1), 0)

    oj = lax.broadcasted_iota(I32, (ns, nc), 0)
    on = lax.broadcasted_iota(I32, (ns, nc), 1)
    ovl_t = jnp.where((on * CMP_STRIDE + CMP_LEN - 1 >= oj * SEL_BLOCK)
                      & (on * CMP_STRIDE <= oj * SEL_BLOCK + SEL_BLOCK - 1), 1.0, 0.0).astype(BF16)
    jb = lax.broadcasted_iota(I32, (ns, QBLK), 0)
    jq = (t0 + lax.broadcasted_iota(I32, (ns, QBLK), 1)) >> 6
    forced = (jb == 0) | (jb == jq) | (jb == jq - 1)

    kc_all = kvc_ref[0, :, 0:LANE].astype(BF16)
    vc_all = kvc_ref[0, :, LANE:2 * LANE].astype(BF16)
    ws = pl.multiple_of(jnp.maximum(t0 - WINDOW, 0), QBLK)
    kwc = kw_ref[0, :, pl.ds(ws, wl)].astype(BF16)
    vwc = vw_ref[0, :, pl.ds(ws, wl)].astype(BF16)

    ncol = lax.broadcasted_iota(I32, (1, nc), 1)
    bias_c = jnp.where((ncol * CMP_STRIDE + CMP_LEN - 1) <= pos_q, 0.0, NEG)
    bias_c = jnp.concatenate([bias_c] * GROUP, axis=0)
    dpos = pos_q - (ws + lax.broadcasted_iota(I32, (1, wl), 1))
    bias_w = jnp.where((dpos >= 0) & (dpos <= WINDOW), 0.0, NEG)
    bias_w = jnp.concatenate([bias_w] * GROUP, axis=0)

    def softmax_parts(s):
        m = jnp.max(s, axis=-1, keepdims=True)
        e = jnp.exp(s - m)
        return e, jnp.where(m > 0.5 * NEG, 1.0 / jnp.sum(e, axis=-1, keepdims=True), 0.0)

    outs = [None] * (NSA_WIDTH // LANE)
    for g in range(KV_HEADS):
        qg = _stack_group_queries(q, g)

        e_c, inv_c = softmax_parts(_dot_nt(qg, kc_all) + bias_c)
        o_c = _dot(e_c, vc_all) * inv_c

        p_c = e_c * inv_c
        psum = p_c[0:QBLK] + p_c[QBLK:2 * QBLK] + p_c[2 * QBLK:3 * QBLK] + p_c[3 * QBLK:4 * QBLK]
        imp_t = sum(lax.dot_general(ovl_t, part, (((1,), (1,)), ((), ())), preferred_element_type=F32)
                    for part in _split3(psum))
        x = jnp.where(forced, jnp.inf, jnp.where(jb > jq, -jnp.inf, imp_t))
        sel_ref[0, 0, g * ns:(g + 1) * ns, :] = jnp.zeros((ns, QBLK), F32)
        for live in range(min(RANK_STEP, ns), ns + 1, RANK_STEP):
            lo = (live - RANK_STEP) // 2
            hi = live // 2 if live < ns else t_len // QBLK

            @pl.when((i >= lo) & (i < hi))
            def _(live=live):
                xs, js = x[0:live], jb[0:live]
                cnt = jnp.zeros((live, QBLK), I32)
                for jp in range(live):
                    row = xs[jp:jp + 1, :]
                    beats = (row > xs) | ((row == xs) & (js > jp))
                    cnt = cnt + jnp.where(beats, 1, 0)
                sel_ref[0, 0, g * ns:g * ns + live, :] = jnp.where(cnt < N_SELECT, 1.0, 0.0)

        e_w, inv_w = softmax_parts(_dot(qg, kwc) + bias_w)
        o_w = _dot_nt(e_w, vwc) * inv_w

        _place_heads((o_c, o_w), lambda h: (gt[:, 3 * h:3 * h + 1], gt[:, 3 * h + 2:3 * h + 3]), g, outs)
    part_ref[0] = jnp.concatenate(outs, axis=1)


def _nsa_select_kernel(q_ref, gt_ref, sel_ref, part_ref, ks_ref, vs_ref, o_ref, *, t_len, kc):
    i = pl.program_id(1)
    nqry = SEL_TILES * QBLK
    t0 = i * nqry
    ns = t_len // SEL_BLOCK
    rows = GROUP * nqry
    kcs = kc
    n_small = (t0 + nqry + kcs[-1] - 1) // kcs[-1]
    gt_t = gt_ref[0].T
    blocks = []
    q_t = (q_ref[0] * (HEAD_DIM ** -0.5)).T
    zero_half = jnp.zeros((HEAD_DIM, nqry), F32)
    for g in range(KV_HEADS):
        heads = []
        for hh in range(GROUP):
            q_h = q_t[(GROUP * g + hh) * HEAD_DIM:(GROUP * g + hh + 1) * HEAD_DIM]
            heads.append(jnp.concatenate([q_h, zero_half] if g == 0 else [zero_half, q_h], axis=0))
        qg_t = jnp.concatenate(heads, axis=1).astype(BF16)

        def sel_chunk(kcz, base, c, carry):
            m_prev, l_prev, acc = carry
            k0 = pl.multiple_of(base + c * kcz, kcs[-1])
            nblk = kcz // SEL_BLOCK
            s = _dot(ks_ref[0, pl.ds(k0, kcz), :], qg_t)
            j0 = pl.multiple_of(g * ns + k0 // SEL_BLOCK, kcs[-1] // SEL_BLOCK)
            chosen = []
            for tile in range(SEL_TILES):
                rows_j = sel_ref[0, tile, pl.ds(j0, nblk), :]
                chosen.append(jnp.concatenate([jnp.broadcast_to(rows_j[j:j + 1], (SEL_BLOCK, QBLK))
                                               for j in range(nblk)], axis=0))
            chosen = jnp.concatenate(chosen, axis=1)
            key_pos = k0 + lax.broadcasted_iota(I32, (kcz, nqry), 0)
            qry_pos = t0 + lax.broadcasted_iota(I32, (kcz, nqry), 1)
            bias = jnp.where((chosen > 0.5) & (key_pos <= qry_pos), 0.0, NEG)
            s = s + jnp.concatenate([bias] * GROUP, axis=1)
            m_new = jnp.maximum(m_prev, jnp.max(s, axis=0, keepdims=True))
            alpha = jnp.exp(m_prev - m_new)
            p = jnp.exp(s - m_new)
            l_new = alpha * l_prev + jnp.sum(p, axis=0, keepdims=True)
            return m_new, l_new, alpha * acc + _dot(vs_ref[0, :, pl.ds(k0, kcz)], p)

        carry = (jnp.full((1, rows), NEG, F32), jnp.zeros((1, rows), F32), jnp.zeros((LANE, rows), F32))
        done = 0
        for kcz in kcs:
            count = (n_small * kcs[-1] - done) // kcz
            carry = lax.fori_loop(0, count, functools.partial(sel_chunk, kcz, done), carry)
            done = done + count * kcz
        _, l_s, acc_s = carry
        o_t = acc_s[g * HEAD_DIM:(g + 1) * HEAD_DIM] / l_s
        for hh in range(GROUP):
            h = GROUP * g + hh
            blocks.append(o_t[:, hh * nqry:(hh + 1) * nqry] * gt_t[3 * h + 1:3 * h + 2])
    o_ref[0] = part_ref[0] + jnp.concatenate(blocks, axis=0).T


def _nsa_prompt(q3, gt3, kvc, ks_tok, kv4t, kvwt, kc):
    b, t, _ = q3.shape
    kc = tuple(k for k in kc if k <= t)
    assert all(t % k == 0 for k in kc) and t % QBLK == 0
    nq = t // QBLK
    ns = t // SEL_BLOCK
    slab = lambda c: pl.BlockSpec((1, LANE, t), lambda i, j: (i, c, 0))
    tile = lambda w_: pl.BlockSpec((1, QBLK, w_), lambda i, j: (i, j, 0))
    sel_spec = pl.BlockSpec((1, 1, KV_HEADS * ns, QBLK), lambda i, j: (i, j, 0, 0))
    part, sel = pl.pallas_call(
        functools.partial(_nsa_dense_kernel, t_len=t),
        grid=(b, nq),
        in_specs=[tile(NSA_WIDTH), tile(GATE_PAD),
                  pl.BlockSpec((1, t // CMP_STRIDE, KV_W), lambda i, j: (i, 0, 0)),
                  slab(0), slab(1)],
        out_specs=[tile(NSA_WIDTH), sel_spec],
        out_shape=[jax.ShapeDtypeStruct((b, t, NSA_WIDTH), F32),
                   jax.ShapeDtypeStruct((b, nq, KV_HEADS * ns, QBLK), F32)],
        compiler_params=_params(("arbitrary", "arbitrary")),
        name="nsa_dense",
    )(q3, gt3, kvc, kvwt, kvwt)
    assert nq % SEL_TILES == 0
    wide = lambda w_: pl.BlockSpec((1, SEL_TILES * QBLK, w_), lambda i, j: (i, j, 0))
    return pl.pallas_call(
        functools.partial(_nsa_select_kernel, t_len=t, kc=kc),
        grid=(b, nq // SEL_TILES),
        in_specs=[wide(NSA_WIDTH), wide(GATE_PAD),
                  pl.BlockSpec((1, SEL_TILES, KV_HEADS * ns, QBLK), lambda i, j: (i, j, 0, 0)), wide(NSA_WIDTH),
                  pl.BlockSpec((1, t, LANE), lambda i, j: (i, 0, 0)), slab(3)],
        out_specs=wide(NSA_WIDTH),
        out_shape=jax.ShapeDtypeStruct((b, t, NSA_WIDTH), F32),
        compiler_params=_params(("arbitrary", "arbitrary")),
        name="nsa_select",
    )(q3, gt3, sel, part, ks_tok, kv4t)


def _xattn_prompt_kernel(x_ref, yp_ref, yn_ref, wo_ref, g_ref, wq_ref, mem_ref, wxo_ref, o_ref):
    y = jnp.concatenate([yp_ref[0], yn_ref[0]], axis=1)
    h1 = x_ref[0] + _dot(y, wo_ref[...])
    qx = _dot(_rms(h1, g_ref[...]), wq_ref[...]) * (X_HEAD_DIM ** -0.5)
    xw = X_HEADS * X_HEAD_DIM
    outs = []
    for h in range(X_HEADS):
        cs = slice(h * X_HEAD_DIM, (h + 1) * X_HEAD_DIM)
        s = _dot_nt(qx[:, cs], mem_ref[0, :, cs])
        e = jnp.exp(s - jnp.max(s, axis=-1, keepdims=True))
        p = e / jnp.sum(e, axis=-1, keepdims=True)
        outs.append(_dot(p, mem_ref[0, :, xw + h * X_HEAD_DIM:xw + (h + 1) * X_HEAD_DIM]))
    o_ref[0] = h1 + _dot(jnp.concatenate(outs, axis=1), wxo_ref[...])


def _xattn_prompt(x3, yp, yn, w_out, g_x, w_xq, mem_kv, w_xo, tm):
    b, t, d = x3.shape
    m = mem_kv.shape[1]
    full = lambda a: pl.BlockSpec(a.shape, lambda i, j: (0,) * a.ndim)
    return pl.pallas_call(
        _xattn_prompt_kernel,
        grid=(b, t // tm),
        in_specs=[pl.BlockSpec((1, tm, d), lambda i, j: (i, j, 0)),
                  pl.BlockSpec((1, tm, POOL_WIDTH), lambda i, j: (i, j, 0)),
                  pl.BlockSpec((1, tm, NSA_WIDTH), lambda i, j: (i, j, 0)),
                  full(w_out), full(g_x), full(w_xq),
                  pl.BlockSpec((1, m, mem_kv.shape[2]), lambda i, j: (i, 0, 0)),
                  full(w_xo)],
        out_specs=pl.BlockSpec((1, tm, d), lambda i, j: (i, j, 0)),
        out_shape=jax.ShapeDtypeStruct((b, t, d), F32),
        compiler_params=_params(("arbitrary", "arbitrary")),
        name="xattn_prompt",
    )(x3, yp, yn, w_out, g_x, w_xq, mem_kv, w_xo)


def _ffn_core(j, nj, h_ref, gf_ref, wg_ref, wv_ref, cw_ref, cb_ref, wd_ref, gfin_ref, y_ref,
              xn_ref, acc_ref, prev_rows):
    @pl.when(j == 0)
    def _():
        xn_ref[...] = _rms(h_ref[...], gf_ref[...]).astype(BF16)
        acc_ref[...] = jnp.zeros(acc_ref.shape, F32)

    xn = xn_ref[...]
    up_g = jnp.dot(xn, wg_ref[...], preferred_element_type=F32)
    up_v = jnp.dot(xn, wv_ref[...], preferred_element_type=F32)
    g1, g2, v1, v2 = prev_rows(up_g, up_v)
    cg = cb_ref[0:1, :] + cw_ref[0, 0:1, :] * g2 + cw_ref[1, 0:1, :] * g1 + cw_ref[2, 0:1, :] * up_g
    cv = cb_ref[1:2, :] + cw_ref[0, 1:2, :] * v2 + cw_ref[1, 1:2, :] * v1 + cw_ref[2, 1:2, :] * up_v
    act = cg / (1.0 + jnp.exp(-cg)) * cv
    acc_ref[...] += _dot(act, wd_ref[...])

    @pl.when(j == nj - 1)
    def _():
        y_ref[...] = _rms(h_ref[...] + acc_ref[...], gfin_ref[...])

    return up_g, up_v


def _ffn_prompt_kernel(h_ref, gf_ref, wg_ref, wv_ref, cw_ref, cb_ref, wd_ref, gfin_ref,
                       y_ref, st_ref, xn_ref, acc_ref, carry_ref, *, tiles_per_seq, nj, tm):
    r = pl.program_id(0)
    j = pl.program_id(1)
    row = lax.broadcasted_iota(I32, (tm, 1), 0)

    @pl.when((r % tiles_per_seq) == 0)
    def _():
        carry_ref[j] = jnp.zeros(carry_ref.shape[1:], F32)

    def prev_rows(up_g, up_v):
        res = []
        for half, up in enumerate((up_g, up_v)):
            prev = carry_ref[j, half]
            m1 = jnp.where(row == 0, prev[7:8], pltpu.roll(up, 1, axis=0))
            m2 = jnp.where(row == 0, prev[6:7], jnp.where(row == 1, prev[7:8], pltpu.roll(up, 2, axis=0)))
            res += [m1, m2]
        return res

    up_g, up_v = _ffn_core(j, nj, h_ref, gf_ref, wg_ref, wv_ref, cw_ref, cb_ref, wd_ref, gfin_ref,
                           y_ref, xn_ref, acc_ref, prev_rows)
    for half, up in enumerate((up_g, up_v)):
        carry_ref[j, half] = up[tm - 8:tm]
        st_ref[0, half] = up[tm - 8:tm]


def _ffn_specs(d, tf, nj):
    return [pl.BlockSpec((1, d), lambda r, j: (0, 0)),
            pl.BlockSpec((d, tf), lambda r, j: (0, j)),
            pl.BlockSpec((d, tf), lambda r, j: (0, nj + j)),
            pl.BlockSpec((3, 2, tf), lambda r, j: (0, 0, j)),
            pl.BlockSpec((2, tf), lambda r, j: (0, j)),
            pl.BlockSpec((tf, d), lambda r, j: (j, 0)),
            pl.BlockSpec((1, d), lambda r, j: (0, 0))]


def _ffn_prompt(h2, g_ffn, w_up, cw3, cb2, w_down, g_final, batch, tm, tf):
    n, d = h2.shape
    nj = D_FF // tf
    tps = n // batch // tm
    return pl.pallas_call(
        functools.partial(_ffn_prompt_kernel, tiles_per_seq=tps, nj=nj, tm=tm),
        grid=(n // tm, nj),
        in_specs=[pl.BlockSpec((tm, d), lambda r, j: (r, 0))] + _ffn_specs(d, tf, nj),
        out_specs=[pl.BlockSpec((tm, d), lambda r, j: (r, 0)),
                   pl.BlockSpec((1, 2, 8, tf), lambda r, j: (r, 0, 0, j))],
        out_shape=[jax.ShapeDtypeStruct((n, d), F32),
                   jax.ShapeDtypeStruct((n // tm, 2, 8, D_FF), F32)],
        scratch_shapes=[pltpu.VMEM((tm, d), BF16), pltpu.VMEM((tm, d), F32),
                        pltpu.VMEM((nj, 2, 8, tf), F32)],
        compiler_params=_params(("arbitrary", "arbitrary")),
        name="ffn_prompt",
    )(h2, g_ffn, w_up, w_up, cw3, cb2, w_down, g_final)


def _ffn_sample_kernel(h_ref, gf_ref, wg_ref, wv_ref, cw_ref, cb_ref, wd_ref, gfin_ref, sg_ref, sv_ref,
                       y_ref, up_ref, xn_ref, acc_ref, *, nj):
    j = pl.program_id(1)
    prev_rows = lambda up_g, up_v: (sg_ref[1], sg_ref[0], sv_ref[1], sv_ref[0])
    up_g, up_v = _ffn_core(j, nj, h_ref, gf_ref, wg_ref, wv_ref, cw_ref, cb_ref, wd_ref, gfin_ref,
                           y_ref, xn_ref, acc_ref, prev_rows)
    up_ref[0] = up_g
    up_ref[1] = up_v


def _ffn_sample(h2, g_ffn, w_up, cw3, cb2, w_down, g_final, st_t, tf):
    n, d = h2.shape
    nj = D_FF // tf
    return pl.pallas_call(
        functools.partial(_ffn_sample_kernel, nj=nj),
        grid=(1, nj),
        in_specs=[pl.BlockSpec((n, d), lambda r, j: (0, 0))] + _ffn_specs(d, tf, nj)
        + [pl.BlockSpec((2, n, tf), lambda r, j: (0, 0, j)),
           pl.BlockSpec((2, n, tf), lambda r, j: (0, 0, nj + j))],
        out_specs=[pl.BlockSpec((n, d), lambda r, j: (0, 0)),
                   pl.BlockSpec((2, n, tf), lambda r, j: (0, 0, j))],
        out_shape=[jax.ShapeDtypeStruct((n, d), F32),
                   jax.ShapeDtypeStruct((2, n, D_FF), F32)],
        scratch_shapes=[pltpu.VMEM((n, d), BF16), pltpu.VMEM((n, d), F32)],
        compiler_params=_params(("arbitrary", "arbitrary")),
        name="ffn_sample",
    )(h2, g_ffn, w_up, w_up, cw3, cb2, w_down, g_final, st_t, st_t)


def _head_rows(qrow):
    rowi = lax.broadcasted_iota(I32, (N_HEADS, LANE), 0)
    lane = lax.broadcasted_iota(I32, (N_HEADS, LANE), 1)
    x = jnp.zeros((N_HEADS, LANE), F32)
    for c in range(NSA_WIDTH // LANE):
        x = jnp.where((rowi >> 1) == c, jnp.broadcast_to(qrow[:, c * LANE:(c + 1) * LANE], (N_HEADS, LANE)), x)
    x = jnp.where((rowi & 1) != (rowi >> 2), pltpu.roll(x, HEAD_DIM, axis=1), x)
    return jnp.where((lane >> 6) == (rowi >> 2), x, 0.0)


def _page_copy(cache_ref, raw_ref, sem_ref, phys, p, slot):
    return pltpu.make_async_copy(cache_ref.at[phys, pl.ds(0, KV_W), :], raw_ref.at[slot, p], sem_ref.at[slot, p])


def _nsa_sample_cmp_kernel(pt_ref, q_ref, cache_ref, wbig_ref, pe_ref, wpe_ref, oc_ref, imp_ref,
                           raw_ref, tok_ref, sem_ref, *, n_pages, n_batch, page):
    b = pl.program_id(0)
    slot = b % 2
    n_tok = n_pages * page
    n_chunk = n_tok // CMP_STRIDE

    def start_all(bb, sl):
        def body(p, _):
            _page_copy(cache_ref, raw_ref, sem_ref, pt_ref[bb * n_pages + p], p, sl).start()
            return 0
        lax.fori_loop(0, n_pages, body, 0)

    @pl.when(b == 0)
    def _():
        start_all(0, 0)

    @pl.when(b + 1 < n_batch)
    def _():
        start_all(b + 1, 1 - slot)

    chunks_per_page = page // CMP_STRIDE

    def to_token_major(pg, _):
        for pp in range(PAGE_UNROLL):
            _page_copy(cache_ref, raw_ref, sem_ref, 0, pg * PAGE_UNROLL + pp, slot).wait()
        for pp in range(PAGE_UNROLL):
            p = pg * PAGE_UNROLL + pp
            xt = raw_ref[slot, p]
            for half in range(2):
                x = xt[half * LANE:(half + 1) * LANE].T
                for nl in range(chunks_per_page):
                    r0 = pl.multiple_of((p * chunks_per_page + nl) * CHUNK_PITCH, 8)
                    tok_ref[half, pl.ds(r0, CMP_STRIDE), :] = x[nl * CMP_STRIDE:(nl + 1) * CMP_STRIDE]
        return 0
    lax.fori_loop(0, n_pages // PAGE_UNROLL, to_token_major, 0)

    load = lambda c, s: tok_ref[c, pl.ds(s, n_chunk, stride=CHUNK_PITCH), :]
    kvc = _compress_rows(load, wbig_ref, pe_ref, wpe_ref, n_chunk)
    q8 = _head_rows(q_ref[0] * (HEAD_DIM ** -0.5))
    s = _dot_nt(q8, kvc[:, 0:LANE])
    ncol = lax.broadcasted_iota(I32, (1, n_chunk), 1)
    p = _masked_softmax(s, (ncol * CMP_STRIDE + CMP_LEN - 1) <= n_tok)
    oc_ref[0] = _dot(p, kvc[:, LANE:2 * LANE])
    rowi = lax.broadcasted_iota(I32, (N_HEADS, n_chunk), 0)
    psum = jnp.where(rowi == 0, jnp.sum(p[0:GROUP], axis=0, keepdims=True),
                     jnp.where(rowi == 1, jnp.sum(p[GROUP:2 * GROUP], axis=0, keepdims=True), 0.0))
    ns_pad = imp_ref.shape[2]
    on = lax.broadcasted_iota(I32, (n_chunk, ns_pad), 0)
    oj = lax.broadcasted_iota(I32, (n_chunk, ns_pad), 1)
    ovl = jnp.where((on * CMP_STRIDE + CMP_LEN - 1 >= oj * SEL_BLOCK)
                    & (on * CMP_STRIDE <= oj * SEL_BLOCK + SEL_BLOCK - 1), 1.0, 0.0).astype(BF16)
    imp_ref[0] = sum(jnp.dot(part, ovl, preferred_element_type=F32) for part in _split3(psum))


def _nsa_sample_cmp(pt_flat, q3, cache3, wbig, pe8, wpe, n_pages, ns_pad):
    nb = q3.shape[0]
    page = cache3.shape[2]
    assert page == LANE
    n_tok = n_pages * page
    gs = pltpu.PrefetchScalarGridSpec(
        num_scalar_prefetch=1, grid=(nb,),
        in_specs=[pl.BlockSpec((1, 1, NSA_WIDTH), lambda i, pt: (i, 0, 0)),
                  pl.BlockSpec(memory_space=pl.ANY),
                  pl.BlockSpec(wbig.shape, lambda i, pt: (0, 0, 0)),
                  pl.BlockSpec(pe8.shape, lambda i, pt: (0, 0)),
                  pl.BlockSpec(wpe.shape, lambda i, pt: (0, 0))],
        out_specs=[pl.BlockSpec((1, N_HEADS, LANE), lambda i, pt: (i, 0, 0)),
                   pl.BlockSpec((1, N_HEADS, ns_pad), lambda i, pt: (i, 0, 0))],
        scratch_shapes=[pltpu.VMEM((2, n_pages, KV_W, page), F32), pltpu.VMEM((2, n_tok // CMP_STRIDE * CHUNK_PITCH, LANE), F32),
                        pltpu.SemaphoreType.DMA((2, n_pages))])
    return pl.pallas_call(
        functools.partial(_nsa_sample_cmp_kernel, n_pages=n_pages, n_batch=nb, page=page),
        grid_spec=gs,
        out_shape=[jax.ShapeDtypeStruct((nb, N_HEADS, LANE), F32),
                   jax.ShapeDtypeStruct((nb, N_HEADS, ns_pad), F32)],
        compiler_params=_params(("arbitrary",)),
        name="nsa_sample_cmp",
    )(pt_flat, q3, cache3, wbig, pe8, wpe)


def _topk_sample_kernel(imp_ref, idx_ref, *, jq, k_past):
    x = imp_ref[...]
    col = lax.broadcasted_iota(I32, x.shape, 1)
    x = jnp.where((col == 0) | (col == jq - 1), jnp.inf, jnp.where(col >= jq, -jnp.inf, x))
    lane = lax.broadcasted_iota(I32, idx_ref.shape, 1)
    out = jnp.zeros(idx_ref.shape, I32)
    big = jnp.int32(1 << 30)
    for k in range(k_past):
        m = jnp.max(x, axis=-1, keepdims=True)
        idx = jnp.min(jnp.where(x == m, col, big), axis=-1, keepdims=True)
        out = jnp.where(lane == k, idx, out)
        x = jnp.where(col == idx, -jnp.inf, x)
    idx_ref[...] = out


def _topk_sample(imp2, jq, k_past):
    n, w = imp2.shape
    return pl.pallas_call(
        functools.partial(_topk_sample_kernel, jq=jq, k_past=k_past),
        grid=(1,),
        in_specs=[pl.BlockSpec((n, w), lambda i: (0, 0))],
        out_specs=pl.BlockSpec((n, LANE), lambda i: (0, 0)),
        out_shape=jax.ShapeDtypeStruct((n, LANE), I32),
        compiler_params=_params(("arbitrary",)),
        name="topk_sample",
    )(imp2)


def _blk_copy(cache_ref, buf_ref, sem_ref, phys, kv, g, r, slot):
    row0 = KV_W + kv * LANE + g * HEAD_DIM
    return pltpu.make_async_copy(cache_ref.at[phys, pl.ds(row0, HEAD_DIM), :],
                                 buf_ref.at[slot, kv, pl.ds(g * HEAD_DIM, HEAD_DIM), pl.ds(r * LANE, LANE)],
                                 sem_ref.at[slot])


def _nsa_sample_sel_kernel(idx_ref, pt_ref, q_ref, gt_ref, oc_ref, win_ref, kvn4_ref, kvnw_ref, cache_ref,
                           o_ref, wout_ref, buf_ref, sem_ref, *, n_pages, n_batch, page, k_past):
    b = pl.program_id(0)
    slot = b % 2
    per_page = page // SEL_BLOCK
    nl = k_past * page

    def for_blocks(bb, fn):
        def body(r, _):
            for g in range(KV_HEADS):
                j = idx_ref[(bb * KV_HEADS + g) * k_past + r]
                phys = pt_ref[bb * n_pages + j // per_page]
                for kv in range(2):
                    fn(phys, kv, g, r)
            return 0
        lax.fori_loop(0, k_past, body, 0)

    def start_all(bb, sl):
        for_blocks(bb, lambda phys, kv, g, r: _blk_copy(cache_ref, buf_ref, sem_ref, phys, kv, g, r, sl).start())

    @pl.when(b == 0)
    def _():
        start_all(0, 0)

    @pl.when(b + 1 < n_batch)
    def _():
        start_all(b + 1, 1 - slot)

    def wait_body(r, _):
        for g in range(KV_HEADS):
            for kv in range(2):
                _blk_copy(cache_ref, buf_ref, sem_ref, 0, kv, g, r, slot).wait()
        return 0
    lax.fori_loop(0, k_past, wait_body, 0)

    q8 = _head_rows(q_ref[0] * (HEAD_DIM ** -0.5))
    rowi = lax.broadcasted_iota(I32, (N_HEADS, 1), 0)
    lane = lax.broadcasted_iota(I32, (N_HEADS, LANE), 1)
    own = (lane >> 6) == (rowi >> 2)
    mine = jnp.where(lax.broadcasted_iota(I32, (N_HEADS, n_batch), 1) == b, 0.0, -jnp.inf)

    def attend(k_t, v_t, bias):
        s = _dot(q8, k_t) + bias
        e = jnp.exp(s - jnp.max(s, axis=-1, keepdims=True))
        return _dot_nt(e, v_t) / jnp.sum(e, axis=-1, keepdims=True)

    lane_s = lax.broadcasted_iota(I32, (1, nl), 1)
    halves = []
    for g in range(KV_HEADS):
        hv = jnp.zeros((1, nl), I32)
        for r in range(k_past):
            hv = jnp.where((lane_s // page) == r, idx_ref[(b * KV_HEADS + g) * k_past + r] % per_page, hv)
        halves.append(hv)
    vis = ((lane_s % page) // SEL_BLOCK) == jnp.where(rowi < GROUP, halves[0], halves[1])
    o_s = attend(jnp.concatenate([buf_ref[slot, 0], kvn4_ref[0, KV_W:KV_W + LANE]], axis=1),
                 jnp.concatenate([buf_ref[slot, 1], kvn4_ref[0, KV_W + LANE:2 * KV_W]], axis=1),
                 jnp.concatenate([jnp.where(vis, 0.0, -jnp.inf), mine], axis=1))

    cw = win_ref[0]
    wlen = cw.shape[1]
    o_w = attend(jnp.concatenate([cw[0:LANE], kvnw_ref[0, 0:LANE]], axis=1),
                 jnp.concatenate([cw[LANE:2 * LANE], kvnw_ref[0, LANE:2 * LANE]], axis=1),
                 jnp.concatenate([jnp.zeros((N_HEADS, wlen), F32), mine], axis=1))
    lane_n = lax.broadcasted_iota(I32, (KV_W, n_batch), 1)
    new_col = jnp.sum(jnp.where(lane_n == b, kvnw_ref[0], 0.0), axis=1, keepdims=True)
    lane_w = lax.broadcasted_iota(I32, (KV_W, wlen), 1)
    wout_ref[0] = jnp.where(lane_w == wlen - 1, new_col, pltpu.roll(cw, wlen - 1, axis=1))

    gt = jnp.broadcast_to(gt_ref[0], (N_HEADS, GATE_PAD))
    gate = lambda br: jnp.sum(jnp.where(lane == rowi * N_BRANCH + br, gt, 0.0), axis=-1, keepdims=True)
    o = gate(0) * oc_ref[0] + gate(1) * o_s + gate(2) * o_w
    o_ref[0] = jnp.where(own, o, 0.0)


def _nsa_sample_sel(idx_flat, pt_flat, q3, gt3, oc, win_t, kvn4_t, kvnw_t, cache3, n_pages, k_past):
    nb = q3.shape[0]
    page = cache3.shape[2]
    wlen = win_t.shape[2]
    assert page == LANE
    per = lambda *blk: pl.BlockSpec((1,) + blk, lambda i, a, c: (i, 0, 0))
    whole = lambda arr: pl.BlockSpec(arr.shape, lambda i, a, c: (0, 0, 0))
    gs = pltpu.PrefetchScalarGridSpec(
        num_scalar_prefetch=2, grid=(nb,),
        in_specs=[per(1, NSA_WIDTH), per(1, GATE_PAD), per(N_HEADS, LANE), per(KV_W, wlen),
                  whole(kvn4_t), whole(kvnw_t), pl.BlockSpec(memory_space=pl.ANY)],
        out_specs=[per(N_HEADS, LANE), per(KV_W, wlen)],
        scratch_shapes=[pltpu.VMEM((2, 2, LANE, k_past * page), F32), pltpu.SemaphoreType.DMA((2,))])
    return pl.pallas_call(
        functools.partial(_nsa_sample_sel_kernel, n_pages=n_pages, n_batch=nb, page=page, k_past=k_past),
        grid_spec=gs,
        out_shape=[jax.ShapeDtypeStruct((nb, N_HEADS, LANE), F32),
                   jax.ShapeDtypeStruct((nb, KV_W, wlen), F32)],
        compiler_params=_params(("arbitrary",)),
        name="nsa_sample_sel",
    )(idx_flat, pt_flat, q3, gt3, oc, win_t, kvn4_t, kvnw_t, cache3)


def _mix_sample_kernel(x_ref, yp_ref, yn_ref, wo_ref, g_ref, wq_ref, h1_ref, qx_ref):
    y = jnp.concatenate([yp_ref[...], yn_ref[...]], axis=1)
    h1 = x_ref[...] + _dot(y, wo_ref[...])
    h1_ref[...] = h1
    qx_ref[...] = _dot(_rms(h1, g_ref[...]), wq_ref[...]) * (X_HEAD_DIM ** -0.5)


def _mix_sample(x, yp, yn, w_out, g_x, w_xq):
    n, d = x.shape
    full = lambda a: pl.BlockSpec(a.shape, lambda i: (0,) * a.ndim)
    args = (x, yp, yn, w_out, g_x, w_xq)
    return pl.pallas_call(
        _mix_sample_kernel, grid=(1,),
        in_specs=[full(a) for a in args],
        out_specs=[pl.BlockSpec((n, d), lambda i: (0, 0))] * 2,
        out_shape=[jax.ShapeDtypeStruct((n, d), F32)] * 2,
        compiler_params=_params(("arbitrary",)),
        name="mix_sample",
    )(*args)


def _xattn_sample_kernel(qx_ref, mem_ref, o_ref):
    n_mem = mem_ref.shape[1]
    n = n_mem * 8
    q = qx_ref[0]
    order = [(half, h) for half in range(X_HEAD_DIM // LANE) for h in range(X_HEADS)]
    col = lambda half, h: slice(h * X_HEAD_DIM + half * LANE, h * X_HEAD_DIM + (half + 1) * LANE)
    qt = jnp.concatenate([q[:, col(half, h)] for half, h in order], axis=0)
    x = mem_ref[0]
    a = _dot_nt(qt, x[:, 0].reshape(n, LANE))
    rowi = lax.broadcasted_iota(I32, (8, n), 0)
    lane = lax.broadcasted_iota(I32, (8, n), 1)
    own = (lane & 7) == rowi
    a = jnp.where(own, a, 0.0)
    other = pltpu.roll(a, X_HEADS, axis=0)
    s = a + jnp.where(rowi < X_HEADS, pltpu.roll(other, n - X_HEADS, axis=1), pltpu.roll(other, X_HEADS, axis=1))
    s = jnp.where(own, s, -jnp.inf)
    e = jnp.exp(s - jnp.max(s, axis=-1, keepdims=True))
    p = e / jnp.sum(e, axis=-1, keepdims=True)
    o8 = _dot(p, x[:, 1].reshape(n, LANE))
    o_ref[0] = jnp.concatenate([o8[half * X_HEADS + h:half * X_HEADS + h + 1]
                                for h in range(X_HEADS) for half in range(X_HEAD_DIM // LANE)], axis=1)


def _xattn_sample(qx3, mem5):
    nb, _, xw = qx3.shape
    return pl.pallas_call(
        _xattn_sample_kernel, grid=(nb,),
        in_specs=[pl.BlockSpec((1, 1, xw), lambda i: (i, 0, 0)),
                  pl.BlockSpec((1,) + mem5.shape[1:], lambda i: (i, 0, 0, 0, 0))],
        out_specs=pl.BlockSpec((1, 1, xw), lambda i: (i, 0, 0)),
        out_shape=jax.ShapeDtypeStruct((nb, 1, xw), F32),
        compiler_params=_params(("arbitrary",)),
        name="xattn_sample",
    )(qx3, mem5)


def _resid_proj_kernel(h_ref, o_ref, w_ref, y_ref):
    y_ref[...] = h_ref[...] + _dot(o_ref[...], w_ref[...])


def _resid_proj(h, o, w):
    n, d = h.shape
    full = lambda a: pl.BlockSpec(a.shape, lambda i: (0,) * a.ndim)
    return pl.pallas_call(
        _resid_proj_kernel, grid=(1,),
        in_specs=[full(h), full(o), full(w)],
        out_specs=pl.BlockSpec((n, d), lambda i: (0, 0)),
        out_shape=jax.ShapeDtypeStruct((n, d), F32),
        compiler_params=_params(("arbitrary",)),
        name="resid_proj",
    )(h, o, w)


def _prep_weights(g_mix, w_in, w_pool, pool_scale, w_cmp, pe_cmp, w_out, g_xattn, g_mem, w_xq, w_xkv,
                  w_xo, g_ffn, w_up, conv_w, conv_b, w_down, g_final):
    o_kv = POOL_WIDTH + NSA_WIDTH
    o_gate = o_kv + N_BRANCH * KV_W
    w_tok = jnp.concatenate([w_in[:, :o_kv + KV_W + LANE], jnp.pad(w_in[:, o_gate:], ((0, 0), (0, GATE_PAD - N_GATE)))],
                            axis=1)
    r = CMP_LEN // CMP_STRIDE
    w5 = w_cmp.reshape(2, r, CMP_STRIDE, HEAD_DIM, HEAD_DIM)
    eye = jnp.eye(2, dtype=F32)
    wbig = jnp.einsum('cisde,gG->csgdiGe', w5, eye).reshape(2, CMP_STRIDE * LANE, r * LANE)
    wpe = jnp.einsum('cke,cC->ckCe', w_cmp.reshape(2, CMP_LEN * HEAD_DIM, HEAD_DIM), eye)
    wpe = jnp.broadcast_to(wpe[:, :, :, None, :], (2, CMP_LEN * HEAD_DIM, 2, KV_HEADS, HEAD_DIM))
    row = lambda v: v.reshape(1, -1)
    return dict(
        g_mix=row(g_mix), w_tok=w_tok.astype(BF16), w_feat=w_in[:, o_kv:o_gate].T.astype(BF16),
        w_pool=w_pool.astype(BF16), pool_scale=row(pool_scale),
        wbig=wbig.astype(BF16), wpe=wpe.reshape(2 * CMP_LEN * HEAD_DIM, KV_W).astype(BF16),
        pe8=jnp.broadcast_to(pe_cmp.reshape(1, -1), (8, 2 * CMP_LEN * HEAD_DIM)),
        w_out=w_out.astype(BF16), g_xattn=row(g_xattn), g_mem=row(g_mem), w_xq=w_xq.astype(BF16),
        w_xkv=w_xkv.astype(BF16), w_xo=w_xo.astype(BF16), g_ffn=row(g_ffn), w_up=w_up.astype(BF16),
        cw3=conv_w.reshape(3, 2, D_FF), cb2=conv_b.reshape(2, D_FF), w_down=w_down.astype(BF16),
        g_final=row(g_final))


def _token_major(a_t, lead):
    b, _, t = a_t.shape
    nd = len(lead)
    return jnp.transpose(a_t.reshape((b,) + lead + (t,)), (0, nd + 1) + tuple(range(1, nd + 1)))


def _feature_major(a, nfeat):
    nd = a.ndim
    return jnp.transpose(a, (0,) + tuple(range(2, nd)) + (1,)).reshape(a.shape[0], nfeat, a.shape[1])


def _prompt_layer(x, mem, w, *, tm=512, tm_x=1024, tf=D_FF, kc=(2048, 1024, 512)):
    b, t, d = x.shape
    n = b * t
    u, q, kvc_tok, ks_tok, gt, kv4t, kvwt = _mix_in(x, w['g_mix'], w['w_tok'], w['w_feat'], tm, "mix_in_prompt")
    r3 = lambda a: a.reshape(b, t, a.shape[-1])
    u, q, kvc_tok, ks_tok, gt = map(r3, (u, q, kvc_tok, ks_tok, gt))
    y_pool = _pool_prompt(u, w['w_pool'], w['pool_scale'], tm)
    kvc = _compress_prompt(kvc_tok, w['wbig'], w['pe8'], w['wpe'])
    y_nsa = _nsa_prompt(q, gt, kvc, ks_tok, kv4t, kvwt, kc)
    m = mem.shape[1]
    (mem_kv,) = _rms_proj(mem.reshape(b * m, d), w['g_mem'], w['w_xkv'], (2 * X_HEADS * X_HEAD_DIM,), m,
                          name="mem_project")
    mem_kv = mem_kv.reshape(b, m, 2 * X_HEADS * X_HEAD_DIM)
    h2 = _xattn_prompt(x, y_pool, y_nsa, w['w_out'], w['g_xattn'], w['w_xq'], mem_kv, w['w_xo'], tm_x)
    y, st = _ffn_prompt(h2.reshape(n, d), w['g_ffn'], w['w_up'], w['cw3'], w['cb2'], w['w_down'],
                        w['g_final'], b, tm, tf)
    wlen = min(WINDOW, t)
    return (y.reshape(b, t, d),
            _token_major(kv4t, (4, KV_HEADS, HEAD_DIM)),
            _token_major(kvwt[:, :, t - wlen:], (2, KV_HEADS, HEAD_DIM)),
            u[:, t - POOL_STATE:],
            jnp.swapaxes(st[t // tm - 1::t // tm, :, 6:8], 1, 2).reshape(b, 2, 2 * D_FF),
            mem_kv.reshape(b, m, 2, X_HEADS, X_HEAD_DIM))


def _sample_layer(x, cache_kv, page_table, cache_win, state_pool, state_ffn, cache_mem, w, *, tf=1408):
    nb, _, d = x.shape
    n_phys, page = cache_kv.shape[:2]
    n_pages = page_table.shape[1]
    past = n_pages * page
    u, q, _, _, gt, kv4t, kvwt = _mix_in(x.reshape(1, nb, d), w['g_mix'], w['w_tok'], w['w_feat'], nb, "mix_in_sample")
    y_pool = _pool_sample(jnp.swapaxes(state_pool, 0, 1), u, w['w_pool'], w['pool_scale'], past)
    cache3 = _feature_major(cache_kv, 2 * KV_W)
    pt_flat = page_table.reshape(-1)
    jq = past // SEL_BLOCK
    ns_pad = -(-(jq + 1) // LANE) * LANE
    k_past = min(N_SELECT, jq + 1) - 1
    q3 = q.reshape(nb, 1, NSA_WIDTH)
    oc, imp = _nsa_sample_cmp(pt_flat, q3, cache3, w['wbig'], w['pe8'], w['wpe'], n_pages, ns_pad)
    idx = _topk_sample(imp[:, :KV_HEADS].reshape(nb * KV_HEADS, ns_pad), jq, k_past)
    idx_flat = idx[:, :k_past].reshape(-1)
    o8, win_out = _nsa_sample_sel(idx_flat, pt_flat, q3, gt.reshape(nb, 1, GATE_PAD), oc,
                                  _feature_major(cache_win, KV_W), kv4t, kvwt, cache3, n_pages, k_past)
    y_nsa = jnp.concatenate([o8[:, :GROUP, :HEAD_DIM].reshape(nb, GROUP * HEAD_DIM),
                             o8[:, GROUP:, HEAD_DIM:].reshape(nb, GROUP * HEAD_DIM)], axis=1)
    h1, qx = _mix_sample(x.reshape(nb, d), y_pool, y_nsa, w['w_out'], w['g_xattn'], w['w_xq'])
    m = cache_mem.shape[1]
    halves = X_HEAD_DIM // LANE
    mem5 = jnp.transpose(cache_mem.reshape(nb, m, 2, X_HEADS, halves, LANE), (0, 1, 2, 4, 3, 5))
    ox = _xattn_sample(qx.reshape(nb, 1, d), mem5.reshape(nb, m, 2, halves * X_HEADS, LANE))
    h2 = _resid_proj(h1, ox.reshape(nb, d), w['w_xo'])
    y, up2 = _ffn_sample(h2, w['g_ffn'], w['w_up'], w['cw3'], w['cb2'], w['w_down'], w['g_final'],
                         jnp.swapaxes(state_ffn, 0, 1), tf)
    up = jnp.swapaxes(up2, 0, 1).reshape(nb, 1, 2 * D_FF)
    return (y.reshape(nb, 1, d),
            jnp.swapaxes(_token_major(kv4t, (4, KV_HEADS, HEAD_DIM)), 0, 1),
            _token_major(win_out, (2, KV_HEADS, HEAD_DIM)),
            jnp.concatenate([state_pool[:, 1:], u[:, None, :]], axis=1),
            jnp.concatenate([state_ffn[:, 1:], up], axis=1),
            )


def kernel(x_prompt, x_sample, mem_prompt, cache_kv, page_table, cache_win, state_pool, state_ffn, cache_mem, g_mix, w_in, w_pool, pool_scale, w_cmp, pe_cmp, w_out, g_xattn, g_mem, w_xq, w_xkv, w_xo, g_ffn, w_up, conv_w, conv_b, w_down, g_final):
    assert g_mix.shape[0] == 1, "single layer"
    w = _prep_weights(g_mix[0], w_in[0], w_pool[0], pool_scale[0], w_cmp[0], pe_cmp[0], w_out[0], g_xattn[0],
                      g_mem[0], w_xq[0], w_xkv[0], w_xo[0], g_ffn[0], w_up[0], conv_w[0], conv_b[0], w_down[0],
                      g_final)
    yp, kv_p, win_p, pool_p, ffn_p, mem_p = _prompt_layer(x_prompt, mem_prompt, w)
    ys, kv_s, win_s, pool_s, ffn_s = _sample_layer(x_sample, cache_kv[0], page_table, cache_win[0],
                                                    state_pool[0], state_ffn[0], cache_mem[0], w)
    lead = lambda a: a[None]
    return (yp, ys, lead(kv_p), lead(kv_s), lead(win_p), lead(win_s), lead(pool_p), lead(pool_s),
            lead(ffn_p), lead(ffn_s), lead(mem_p))
```

```python
import functools

import jax
import jax.numpy as jnp
from jax import lax
from jax.experimental import pallas as pl
from jax.experimental.pallas import tpu as pltpu

F32 = jnp.float32
BF16 = jnp.bfloat16
I32 = jnp.int32

D_MODEL = 1024
POOL_WIDTH = 512
POOL_WINDOWS = (2, 4, 8, 16)
POOL_GC = 128
POOL_STATE = 15
NSA_WIDTH = 512
HEAD_DIM = 64
N_HEADS = 8
KV_HEADS = 2
GROUP = 4
N_BRANCH = 3
CMP_LEN = 32
CMP_STRIDE = 16
SEL_BLOCK = 64
N_SELECT = 16
WINDOW = 512
QBLK = 128
KV_W = 2 * KV_HEADS * HEAD_DIM
N_GATE = N_HEADS * N_BRANCH
LANE = 128
GATE_PAD = LANE
X_HEADS = 4
X_HEAD_DIM = 256
D_FF = 2816
EPS = 1e-6
NEG = -1e30
VMEM_LIMIT = 56 * 1024 * 1024
CHUNK_PITCH = 24
PAGE_UNROLL = 16
SEL_TILES = 2

def _dot(a, b):
    return jnp.dot(a.astype(BF16), b.astype(BF16), preferred_element_type=F32)


def _dot_nt(a, b):
    return lax.dot_general(a.astype(BF16), b.astype(BF16), (((1,), (1,)), ((), ())),
                           preferred_element_type=F32)


def _rms(x, g):
    return x * lax.rsqrt(jnp.mean(x * x, axis=-1, keepdims=True) + EPS) * g


def _split3(x):
    p0 = x.astype(BF16)
    r = x - p0.astype(F32)
    p1 = r.astype(BF16)
    p2 = (r - p1.astype(F32)).astype(BF16)
    return p0, p1, p2


def _masked_softmax(s, mask):
    s = jnp.where(mask, s, -jnp.inf)
    m = jnp.max(s, axis=-1, keepdims=True)
    m = jnp.where(m == -jnp.inf, 0.0, m)
    e = jnp.exp(s - m)
    return e / jnp.maximum(jnp.sum(e, axis=-1, keepdims=True), 1e-30)


def _params(sem):
    return pltpu.CompilerParams(dimension_semantics=sem, vmem_limit_bytes=VMEM_LIMIT)


def _rms_proj_kernel(x_ref, g_ref, w_ref, *out_refs, widths, sigmoid_last):
    xn = _rms(x_ref[...], g_ref[...]).astype(BF16)
    off = 0
    for k, (o_ref, wd) in enumerate(zip(out_refs, widths)):
        y = jnp.dot(xn, w_ref[:, off:off + wd], preferred_element_type=F32)
        if sigmoid_last and k == len(widths) - 1:
            y = 1.0 / (1.0 + jnp.exp(-y))
        o_ref[...] = y
        off += wd


def _rms_proj(x, g, w, widths, tm, sigmoid_last=False, name="rms_proj"):
    n, d = x.shape
    ntot = w.shape[1]
    assert sum(widths) == ntot and n % tm == 0
    return pl.pallas_call(
        functools.partial(_rms_proj_kernel, widths=tuple(widths), sigmoid_last=sigmoid_last),
        grid=(n // tm,),
        in_specs=[pl.BlockSpec((tm, d), lambda i: (i, 0)),
                  pl.BlockSpec((1, d), lambda i: (0, 0)),
                  pl.BlockSpec((d, ntot), lambda i: (0, 0))],
        out_specs=[pl.BlockSpec((tm, wd), lambda i: (i, 0)) for wd in widths],
        out_shape=[jax.ShapeDtypeStruct((n, wd), F32) for wd in widths],
        compiler_params=_params(("arbitrary",)),
        name=name,
    )(x, g, w)


def _mix_in_kernel(x_ref, g_ref, wt_ref, wf_ref, u_ref, q_ref, kc_ref, ks_ref, gt_ref, kv4_ref, kvw_ref):
    xn = _rms(x_ref[...], g_ref[...]).astype(BF16)
    off = 0
    for o_ref in (u_ref, q_ref, kc_ref, ks_ref, gt_ref):
        wd = o_ref.shape[-1]
        y = jnp.dot(xn, wt_ref[:, off:off + wd], preferred_element_type=F32)
        o_ref[...] = 1.0 / (1.0 + jnp.exp(-y)) if o_ref is gt_ref else y
        off += wd
    kv_t = lax.dot_general(wf_ref[...], xn, (((1,), (1,)), ((), ())), preferred_element_type=F32)
    kv4_ref[0] = kv_t[0:2 * KV_W]
    kvw_ref[0] = kv_t[2 * KV_W:3 * KV_W]


def _mix_in(x3, g, w_tok, w_feat, tm, name):
    b, t, d = x3.shape
    n = b * t
    tps = t // tm
    widths = (POOL_WIDTH, NSA_WIDTH, KV_W, LANE, GATE_PAD)
    assert sum(widths) == w_tok.shape[1] and t % tm == 0
    return pl.pallas_call(
        _mix_in_kernel,
        grid=(n // tm,),
        in_specs=[pl.BlockSpec((tm, d), lambda i: (i, 0)),
                  pl.BlockSpec((1, d), lambda i: (0, 0)),
                  pl.BlockSpec(w_tok.shape, lambda i: (0, 0)),
                  pl.BlockSpec(w_feat.shape, lambda i: (0, 0))],
        out_specs=[pl.BlockSpec((tm, wd), lambda i: (i, 0)) for wd in widths]
        + [pl.BlockSpec((1, 2 * KV_W, tm), lambda i: (i // tps, 0, i % tps)),
           pl.BlockSpec((1, KV_W, tm), lambda i: (i // tps, 0, i % tps))],
        out_shape=[jax.ShapeDtypeStruct((n, wd), F32) for wd in widths]
        + [jax.ShapeDtypeStruct((b, 2 * KV_W, t), F32), jax.ShapeDtypeStruct((b, KV_W, t), F32)],
        compiler_params=_params(("arbitrary",)),
        name=name,
    )(x3.reshape(n, d), g, w_tok, w_feat)


def _pool_tail(s_list, u, cnts, w_ref, sc_ref, y_ref, lead):
    for gi in range(len(POOL_WINDOWS)):
        cols = slice(gi * POOL_GC, (gi + 1) * POOL_GC)
        d = s_list[gi] / cnts[gi] - u[:, cols]
        y = _dot(d, w_ref[gi]) * sc_ref[:, cols]
        if lead:
            y_ref[0, :, cols] = y
        else:
            y_ref[:, cols] = y


def _pool_prompt_kernel(u_ref, w_ref, sc_ref, y_ref, ext_ref, *, tm):
    t = pl.program_id(1)
    hist = 16

    @pl.when(t == 0)
    def _():
        ext_ref[0:hist, :] = jnp.zeros((hist, POOL_WIDTH), F32)

    @pl.when(t > 0)
    def _():
        ext_ref[0:hist, :] = ext_ref[tm:tm + hist, :]

    u = u_ref[0]
    ext_ref[hist:hist + tm, :] = u
    pos = t * tm + lax.broadcasted_iota(I32, (tm, 1), 0)
    s_list, cnts = [], []
    for gi, w in enumerate(POOL_WINDOWS):
        cols = slice(gi * POOL_GC, (gi + 1) * POOL_GC)
        s = u[:, cols]
        for k in range(1, w):
            s = s + ext_ref[hist - k:hist - k + tm, cols]
        s_list.append(s)
        cnts.append(jnp.minimum(pos + 1, w).astype(F32))
    _pool_tail(s_list, u, cnts, w_ref, sc_ref, y_ref, True)


def _pool_prompt(u3, w_pool, pool_scale, tm):
    b, t, _ = u3.shape
    return pl.pallas_call(
        functools.partial(_pool_prompt_kernel, tm=tm),
        grid=(b, t // tm),
        in_specs=[pl.BlockSpec((1, tm, POOL_WIDTH), lambda i, j: (i, j, 0)),
                  pl.BlockSpec((4, POOL_GC, POOL_GC), lambda i, j: (0, 0, 0)),
                  pl.BlockSpec((1, POOL_WIDTH), lambda i, j: (0, 0))],
        out_specs=pl.BlockSpec((1, tm, POOL_WIDTH), lambda i, j: (i, j, 0)),
        out_shape=jax.ShapeDtypeStruct((b, t, POOL_WIDTH), F32),
        scratch_shapes=[pltpu.VMEM((tm + 16, POOL_WIDTH), F32)],
        compiler_params=_params(("arbitrary", "arbitrary")),
        name="pool_prompt",
    )(u3, w_pool, pool_scale)


def _pool_sample_kernel(st_ref, u_ref, w_ref, sc_ref, y_ref, *, pos0):
    u = u_ref[...]
    s_list, cnts = [], []
    for gi, w in enumerate(POOL_WINDOWS):
        cols = slice(gi * POOL_GC, (gi + 1) * POOL_GC)
        s = u[:, cols]
        for k in range(1, w):
            s = s + st_ref[POOL_STATE - k, :, cols]
        s_list.append(s)
        cnts.append(float(min(pos0 + 1, w)))
    _pool_tail(s_list, u, cnts, w_ref, sc_ref, y_ref, False)


def _pool_sample(st_t, u, w_pool, pool_scale, pos0):
    n = u.shape[0]
    return pl.pallas_call(
        functools.partial(_pool_sample_kernel, pos0=pos0),
        grid=(1,),
        in_specs=[pl.BlockSpec((POOL_STATE, n, POOL_WIDTH), lambda i: (0, 0, 0)),
                  pl.BlockSpec((n, POOL_WIDTH), lambda i: (0, 0)),
                  pl.BlockSpec((4, POOL_GC, POOL_GC), lambda i: (0, 0, 0)),
                  pl.BlockSpec((1, POOL_WIDTH), lambda i: (0, 0))],
        out_specs=pl.BlockSpec((n, POOL_WIDTH), lambda i: (0, 0)),
        out_shape=jax.ShapeDtypeStruct((n, POOL_WIDTH), F32),
        compiler_params=_params(("arbitrary",)),
        name="pool_sample",
    )(st_t, u, w_pool, pool_scale)


def _compress_rows(load_half, wbig_ref, pe_ref, wpe_ref, n_chunk):
    pe_row = _dot(pe_ref[...], wpe_ref[...])[0:1]
    halves = []
    for c in range(2):
        x = jnp.concatenate([load_half(c, s).astype(BF16) for s in range(CMP_STRIDE)], axis=1)
        y = jnp.dot(x, wbig_ref[c], preferred_element_type=F32)
        halves.append(y[:, :LANE] + pltpu.roll(y[:, LANE:], n_chunk - 1, axis=0))
    return jnp.concatenate(halves, axis=1) + pe_row


def _compress_prompt_kernel(k_ref, v_ref, wbig_ref, pe_ref, wpe_ref, out_ref, *, n_chunk):
    load = lambda c, s: (k_ref, v_ref)[c][0, pl.ds(s, n_chunk, stride=CMP_STRIDE), :]
    out_ref[0] = _compress_rows(load, wbig_ref, pe_ref, wpe_ref, n_chunk)


def _compress_prompt(kv4, wbig, pe8, wpe):
    b, t, _ = kv4.shape
    n_chunk = t // CMP_STRIDE
    return pl.pallas_call(
        functools.partial(_compress_prompt_kernel, n_chunk=n_chunk),
        grid=(b,),
        in_specs=[pl.BlockSpec((1, t, LANE), lambda i: (i, 0, 0)),
                  pl.BlockSpec((1, t, LANE), lambda i: (i, 0, 1)),
                  pl.BlockSpec(wbig.shape, lambda i: (0, 0, 0)),
                  pl.BlockSpec(pe8.shape, lambda i: (0, 0)),
                  pl.BlockSpec(wpe.shape, lambda i: (0, 0))],
        out_specs=pl.BlockSpec((1, n_chunk, KV_W), lambda i: (i, 0, 0)),
        out_shape=jax.ShapeDtypeStruct((b, n_chunk, KV_W), F32),
        compiler_params=_params(("arbitrary",)),
        name="compress_prompt",
    )(kv4, kv4, wbig, pe8, wpe)


def _stack_group_queries(q, g):
    lane = lax.broadcasted_iota(I32, (q.shape[0], LANE), 1)
    parts = []
    for hh in range(GROUP):
        h = GROUP * g + hh
        ch = q[:, (h // 2) * LANE:(h // 2 + 1) * LANE]
        if h % 2 != g:
            ch = pltpu.roll(ch, HEAD_DIM, axis=1)
        parts.append(jnp.where((lane >> 6) == g, ch, 0.0))
    return jnp.concatenate(parts, axis=0).astype(BF16)


def _place_heads(o_g, weights, g, outs):
    lane = lax.broadcasted_iota(I32, (QBLK, LANE), 1)
    for hh in range(GROUP):
        h = GROUP * g + hh
        rs = slice(hh * QBLK, (hh + 1) * QBLK)
        piece = sum(w_[:, 0:1] * o_[rs] for w_, o_ in zip(weights(h), o_g))
        if h % 2 != g:
            piece = pltpu.roll(piece, HEAD_DIM, axis=1)
        piece = jnp.where((lane >> 6) == (h % 2), piece, 0.0)
        outs[h // 2] = piece if outs[h // 2] is None else outs[h // 2] + piece


def _nsa_dense_kernel(q_ref, gt_ref, kvc_ref, kw_ref, vw_ref, part_ref, sel_ref, *, t_len):
    i = pl.program_id(1)
    t0 = i * QBLK
    nc = t_len // CMP_STRIDE
    ns = t_len // SEL_BLOCK
    wl = min(WINDOW + QBLK, t_len)
    q = q_ref[0] * (HEAD_DIM ** -0.5)
    gt = gt_ref[0]
    pos_q = t0 + lax.broadcasted_iota(I32, (QBLK, 1), 0)

    oj = lax.broadcasted_iota(I32, (ns, nc), 0)
    on = lax.broadcasted_iota(I32, (ns, nc), 1)
    ovl_t = jnp.where((on * CMP_STRIDE + CMP_LEN - 1 >= oj * SEL_BLOCK)
                      & (on * CMP_STRIDE <= oj * SEL_BLOCK + SEL_BLOCK - 1), 1.0, 0.0).astype(BF16)
    jb = lax.broadcasted_iota(I32, (ns, QBLK), 0)
    jq = (t0 + lax.broadcasted_iota(I32, (ns, QBLK), 1)) >> 6
    forced = (jb == 0) | (jb == jq) | (jb == jq - 1)

    kc_all = kvc_ref[0, :, 0:LANE].astype(BF16)
    vc_all = kvc_ref[0, :, LANE:2 * LANE].astype(BF16)
    ws = pl.multiple_of(jnp.maximum(t0 - WINDOW, 0), QBLK)
    kwc = kw_ref[0, :, pl.ds(ws, wl)].astype(BF16)
    vwc = vw_ref[0, :, pl.ds(ws, wl)].astype(BF16)

    ncol = lax.broadcasted_iota(I32, (1, nc), 1)
    bias_c = jnp.where((ncol * CMP_STRIDE + CMP_LEN - 1) <= pos_q, 0.0, NEG)
    bias_c = jnp.concatenate([bias_c] * GROUP, axis=0)
    dpos = pos_q - (ws + lax.broadcasted_iota(I32, (1, wl), 1))
    bias_w = jnp.where((dpos >= 0) & (dpos <= WINDOW), 0.0, NEG)
    bias_w = jnp.concatenate([bias_w] * GROUP, axis=0)

    def softmax_parts(s):
        m = jnp.max(s, axis=-1, keepdims=True)
        e = jnp.exp(s - m)
        return e, jnp.where(m > 0.5 * NEG, 1.0 / jnp.sum(e, axis=-1, keepdims=True), 0.0)

    outs = [None] * (NSA_WIDTH // LANE)
    for g in range(KV_HEADS):
        qg = _stack_group_queries(q, g)

        e_c, inv_c = softmax_parts(_dot_nt(qg, kc_all) + bias_c)
        o_c = _dot(e_c, vc_all) * inv_c

        p_c = e_c * inv_c
        psum = p_c[0:QBLK] + p_c[QBLK:2 * QBLK] + p_c[2 * QBLK:3 * QBLK] + p_c[3 * QBLK:4 * QBLK]
        imp_t = sum(lax.dot_general(ovl_t, part, (((1,), (1,)), ((), ())), preferred_element_type=F32)
                    for part in _split3(psum))
        x = jnp.where(forced, jnp.inf, jnp.where(jb > jq, -jnp.inf, imp_t))
        cnt = jnp.zeros((ns, QBLK), I32)
        for jp in range(ns):
            row = x[jp:jp + 1, :]
            beats = (row > x) | ((row == x) & (jb > jp))
            cnt = cnt + jnp.where(beats, 1, 0)
        sel_ref[0, 0, g * ns:(g + 1) * ns, :] = jnp.where(cnt < N_SELECT, 1.0, 0.0)

        e_w, inv_w = softmax_parts(_dot(qg, kwc) + bias_w)
        o_w = _dot_nt(e_w, vwc) * inv_w

        _place_heads((o_c, o_w), lambda h: (gt[:, 3 * h:3 * h + 1], gt[:, 3 * h + 2:3 * h + 3]), g, outs)
    part_ref[0] = jnp.concatenate(outs, axis=1)


def _nsa_select_kernel(q_ref, gt_ref, sel_ref, part_ref, ks_ref, vs_ref, o_ref, *, t_len, kc):
    i = pl.program_id(1)
    nqry = SEL_TILES * QBLK
    t0 = i * nqry
    ns = t_len // SEL_BLOCK
    rows = GROUP * nqry
    kcs = kc
    n_small = (t0 + nqry + kcs[-1] - 1) // kcs[-1]
    gt_t = gt_ref[0].T
    blocks = []
    q_t = (q_ref[0] * (HEAD_DIM ** -0.5)).T
    zero_half = jnp.zeros((HEAD_DIM, nqry), F32)
    for g in range(KV_HEADS):
        heads = []
        for hh in range(GROUP):
            q_h = q_t[(GROUP * g + hh) * HEAD_DIM:(GROUP * g + hh + 1) * HEAD_DIM]
            heads.append(jnp.concatenate([q_h, zero_half] if g == 0 else [zero_half, q_h], axis=0))
        qg_t = jnp.concatenate(heads, axis=1).astype(BF16)

        def sel_chunk(kcz, base, c, carry):
            m_prev, l_prev, acc = carry
            k0 = pl.multiple_of(base + c * kcz, kcs[-1])
            nblk = kcz // SEL_BLOCK
            s = _dot(ks_ref[0, pl.ds(k0, kcz), :], qg_t)
            j0 = pl.multiple_of(g * ns + k0 // SEL_BLOCK, kcs[-1] // SEL_BLOCK)
            chosen = []
            for tile in range(SEL_TILES):
                rows_j = sel_ref[0, tile, pl.ds(j0, nblk), :]
                chosen.append(jnp.concatenate([jnp.broadcast_to(rows_j[j:j + 1], (SEL_BLOCK, QBLK))
                                               for j in range(nblk)], axis=0))
            chosen = jnp.concatenate(chosen, axis=1)
            key_pos = k0 + lax.broadcasted_iota(I32, (kcz, nqry), 0)
            qry_pos = t0 + lax.broadcasted_iota(I32, (kcz, nqry), 1)
            bias = jnp.where((chosen > 0.5) & (key_pos <= qry_pos), 0.0, NEG)
            s = s + jnp.concatenate([bias] * GROUP, axis=1)
            m_new = jnp.maximum(m_prev, jnp.max(s, axis=0, keepdims=True))
            alpha = jnp.exp(m_prev - m_new)
            p = jnp.exp(s - m_new)
            l_new = alpha * l_prev + jnp.sum(p, axis=0, keepdims=True)
            return m_new, l_new, alpha * acc + _dot(vs_ref[0, :, pl.ds(k0, kcz)], p)

        carry = (jnp.full((1, rows), NEG, F32), jnp.zeros((1, rows), F32), jnp.zeros((LANE, rows), F32))
        done = 0
        for kcz in kcs:
            count = (n_small * kcs[-1] - done) // kcz
            carry = lax.fori_loop(0, count, functools.partial(sel_chunk, kcz, done), carry)
            done = done + count * kcz
        _, l_s, acc_s = carry
        o_t = acc_s[g * HEAD_DIM:(g + 1) * HEAD_DIM] / l_s
        for hh in range(GROUP):
            h = GROUP * g + hh
            blocks.append(o_t[:, hh * nqry:(hh + 1) * nqry] * gt_t[3 * h + 1:3 * h + 2])
    o_ref[0] = part_ref[0] + jnp.concatenate(blocks, axis=0).T


def _nsa_prompt(q3, gt3, kvc, ks_tok, kv4t, kvwt, kc):
    b, t, _ = q3.shape
    kc = tuple(k for k in kc if k <= t)
    assert all(t % k == 0 for k in kc) and t % QBLK == 0
    nq = t // QBLK
    ns = t // SEL_BLOCK
    slab = lambda c: pl.BlockSpec((1, LANE, t), lambda i, j: (i, c, 0))
    tile = lambda w_: pl.BlockSpec((1, QBLK, w_), lambda i, j: (i, j, 0))
    sel_spec = pl.BlockSpec((1, 1, KV_HEADS * ns, QBLK), lambda i, j: (i, j, 0, 0))
    part, sel = pl.pallas_call(
        functools.partial(_nsa_dense_kernel, t_len=t),
        grid=(b, nq),
        in_specs=[tile(NSA_WIDTH), tile(GATE_PAD),
                  pl.BlockSpec((1, t // CMP_STRIDE, KV_W), lambda i, j: (i, 0, 0)),
                  slab(0), slab(1)],
        out_specs=[tile(NSA_WIDTH), sel_spec],
        out_shape=[jax.ShapeDtypeStruct((b, t, NSA_WIDTH), F32),
                   jax.ShapeDtypeStruct((b, nq, KV_HEADS * ns, QBLK), F32)],
        compiler_params=_params(("arbitrary", "arbitrary")),
        name="nsa_dense",
    )(q3, gt3, kvc, kvwt, kvwt)
    assert nq % SEL_TILES == 0
    wide = lambda w_: pl.BlockSpec((1, SEL_TILES * QBLK, w_), lambda i, j: (i, j, 0))
    return pl.pallas_call(
        functools.partial(_nsa_select_kernel, t_len=t, kc=kc),
        grid=(b, nq // SEL_TILES),
        in_specs=[wide(NSA_WIDTH), wide(GATE_PAD),
                  pl.BlockSpec((1, SEL_TILES, KV_HEADS * ns, QBLK), lambda i, j: (i, j, 0, 0)), wide(NSA_WIDTH),
                  pl.BlockSpec((1, t, LANE), lambda i, j: (i, 0, 0)), slab(3)],
        out_specs=wide(NSA_WIDTH),
        out_shape=jax.ShapeDtypeStruct((b, t, NSA_WIDTH), F32),
        compiler_params=_params(("arbitrary", "arbitrary")),
        name="nsa_select",
    )(q3, gt3, sel, part, ks_tok, kv4t)


def _xattn_prompt_kernel(x_ref, yp_ref, yn_ref, wo_ref, g_ref, wq_ref, mem_ref, wxo_ref, o_ref):
    y = jnp.concatenate([yp_ref[0], yn_ref[0]], axis=1)
    h1 = x_ref[0] + _dot(y, wo_ref[...])
    qx = _dot(_rms(h1, g_ref[...]), wq_ref[...]) * (X_HEAD_DIM ** -0.5)
    xw = X_HEADS * X_HEAD_DIM
    outs = []
    for h in range(X_HEADS):
        cs = slice(h * X_HEAD_DIM, (h + 1) * X_HEAD_DIM)
        s = _dot_nt(qx[:, cs], mem_ref[0, :, cs])
        e = jnp.exp(s - jnp.max(s, axis=-1, keepdims=True))
        p = e / jnp.sum(e, axis=-1, keepdims=True)
        outs.append(_dot(p, mem_ref[0, :, xw + h * X_HEAD_DIM:xw + (h + 1) * X_HEAD_DIM]))
    o_ref[0] = h1 + _dot(jnp.concatenate(outs, axis=1), wxo_ref[...])


def _xattn_prompt(x3, yp, yn, w_out, g_x, w_xq, mem_kv, w_xo, tm):
    b, t, d = x3.shape
    m = mem_kv.shape[1]
    full = lambda a: pl.BlockSpec(a.shape, lambda i, j: (0,) * a.ndim)
    return pl.pallas_call(
        _xattn_prompt_kernel,
        grid=(b, t // tm),
        in_specs=[pl.BlockSpec((1, tm, d), lambda i, j: (i, j, 0)),
                  pl.BlockSpec((1, tm, POOL_WIDTH), lambda i, j: (i, j, 0)),
                  pl.BlockSpec((1, tm, NSA_WIDTH), lambda i, j: (i, j, 0)),
                  full(w_out), full(g_x), full(w_xq),
                  pl.BlockSpec((1, m, mem_kv.shape[2]), lambda i, j: (i, 0, 0)),
                  full(w_xo)],
        out_specs=pl.BlockSpec((1, tm, d), lambda i, j: (i, j, 0)),
        out_shape=jax.ShapeDtypeStruct((b, t, d), F32),
        compiler_params=_params(("arbitrary", "arbitrary")),
        name="xattn_prompt",
    )(x3, yp, yn, w_out, g_x, w_xq, mem_kv, w_xo)


def _ffn_core(j, nj, h_ref, gf_ref, wg_ref, wv_ref, cw_ref, cb_ref, wd_ref, gfin_ref, y_ref,
              xn_ref, acc_ref, prev_rows):
    @pl.when(j == 0)
    def _():
        xn_ref[...] = _rms(h_ref[...], gf_ref[...]).astype(BF16)
        acc_ref[...] = jnp.zeros(acc_ref.shape, F32)

    xn = xn_ref[...]
    up_g = jnp.dot(xn, wg_ref[...], preferred_element_type=F32)
    up_v = jnp.dot(xn, wv_ref[...], preferred_element_type=F32)
    g1, g2, v1, v2 = prev_rows(up_g, up_v)
    cg = cb_ref[0:1, :] + cw_ref[0, 0:1, :] * g2 + cw_ref[1, 0:1, :] * g1 + cw_ref[2, 0:1, :] * up_g
    cv = cb_ref[1:2, :] + cw_ref[0, 1:2, :] * v2 + cw_ref[1, 1:2, :] * v1 + cw_ref[2, 1:2, :] * up_v
    act = cg / (1.0 + jnp.exp(-cg)) * cv
    acc_ref[...] += _dot(act, wd_ref[...])

    @pl.when(j == nj - 1)
    def _():
        y_ref[...] = _rms(h_ref[...] + acc_ref[...], gfin_ref[...])

    return up_g, up_v


def _ffn_prompt_kernel(h_ref, gf_ref, wg_ref, wv_ref, cw_ref, cb_ref, wd_ref, gfin_ref,
                       y_ref, st_ref, xn_ref, acc_ref, carry_ref, *, tiles_per_seq, nj, tm):
    r = pl.program_id(0)
    j = pl.program_id(1)
    row = lax.broadcasted_iota(I32, (tm, 1), 0)

    @pl.when((r % tiles_per_seq) == 0)
    def _():
        carry_ref[j] = jnp.zeros(carry_ref.shape[1:], F32)

    def prev_rows(up_g, up_v):
        res = []
        for half, up in enumerate((up_g, up_v)):
            prev = carry_ref[j, half]
            m1 = jnp.where(row == 0, prev[7:8], pltpu.roll(up, 1, axis=0))
            m2 = jnp.where(row == 0, prev[6:7], jnp.where(row == 1, prev[7:8], pltpu.roll(up, 2, axis=0)))
            res += [m1, m2]
        return res

    up_g, up_v = _ffn_core(j, nj, h_ref, gf_ref, wg_ref, wv_ref, cw_ref, cb_ref, wd_ref, gfin_ref,
                           y_ref, xn_ref, acc_ref, prev_rows)
    for half, up in enumerate((up_g, up_v)):
        carry_ref[j, half] = up[tm - 8:tm]
        st_ref[0, half] = up[tm - 8:tm]


def _ffn_specs(d, tf, nj):
    return [pl.BlockSpec((1, d), lambda r, j: (0, 0)),
            pl.BlockSpec((d, tf), lambda r, j: (0, j)),
            pl.BlockSpec((d, tf), lambda r, j: (0, nj + j)),
            pl.BlockSpec((3, 2, tf), lambda r, j: (0, 0, j)),
            pl.BlockSpec((2, tf), lambda r, j: (0, j)),
            pl.BlockSpec((tf, d), lambda r, j: (j, 0)),
            pl.BlockSpec((1, d), lambda r, j: (0, 0))]


def _ffn_prompt(h2, g_ffn, w_up, cw3, cb2, w_down, g_final, batch, tm, tf):
    n, d = h2.shape
    nj = D_FF // tf
    tps = n // batch // tm
    return pl.pallas_call(
        functools.partial(_ffn_prompt_kernel, tiles_per_seq=tps, nj=nj, tm=tm),
        grid=(n // tm, nj),
        in_specs=[pl.BlockSpec((tm, d), lambda r, j: (r, 0))] + _ffn_specs(d, tf, nj),
        out_specs=[pl.BlockSpec((tm, d), lambda r, j: (r, 0)),
                   pl.BlockSpec((1, 2, 8, tf), lambda r, j: (r, 0, 0, j))],
        out_shape=[jax.ShapeDtypeStruct((n, d), F32),
                   jax.ShapeDtypeStruct((n // tm, 2, 8, D_FF), F32)],
        scratch_shapes=[pltpu.VMEM((tm, d), BF16), pltpu.VMEM((tm, d), F32),
                        pltpu.VMEM((nj, 2, 8, tf), F32)],
        compiler_params=_params(("arbitrary", "arbitrary")),
        name="ffn_prompt",
    )(h2, g_ffn, w_up, w_up, cw3, cb2, w_down, g_final)


def _ffn_sample_kernel(h_ref, gf_ref, wg_ref, wv_ref, cw_ref, cb_ref, wd_ref, gfin_ref, sg_ref, sv_ref,
                       y_ref, up_ref, xn_ref, acc_ref, *, nj):
    j = pl.program_id(1)
    prev_rows = lambda up_g, up_v: (sg_ref[1], sg_ref[0], sv_ref[1], sv_ref[0])
    up_g, up_v = _ffn_core(j, nj, h_ref, gf_ref, wg_ref, wv_ref, cw_ref, cb_ref, wd_ref, gfin_ref,
                           y_ref, xn_ref, acc_ref, prev_rows)
    up_ref[0] = up_g
    up_ref[1] = up_v


def _ffn_sample(h2, g_ffn, w_up, cw3, cb2, w_down, g_final, st_t, tf):
    n, d = h2.shape
    nj = D_FF // tf
    return pl.pallas_call(
        functools.partial(_ffn_sample_kernel, nj=nj),
        grid=(1, nj),
        in_specs=[pl.BlockSpec((n, d), lambda r, j: (0, 0))] + _ffn_specs(d, tf, nj)
        + [pl.BlockSpec((2, n, tf), lambda r, j: (0, 0, j)),
           pl.BlockSpec((2, n, tf), lambda r, j: (0, 0, nj + j))],
        out_specs=[pl.BlockSpec((n, d), lambda r, j: (0, 0)),
                   pl.BlockSpec((2, n, tf), lambda r, j: (0, 0, j))],
        out_shape=[jax.ShapeDtypeStruct((n, d), F32),
                   jax.ShapeDtypeStruct((2, n, D_FF), F32)],
        scratch_shapes=[pltpu.VMEM((n, d), BF16), pltpu.VMEM((n, d), F32)],
        compiler_params=_params(("arbitrary", "arbitrary")),
        name="ffn_sample",
    )(h2, g_ffn, w_up, w_up, cw3, cb2, w_down, g_final, st_t, st_t)


def _head_rows(qrow):
    rowi = lax.broadcasted_iota(I32, (N_HEADS, LANE), 0)
    lane = lax.broadcasted_iota(I32, (N_HEADS, LANE), 1)
    x = jnp.zeros((N_HEADS, LANE), F32)
    for c in range(NSA_WIDTH // LANE):
        x = jnp.where((rowi >> 1) == c, jnp.broadcast_to(qrow[:, c * LANE:(c + 1) * LANE], (N_HEADS, LANE)), x)
    x = jnp.where((rowi & 1) != (rowi >> 2), pltpu.roll(x, HEAD_DIM, axis=1), x)
    return jnp.where((lane >> 6) == (rowi >> 2), x, 0.0)


def _page_copy(cache_ref, raw_ref, sem_ref, phys, p, slot):
    return pltpu.make_async_copy(cache_ref.at[phys, pl.ds(0, KV_W), :], raw_ref.at[slot, p], sem_ref.at[slot, p])


def _nsa_sample_cmp_kernel(pt_ref, q_ref, cache_ref, wbig_ref, pe_ref, wpe_ref, oc_ref, imp_ref,
                           raw_ref, tok_ref, sem_ref, *, n_pages, n_batch, page):
    b = pl.program_id(0)
    slot = b % 2
    n_tok = n_pages * page
    n_chunk = n_tok // CMP_STRIDE

    def start_all(bb, sl):
        def body(p, _):
            _page_copy(cache_ref, raw_ref, sem_ref, pt_ref[bb * n_pages + p], p, sl).start()
            return 0
        lax.fori_loop(0, n_pages, body, 0)

    @pl.when(b == 0)
    def _():
        start_all(0, 0)

    @pl.when(b + 1 < n_batch)
    def _():
        start_all(b + 1, 1 - slot)

    chunks_per_page = page // CMP_STRIDE

    def to_token_major(pg, _):
        for pp in range(PAGE_UNROLL):
            _page_copy(cache_ref, raw_ref, sem_ref, 0, pg * PAGE_UNROLL + pp, slot).wait()
        for pp in range(PAGE_UNROLL):
            p = pg * PAGE_UNROLL + pp
            xt = raw_ref[slot, p]
            for half in range(2):
                x = xt[half * LANE:(half + 1) * LANE].T
                for nl in range(chunks_per_page):
                    r0 = pl.multiple_of((p * chunks_per_page + nl) * CHUNK_PITCH, 8)
                    tok_ref[half, pl.ds(r0, CMP_STRIDE), :] = x[nl * CMP_STRIDE:(nl + 1) * CMP_STRIDE]
        return 0
    lax.fori_loop(0, n_pages // PAGE_UNROLL, to_token_major, 0)

    load = lambda c, s: tok_ref[c, pl.ds(s, n_chunk, stride=CHUNK_PITCH), :]
    kvc = _compress_rows(load, wbig_ref, pe_ref, wpe_ref, n_chunk)
    q8 = _head_rows(q_ref[0] * (HEAD_DIM ** -0.5))
    s = _dot_nt(q8, kvc[:, 0:LANE])
    ncol = lax.broadcasted_iota(I32, (1, n_chunk), 1)
    p = _masked_softmax(s, (ncol * CMP_STRIDE + CMP_LEN - 1) <= n_tok)
    oc_ref[0] = _dot(p, kvc[:, LANE:2 * LANE])
    rowi = lax.broadcasted_iota(I32, (N_HEADS, n_chunk), 0)
    psum = jnp.where(rowi == 0, jnp.sum(p[0:GROUP], axis=0, keepdims=True),
                     jnp.where(rowi == 1, jnp.sum(p[GROUP:2 * GROUP], axis=0, keepdims=True), 0.0))
    ns_pad = imp_ref.shape[2]
    on = lax.broadcasted_iota(I32, (n_chunk, ns_pad), 0)
    oj = lax.broadcasted_iota(I32, (n_chunk, ns_pad), 1)
    ovl = jnp.where((on * CMP_STRIDE + CMP_LEN - 1 >= oj * SEL_BLOCK)
                    & (on * CMP_STRIDE <= oj * SEL_BLOCK + SEL_BLOCK - 1), 1.0, 0.0).astype(BF16)
    imp_ref[0] = sum(jnp.dot(part, ovl, preferred_element_type=F32) for part in _split3(psum))


def _nsa_sample_cmp(pt_flat, q3, cache3, wbig, pe8, wpe, n_pages, ns_pad):
    nb = q3.shape[0]
    page = cache3.shape[2]
    assert page == LANE
    n_tok = n_pages * page
    gs = pltpu.PrefetchScalarGridSpec(
        num_scalar_prefetch=1, grid=(nb,),
        in_specs=[pl.BlockSpec((1, 1, NSA_WIDTH), lambda i, pt: (i, 0, 0)),
                  pl.BlockSpec(memory_space=pl.ANY),
                  pl.BlockSpec(wbig.shape, lambda i, pt: (0, 0, 0)),
                  pl.BlockSpec(pe8.shape, lambda i, pt: (0, 0)),
                  pl.BlockSpec(wpe.shape, lambda i, pt: (0, 0))],
        out_specs=[pl.BlockSpec((1, N_HEADS, LANE), lambda i, pt: (i, 0, 0)),
                   pl.BlockSpec((1, N_HEADS, ns_pad), lambda i, pt: (i, 0, 0))],
        scratch_shapes=[pltpu.VMEM((2, n_pages, KV_W, page), F32), pltpu.VMEM((2, n_tok // CMP_STRIDE * CHUNK_PITCH, LANE), F32),
                        pltpu.SemaphoreType.DMA((2, n_pages))])
    return pl.pallas_call(
        functools.partial(_nsa_sample_cmp_kernel, n_pages=n_pages, n_batch=nb, page=page),
        grid_spec=gs,
        out_shape=[jax.ShapeDtypeStruct((nb, N_HEADS, LANE), F32),
                   jax.ShapeDtypeStruct((nb, N_HEADS, ns_pad), F32)],
        compiler_params=_params(("arbitrary",)),
        name="nsa_sample_cmp",
    )(pt_flat, q3, cache3, wbig, pe8, wpe)


def _topk_sample_kernel(imp_ref, idx_ref, *, jq, k_past):
    x = imp_ref[...]
    col = lax.broadcasted_iota(I32, x.shape, 1)
    x = jnp.where((col == 0) | (col == jq - 1), jnp.inf, jnp.where(col >= jq, -jnp.inf, x))
    lane = lax.broadcasted_iota(I32, idx_ref.shape, 1)
    out = jnp.zeros(idx_ref.shape, I32)
    big = jnp.int32(1 << 30)
    for k in range(k_past):
        m = jnp.max(x, axis=-1, keepdims=True)
        idx = jnp.min(jnp.where(x == m, col, big), axis=-1, keepdims=True)
        out = jnp.where(lane == k, idx, out)
        x = jnp.where(col == idx, -jnp.inf, x)
    idx_ref[...] = out


def _topk_sample(imp2, jq, k_past):
    n, w = imp2.shape
    return pl.pallas_call(
        functools.partial(_topk_sample_kernel, jq=jq, k_past=k_past),
        grid=(1,),
        in_specs=[pl.BlockSpec((n, w), lambda i: (0, 0))],
        out_specs=pl.BlockSpec((n, LANE), lambda i: (0, 0)),
        out_shape=jax.ShapeDtypeStruct((n, LANE), I32),
        compiler_params=_params(("arbitrary",)),
        name="topk_sample",
    )(imp2)


def _blk_copy(cache_ref, buf_ref, sem_ref, phys, kv, g, r, slot):
    row0 = KV_W + kv * LANE + g * HEAD_DIM
    return pltpu.make_async_copy(cache_ref.at[phys, pl.ds(row0, HEAD_DIM), :],
                                 buf_ref.at[slot, kv, pl.ds(g * HEAD_DIM, HEAD_DIM), pl.ds(r * LANE, LANE)],
                                 sem_ref.at[slot])


def _nsa_sample_sel_kernel(idx_ref, pt_ref, q_ref, gt_ref, oc_ref, win_ref, kvn4_ref, kvnw_ref, cache_ref,
                           o_ref, wout_ref, buf_ref, sem_ref, *, n_pages, n_batch, page, k_past):
    b = pl.program_id(0)
    slot = b % 2
    per_page = page // SEL_BLOCK
    nl = k_past * page

    def for_blocks(bb, fn):
        def body(r, _):
            for g in range(KV_HEADS):
                j = idx_ref[(bb * KV_HEADS + g) * k_past + r]
                phys = pt_ref[bb * n_pages + j // per_page]
                for kv in range(2):
                    fn(phys, kv, g, r)
            return 0
        lax.fori_loop(0, k_past, body, 0)

    def start_all(bb, sl):
        for_blocks(bb, lambda phys, kv, g, r: _blk_copy(cache_ref, buf_ref, sem_ref, phys, kv, g, r, sl).start())

    @pl.when(b == 0)
    def _():
        start_all(0, 0)

    @pl.when(b + 1 < n_batch)
    def _():
        start_all(b + 1, 1 - slot)

    def wait_body(r, _):
        for g in range(KV_HEADS):
            for kv in range(2):
                _blk_copy(cache_ref, buf_ref, sem_ref, 0, kv, g, r, slot).wait()
        return 0
    lax.fori_loop(0, k_past, wait_body, 0)

    q8 = _head_rows(q_ref[0] * (HEAD_DIM ** -0.5))
    rowi = lax.broadcasted_iota(I32, (N_HEADS, 1), 0)
    lane = lax.broadcasted_iota(I32, (N_HEADS, LANE), 1)
    own = (lane >> 6) == (rowi >> 2)
    mine = jnp.where(lax.broadcasted_iota(I32, (N_HEADS, n_batch), 1) == b, 0.0, -jnp.inf)

    def attend(k_t, v_t, bias):
        s = _dot(q8, k_t) + bias
        e = jnp.exp(s - jnp.max(s, axis=-1, keepdims=True))
        return _dot_nt(e, v_t) / jnp.sum(e, axis=-1, keepdims=True)

    lane_s = lax.broadcasted_iota(I32, (1, nl), 1)
    halves = []
    for g in range(KV_HEADS):
        hv = jnp.zeros((1, nl), I32)
        for r in range(k_past):
            hv = jnp.where((lane_s // page) == r, idx_ref[(b * KV_HEADS + g) * k_past + r] % per_page, hv)
        halves.append(hv)
    vis = ((lane_s % page) // SEL_BLOCK) == jnp.where(rowi < GROUP, halves[0], halves[1])
    o_s = attend(jnp.concatenate([buf_ref[slot, 0], kvn4_ref[0, KV_W:KV_W + LANE]], axis=1),
                 jnp.concatenate([buf_ref[slot, 1], kvn4_ref[0, KV_W + LANE:2 * KV_W]], axis=1),
                 jnp.concatenate([jnp.where(vis, 0.0, -jnp.inf), mine], axis=1))

    cw = win_ref[0]
    wlen = cw.shape[1]
    o_w = attend(jnp.concatenate([cw[0:LANE], kvnw_ref[0, 0:LANE]], axis=1),
                 jnp.concatenate([cw[LANE:2 * LANE], kvnw_ref[0, LANE:2 * LANE]], axis=1),
                 jnp.concatenate([jnp.zeros((N_HEADS, wlen), F32), mine], axis=1))
    lane_n = lax.broadcasted_iota(I32, (KV_W, n_batch), 1)
    new_col = jnp.sum(jnp.where(lane_n == b, kvnw_ref[0], 0.0), axis=1, keepdims=True)
    lane_w = lax.broadcasted_iota(I32, (KV_W, wlen), 1)
    wout_ref[0] = jnp.where(lane_w == wlen - 1, new_col, pltpu.roll(cw, wlen - 1, axis=1))

    gt = jnp.broadcast_to(gt_ref[0], (N_HEADS, GATE_PAD))
    gate = lambda br: jnp.sum(jnp.where(lane == rowi * N_BRANCH + br, gt, 0.0), axis=-1, keepdims=True)
    o = gate(0) * oc_ref[0] + gate(1) * o_s + gate(2) * o_w
    o_ref[0] = jnp.where(own, o, 0.0)


def _nsa_sample_sel(idx_flat, pt_flat, q3, gt3, oc, win_t, kvn4_t, kvnw_t, cache3, n_pages, k_past):
    nb = q3.shape[0]
    page = cache3.shape[2]
    wlen = win_t.shape[2]
    assert page == LANE
    per = lambda *blk: pl.BlockSpec((1,) + blk, lambda i, a, c: (i, 0, 0))
    whole = lambda arr: pl.BlockSpec(arr.shape, lambda i, a, c: (0, 0, 0))
    gs = pltpu.PrefetchScalarGridSpec(
        num_scalar_prefetch=2, grid=(nb,),
        in_specs=[per(1, NSA_WIDTH), per(1, GATE_PAD), per(N_HEADS, LANE), per(KV_W, wlen),
                  whole(kvn4_t), whole(kvnw_t), pl.BlockSpec(memory_space=pl.ANY)],
        out_specs=[per(N_HEADS, LANE), per(KV_W, wlen)],
        scratch_shapes=[pltpu.VMEM((2, 2, LANE, k_past * page), F32), pltpu.SemaphoreType.DMA((2,))])
    return pl.pallas_call(
        functools.partial(_nsa_sample_sel_kernel, n_pages=n_pages, n_batch=nb, page=page, k_past=k_past),
        grid_spec=gs,
        out_shape=[jax.ShapeDtypeStruct((nb, N_HEADS, LANE), F32),
                   jax.ShapeDtypeStruct((nb, KV_W, wlen), F32)],
        compiler_params=_params(("arbitrary",)),
        name="nsa_sample_sel",
    )(idx_flat, pt_flat, q3, gt3, oc, win_t, kvn4_t, kvnw_t, cache3)


def _mix_sample_kernel(x_ref, yp_ref, yn_ref, wo_ref, g_ref, wq_ref, h1_ref, qx_ref):
    y = jnp.concatenate([yp_ref[...], yn_ref[...]], axis=1)
    h1 = x_ref[...] + _dot(y, wo_ref[...])
    h1_ref[...] = h1
    qx_ref[...] = _dot(_rms(h1, g_ref[...]), wq_ref[...]) * (X_HEAD_DIM ** -0.5)


def _mix_sample(x, yp, yn, w_out, g_x, w_xq):
    n, d = x.shape
    full = lambda a: pl.BlockSpec(a.shape, lambda i: (0,) * a.ndim)
    args = (x, yp, yn, w_out, g_x, w_xq)
    return pl.pallas_call(
        _mix_sample_kernel, grid=(1,),
        in_specs=[full(a) for a in args],
        out_specs=[pl.BlockSpec((n, d), lambda i: (0, 0))] * 2,
        out_shape=[jax.ShapeDtypeStruct((n, d), F32)] * 2,
        compiler_params=_params(("arbitrary",)),
        name="mix_sample",
    )(*args)


def _xattn_sample_kernel(qx_ref, mem_ref, o_ref):
    n_mem = mem_ref.shape[1]
    n = n_mem * 8
    q = qx_ref[0]
    order = [(half, h) for half in range(X_HEAD_DIM // LANE) for h in range(X_HEADS)]
    col = lambda half, h: slice(h * X_HEAD_DIM + half * LANE, h * X_HEAD_DIM + (half + 1) * LANE)
    qt = jnp.concatenate([q[:, col(half, h)] for half, h in order], axis=0)
    x = mem_ref[0]
    a = _dot_nt(qt, x[:, 0].reshape(n, LANE))
    rowi = lax.broadcasted_iota(I32, (8, n), 0)
    lane = lax.broadcasted_iota(I32, (8, n), 1)
    own = (lane & 7) == rowi
    a = jnp.where(own, a, 0.0)
    other = pltpu.roll(a, X_HEADS, axis=0)
    s = a + jnp.where(rowi < X_HEADS, pltpu.roll(other, n - X_HEADS, axis=1), pltpu.roll(other, X_HEADS, axis=1))
    s = jnp.where(own, s, -jnp.inf)
    e = jnp.exp(s - jnp.max(s, axis=-1, keepdims=True))
    p = e / jnp.sum(e, axis=-1, keepdims=True)
    o8 = _dot(p, x[:, 1].reshape(n, LANE))
    o_ref[0] = jnp.concatenate([o8[half * X_HEADS + h:half * X_HEADS + h + 1]
                                for h in range(X_HEADS) for half in range(X_HEAD_DIM // LANE)], axis=1)


def _xattn_sample(qx3, mem5):
    nb, _, xw = qx3.shape
    return pl.pallas_call(
        _xattn_sample_kernel, grid=(nb,),
        in_specs=[pl.BlockSpec((1, 1, xw), lambda i: (i, 0, 0)),
                  pl.BlockSpec((1,) + mem5.shape[1:], lambda i: (i, 0, 0, 0, 0))],
        out_specs=pl.BlockSpec((1, 1, xw), lambda i: (i, 0, 0)),
        out_shape=jax.ShapeDtypeStruct((nb, 1, xw), F32),
        compiler_params=_params(("arbitrary",)),
        name="xattn_sample",
    )(qx3, mem5)


def _resid_proj_kernel(h_ref, o_ref, w_ref, y_ref):
    y_ref[...] = h_ref[...] + _dot(o_ref[...], w_ref[...])


def _resid_proj(h, o, w):
    n, d = h.shape
    full = lambda a: pl.BlockSpec(a.shape, lambda i: (0,) * a.ndim)
    return pl.pallas_call(
        _resid_proj_kernel, grid=(1,),
        in_specs=[full(h), full(o), full(w)],
        out_specs=pl.BlockSpec((n, d), lambda i: (0, 0)),
        out_shape=jax.ShapeDtypeStruct((n, d), F32),
        compiler_params=_params(("arbitrary",)),
        name="resid_proj",
    )(h, o, w)


def _prep_weights(g_mix, w_in, w_pool, pool_scale, w_cmp, pe_cmp, w_out, g_xattn, g_mem, w_xq, w_xkv,
                  w_xo, g_ffn, w_up, conv_w, conv_b, w_down, g_final):
    o_kv = POOL_WIDTH + NSA_WIDTH
    o_gate = o_kv + N_BRANCH * KV_W
    w_tok = jnp.concatenate([w_in[:, :o_kv + KV_W + LANE], jnp.pad(w_in[:, o_gate:], ((0, 0), (0, GATE_PAD - N_GATE)))],
                            axis=1)
    r = CMP_LEN // CMP_STRIDE
    w5 = w_cmp.reshape(2, r, CMP_STRIDE, HEAD_DIM, HEAD_DIM)
    eye = jnp.eye(2, dtype=F32)
    wbig = jnp.einsum('cisde,gG->csgdiGe', w5, eye).reshape(2, CMP_STRIDE * LANE, r * LANE)
    wpe = jnp.einsum('cke,cC->ckCe', w_cmp.reshape(2, CMP_LEN * HEAD_DIM, HEAD_DIM), eye)
    wpe = jnp.broadcast_to(wpe[:, :, :, None, :], (2, CMP_LEN * HEAD_DIM, 2, KV_HEADS, HEAD_DIM))
    row = lambda v: v.reshape(1, -1)
    return dict(
        g_mix=row(g_mix), w_tok=w_tok.astype(BF16), w_feat=w_in[:, o_kv:o_gate].T.astype(BF16),
        w_pool=w_pool.astype(BF16), pool_scale=row(pool_scale),
        wbig=wbig.astype(BF16), wpe=wpe.reshape(2 * CMP_LEN * HEAD_DIM, KV_W).astype(BF16),
        pe8=jnp.broadcast_to(pe_cmp.reshape(1, -1), (8, 2 * CMP_LEN * HEAD_DIM)),
        w_out=w_out.astype(BF16), g_xattn=row(g_xattn), g_mem=row(g_mem), w_xq=w_xq.astype(BF16),
        w_xkv=w_xkv.astype(BF16), w_xo=w_xo.astype(BF16), g_ffn=row(g_ffn), w_up=w_up.astype(BF16),
        cw3=conv_w.reshape(3, 2, D_FF), cb2=conv_b.reshape(2, D_FF), w_down=w_down.astype(BF16),
        g_final=row(g_final))


def _token_major(a_t, lead):
    b, _, t = a_t.shape
    nd = len(lead)
    return jnp.transpose(a_t.reshape((b,) + lead + (t,)), (0, nd + 1) + tuple(range(1, nd + 1)))


def _feature_major(a, nfeat):
    nd = a.ndim
    return jnp.transpose(a, (0,) + tuple(range(2, nd)) + (1,)).reshape(a.shape[0], nfeat, a.shape[1])


def _prompt_layer(x, mem, w, *, tm=512, tm_x=1024, tf=D_FF, kc=(2048, 1024, 512)):
    b, t, d = x.shape
    n = b * t
    u, q, kvc_tok, ks_tok, gt, kv4t, kvwt = _mix_in(x, w['g_mix'], w['w_tok'], w['w_feat'], tm, "mix_in_prompt")
    r3 = lambda a: a.reshape(b, t, a.shape[-1])
    u, q, kvc_tok, ks_tok, gt = map(r3, (u, q, kvc_tok, ks_tok, gt))
    y_pool = _pool_prompt(u, w['w_pool'], w['pool_scale'], tm)
    kvc = _compress_prompt(kvc_tok, w['wbig'], w['pe8'], w['wpe'])
    y_nsa = _nsa_prompt(q, gt, kvc, ks_tok, kv4t, kvwt, kc)
    m = mem.shape[1]
    (mem_kv,) = _rms_proj(mem.reshape(b * m, d), w['g_mem'], w['w_xkv'], (2 * X_HEADS * X_HEAD_DIM,), m,
                          name="mem_project")
    mem_kv = mem_kv.reshape(b, m, 2 * X_HEADS * X_HEAD_DIM)
    h2 = _xattn_prompt(x, y_pool, y_nsa, w['w_out'], w['g_xattn'], w['w_xq'], mem_kv, w['w_xo'], tm_x)
    y, st = _ffn_prompt(h2.reshape(n, d), w['g_ffn'], w['w_up'], w['cw3'], w['cb2'], w['w_down'],
                        w['g_final'], b, tm, tf)
    wlen = min(WINDOW, t)
    return (y.reshape(b, t, d),
            _token_major(kv4t, (4, KV_HEADS, HEAD_DIM)),
            _token_major(kvwt[:, :, t - wlen:], (2, KV_HEADS, HEAD_DIM)),
            u[:, t - POOL_STATE:],
            jnp.swapaxes(st[t // tm - 1::t // tm, :, 6:8], 1, 2).reshape(b, 2, 2 * D_FF),
            mem_kv.reshape(b, m, 2, X_HEADS, X_HEAD_DIM))


def _sample_layer(x, cache_kv, page_table, cache_win, state_pool, state_ffn, cache_mem, w, *, tf=1408):
    nb, _, d = x.shape
    n_phys, page = cache_kv.shape[:2]
    n_pages = page_table.shape[1]
    past = n_pages * page
    u, q, _, _, gt, kv4t, kvwt = _mix_in(x.reshape(1, nb, d), w['g_mix'], w['w_tok'], w['w_feat'], nb, "mix_in_sample")
    y_pool = _pool_sample(jnp.swapaxes(state_pool, 0, 1), u, w['w_pool'], w['pool_scale'], past)
    cache3 = _feature_major(cache_kv, 2 * KV_W)
    pt_flat = page_table.reshape(-1)
    jq = past // SEL_BLOCK
    ns_pad = -(-(jq + 1) // LANE) * LANE
    k_past = min(N_SELECT, jq + 1) - 1
    q3 = q.reshape(nb, 1, NSA_WIDTH)
    oc, imp = _nsa_sample_cmp(pt_flat, q3, cache3, w['wbig'], w['pe8'], w['wpe'], n_pages, ns_pad)
    idx = _topk_sample(imp[:, :KV_HEADS].reshape(nb * KV_HEADS, ns_pad), jq, k_past)
    idx_flat = idx[:, :k_past].reshape(-1)
    o8, win_out = _nsa_sample_sel(idx_flat, pt_flat, q3, gt.reshape(nb, 1, GATE_PAD), oc,
                                  _feature_major(cache_win, KV_W), kv4t, kvwt, cache3, n_pages, k_past)
    y_nsa = jnp.concatenate([o8[:, :GROUP, :HEAD_DIM].reshape(nb, GROUP * HEAD_DIM),
                             o8[:, GROUP:, HEAD_DIM:].reshape(nb, GROUP * HEAD_DIM)], axis=1)
    h1, qx = _mix_sample(x.reshape(nb, d), y_pool, y_nsa, w['w_out'], w['g_xattn'], w['w_xq'])
    m = cache_mem.shape[1]
    halves = X_HEAD_DIM // LANE
    mem5 = jnp.transpose(cache_mem.reshape(nb, m, 2, X_HEADS, halves, LANE), (0, 1, 2, 4, 3, 5))
    ox = _xattn_sample(qx.reshape(nb, 1, d), mem5.reshape(nb, m, 2, halves * X_HEADS, LANE))
    h2 = _resid_proj(h1, ox.reshape(nb, d), w['w_xo'])
    y, up2 = _ffn_sample(h2, w['g_ffn'], w['w_up'], w['cw3'], w['cb2'], w['w_down'], w['g_final'],
                         jnp.swapaxes(state_ffn, 0, 1), tf)
    up = jnp.swapaxes(up2, 0, 1).reshape(nb, 1, 2 * D_FF)
    return (y.reshape(nb, 1, d),
            jnp.swapaxes(_token_major(kv4t, (4, KV_HEADS, HEAD_DIM)), 0, 1),
            _token_major(win_out, (2, KV_HEADS, HEAD_DIM)),
            jnp.concatenate([state_pool[:, 1:], u[:, None, :]], axis=1),
            jnp.concatenate([state_ffn[:, 1:], up], axis=1),
            )


def kernel(x_prompt, x_sample, mem_prompt, cache_kv, page_table, cache_win, state_pool, state_ffn, cache_mem, g_mix, w_in, w_pool, pool_scale, w_cmp, pe_cmp, w_out, g_xattn, g_mem, w_xq, w_xkv, w_xo, g_ffn, w_up, conv_w, conv_b, w_down, g_final):
    assert g_mix.shape[0] == 1, "single layer"
    w = _prep_weights(g_mix[0], w_in[0], w_pool[0], pool_scale[0], w_cmp[0], pe_cmp[0], w_out[0], g_xattn[0],
                      g_mem[0], w_xq[0], w_xkv[0], w_xo[0], g_ffn[0], w_up[0], conv_w[0], conv_b[0], w_down[0],
                      g_final)
    yp, kv_p, win_p, pool_p, ffn_p, mem_p = _prompt_layer(x_prompt, mem_prompt, w)
    ys, kv_s, win_s, pool_s, ffn_s = _sample_layer(x_sample, cache_kv[0], page_table, cache_win[0],
                                                    state_pool[0], state_ffn[0], cache_mem[0], w)
    lead = lambda a: a[None]
    return (yp, ys, lead(kv_p), lead(kv_s), lead(win_p), lead(win_s), lead(pool_p), lead(pool_s),
            lead(ffn_p), lead(ffn_s), lead(mem_p))
```

```python
import functools

import jax
import jax.numpy as jnp
from jax import lax
from jax.experimental import pallas as pl
from jax.experimental.pallas import tpu as pltpu

F32 = jnp.float32
BF16 = jnp.bfloat16
I32 = jnp.int32

D_MODEL = 1024
POOL_WIDTH = 512
POOL_WINDOWS = (2, 4, 8, 16)
POOL_GC = 128
POOL_STATE = 15
NSA_WIDTH = 512
HEAD_DIM = 64
N_HEADS = 8
KV_HEADS = 2
GROUP = 4
N_BRANCH = 3
CMP_LEN = 32
CMP_STRIDE = 16
SEL_BLOCK = 64
N_SELECT = 16
WINDOW = 512
QBLK = 128
KV_W = 2 * KV_HEADS * HEAD_DIM
N_GATE = N_HEADS * N_BRANCH
LANE = 128
GATE_PAD = LANE
X_HEADS = 4
X_HEAD_DIM = 256
D_FF = 2816
EPS = 1e-6
NEG = -1e30
VMEM_LIMIT = 56 * 1024 * 1024
CHUNK_PITCH = 24
PAGE_UNROLL = 16
SEL_TILES = 4

def _dot(a, b):
    return jnp.dot(a.astype(BF16), b.astype(BF16), preferred_element_type=F32)


def _dot_nt(a, b):
    return lax.dot_general(a.astype(BF16), b.astype(BF16), (((1,), (1,)), ((), ())),
                           preferred_element_type=F32)


def _rms(x, g):
    return x * lax.rsqrt(jnp.mean(x * x, axis=-1, keepdims=True) + EPS) * g


def _split3(x):
    p0 = x.astype(BF16)
    r = x - p0.astype(F32)
    p1 = r.astype(BF16)
    p2 = (r - p1.astype(F32)).astype(BF16)
    return p0, p1, p2


def _masked_softmax(s, mask):
    s = jnp.where(mask, s, -jnp.inf)
    m = jnp.max(s, axis=-1, keepdims=True)
    m = jnp.where(m == -jnp.inf, 0.0, m)
    e = jnp.exp(s - m)
    return e / jnp.maximum(jnp.sum(e, axis=-1, keepdims=True), 1e-30)


def _params(sem):
    return pltpu.CompilerParams(dimension_semantics=sem, vmem_limit_bytes=VMEM_LIMIT)


def _rms_proj_kernel(x_ref, g_ref, w_ref, *out_refs, widths, sigmoid_last):
    xn = _rms(x_ref[...], g_ref[...]).astype(BF16)
    off = 0
    for k, (o_ref, wd) in enumerate(zip(out_refs, widths)):
        y = jnp.dot(xn, w_ref[:, off:off + wd], preferred_element_type=F32)
        if sigmoid_last and k == len(widths) - 1:
            y = 1.0 / (1.0 + jnp.exp(-y))
        o_ref[...] = y
        off += wd


def _rms_proj(x, g, w, widths, tm, sigmoid_last=False, name="rms_proj"):
    n, d = x.shape
    ntot = w.shape[1]
    assert sum(widths) == ntot and n % tm == 0
    return pl.pallas_call(
        functools.partial(_rms_proj_kernel, widths=tuple(widths), sigmoid_last=sigmoid_last),
        grid=(n // tm,),
        in_specs=[pl.BlockSpec((tm, d), lambda i: (i, 0)),
                  pl.BlockSpec((1, d), lambda i: (0, 0)),
                  pl.BlockSpec((d, ntot), lambda i: (0, 0))],
        out_specs=[pl.BlockSpec((tm, wd), lambda i: (i, 0)) for wd in widths],
        out_shape=[jax.ShapeDtypeStruct((n, wd), F32) for wd in widths],
        compiler_params=_params(("arbitrary",)),
        name=name,
    )(x, g, w)


def _mix_in_kernel(x_ref, g_ref, wt_ref, wf_ref, u_ref, q_ref, kc_ref, ks_ref, gt_ref, kv4_ref, kvw_ref):
    xn = _rms(x_ref[...], g_ref[...]).astype(BF16)
    off = 0
    for o_ref in (u_ref, q_ref, kc_ref, ks_ref, gt_ref):
        wd = o_ref.shape[-1]
        y = jnp.dot(xn, wt_ref[:, off:off + wd], preferred_element_type=F32)
        o_ref[...] = 1.0 / (1.0 + jnp.exp(-y)) if o_ref is gt_ref else y
        off += wd
    kv_t = lax.dot_general(wf_ref[...], xn, (((1,), (1,)), ((), ())), preferred_element_type=F32)
    kv4_ref[0] = kv_t[0:2 * KV_W]
    kvw_ref[0] = kv_t[2 * KV_W:3 * KV_W]


def _mix_in(x3, g, w_tok, w_feat, tm, name):
    b, t, d = x3.shape
    n = b * t
    tps = t // tm
    widths = (POOL_WIDTH, NSA_WIDTH, KV_W, LANE, GATE_PAD)
    assert sum(widths) == w_tok.shape[1] and t % tm == 0
    return pl.pallas_call(
        _mix_in_kernel,
        grid=(n // tm,),
        in_specs=[pl.BlockSpec((tm, d), lambda i: (i, 0)),
                  pl.BlockSpec((1, d), lambda i: (0, 0)),
                  pl.BlockSpec(w_tok.shape, lambda i: (0, 0)),
                  pl.BlockSpec(w_feat.shape, lambda i: (0, 0))],
        out_specs=[pl.BlockSpec((tm, wd), lambda i: (i, 0)) for wd in widths]
        + [pl.BlockSpec((1, 2 * KV_W, tm), lambda i: (i // tps, 0, i % tps)),
           pl.BlockSpec((1, KV_W, tm), lambda i: (i // tps, 0, i % tps))],
        out_shape=[jax.ShapeDtypeStruct((n, wd), F32) for wd in widths]
        + [jax.ShapeDtypeStruct((b, 2 * KV_W, t), F32), jax.ShapeDtypeStruct((b, KV_W, t), F32)],
        compiler_params=_params(("arbitrary",)),
        name=name,
    )(x3.reshape(n, d), g, w_tok, w_feat)


def _pool_tail(s_list, u, cnts, w_ref, sc_ref, y_ref, lead):
    for gi in range(len(POOL_WINDOWS)):
        cols = slice(gi * POOL_GC, (gi + 1) * POOL_GC)
        d = s_list[gi] / cnts[gi] - u[:, cols]
        y = _dot(d, w_ref[gi]) * sc_ref[:, cols]
        if lead:
            y_ref[0, :, cols] = y
        else:
            y_ref[:, cols] = y


def _pool_prompt_kernel(u_ref, w_ref, sc_ref, y_ref, ext_ref, *, tm):
    t = pl.program_id(1)
    hist = 16

    @pl.when(t == 0)
    def _():
        ext_ref[0:hist, :] = jnp.zeros((hist, POOL_WIDTH), F32)

    @pl.when(t > 0)
    def _():
        ext_ref[0:hist, :] = ext_ref[tm:tm + hist, :]

    u = u_ref[0]
    ext_ref[hist:hist + tm, :] = u
    pos = t * tm + lax.broadcasted_iota(I32, (tm, 1), 0)
    s_list, cnts = [], []
    for gi, w in enumerate(POOL_WINDOWS):
        cols = slice(gi * POOL_GC, (gi + 1) * POOL_GC)
        s = u[:, cols]
        for k in range(1, w):
            s = s + ext_ref[hist - k:hist - k + tm, cols]
        s_list.append(s)
        cnts.append(jnp.minimum(pos + 1, w).astype(F32))
    _pool_tail(s_list, u, cnts, w_ref, sc_ref, y_ref, True)


def _pool_prompt(u3, w_pool, pool_scale, tm):
    b, t, _ = u3.shape
    return pl.pallas_call(
        functools.partial(_pool_prompt_kernel, tm=tm),
        grid=(b, t // tm),
        in_specs=[pl.BlockSpec((1, tm, POOL_WIDTH), lambda i, j: (i, j, 0)),
                  pl.BlockSpec((4, POOL_GC, POOL_GC), lambda i, j: (0, 0, 0)),
                  pl.BlockSpec((1, POOL_WIDTH), lambda i, j: (0, 0))],
        out_specs=pl.BlockSpec((1, tm, POOL_WIDTH), lambda i, j: (i, j, 0)),
        out_shape=jax.ShapeDtypeStruct((b, t, POOL_WIDTH), F32),
        scratch_shapes=[pltpu.VMEM((tm + 16, POOL_WIDTH), F32)],
        compiler_params=_params(("arbitrary", "arbitrary")),
        name="pool_prompt",
    )(u3, w_pool, pool_scale)


def _pool_sample_kernel(st_ref, u_ref, w_ref, sc_ref, y_ref, *, pos0):
    u = u_ref[...]
    s_list, cnts = [], []
    for gi, w in enumerate(POOL_WINDOWS):
        cols = slice(gi * POOL_GC, (gi + 1) * POOL_GC)
        s = u[:, cols]
        for k in range(1, w):
            s = s + st_ref[POOL_STATE - k, :, cols]
        s_list.append(s)
        cnts.append(float(min(pos0 + 1, w)))
    _pool_tail(s_list, u, cnts, w_ref, sc_ref, y_ref, False)


def _pool_sample(st_t, u, w_pool, pool_scale, pos0):
    n = u.shape[0]
    return pl.pallas_call(
        functools.partial(_pool_sample_kernel, pos0=pos0),
        grid=(1,),
        in_specs=[pl.BlockSpec((POOL_STATE, n, POOL_WIDTH), lambda i: (0, 0, 0)),
                  pl.BlockSpec((n, POOL_WIDTH), lambda i: (0, 0)),
                  pl.BlockSpec((4, POOL_GC, POOL_GC), lambda i: (0, 0, 0)),
                  pl.BlockSpec((1, POOL_WIDTH), lambda i: (0, 0))],
        out_specs=pl.BlockSpec((n, POOL_WIDTH), lambda i: (0, 0)),
        out_shape=jax.ShapeDtypeStruct((n, POOL_WIDTH), F32),
        compiler_params=_params(("arbitrary",)),
        name="pool_sample",
    )(st_t, u, w_pool, pool_scale)


def _compress_rows(load_half, wbig_ref, pe_ref, wpe_ref, n_chunk):
    pe_row = _dot(pe_ref[...], wpe_ref[...])[0:1]
    halves = []
    for c in range(2):
        x = jnp.concatenate([load_half(c, s).astype(BF16) for s in range(CMP_STRIDE)], axis=1)
        y = jnp.dot(x, wbig_ref[c], preferred_element_type=F32)
        halves.append(y[:, :LANE] + pltpu.roll(y[:, LANE:], n_chunk - 1, axis=0))
    return jnp.concatenate(halves, axis=1) + pe_row


def _compress_prompt_kernel(k_ref, v_ref, wbig_ref, pe_ref, wpe_ref, out_ref, *, n_chunk):
    load = lambda c, s: (k_ref, v_ref)[c][0, pl.ds(s, n_chunk, stride=CMP_STRIDE), :]
    out_ref[0] = _compress_rows(load, wbig_ref, pe_ref, wpe_ref, n_chunk)


def _compress_prompt(kv4, wbig, pe8, wpe):
    b, t, _ = kv4.shape
    n_chunk = t // CMP_STRIDE
    return pl.pallas_call(
        functools.partial(_compress_prompt_kernel, n_chunk=n_chunk),
        grid=(b,),
        in_specs=[pl.BlockSpec((1, t, LANE), lambda i: (i, 0, 0)),
                  pl.BlockSpec((1, t, LANE), lambda i: (i, 0, 1)),
                  pl.BlockSpec(wbig.shape, lambda i: (0, 0, 0)),
                  pl.BlockSpec(pe8.shape, lambda i: (0, 0)),
                  pl.BlockSpec(wpe.shape, lambda i: (0, 0))],
        out_specs=pl.BlockSpec((1, n_chunk, KV_W), lambda i: (i, 0, 0)),
        out_shape=jax.ShapeDtypeStruct((b, n_chunk, KV_W), F32),
        compiler_params=_params(("arbitrary",)),
        name="compress_prompt",
    )(kv4, kv4, wbig, pe8, wpe)


def _stack_group_queries(q, g):
    lane = lax.broadcasted_iota(I32, (q.shape[0], LANE), 1)
    parts = []
    for hh in range(GROUP):
        h = GROUP * g + hh
        ch = q[:, (h // 2) * LANE:(h // 2 + 1) * LANE]
        if h % 2 != g:
            ch = pltpu.roll(ch, HEAD_DIM, axis=1)
        parts.append(jnp.where((lane >> 6) == g, ch, 0.0))
    return jnp.concatenate(parts, axis=0).astype(BF16)


def _place_heads(o_g, weights, g, outs):
    lane = lax.broadcasted_iota(I32, (QBLK, LANE), 1)
    for hh in range(GROUP):
        h = GROUP * g + hh
        rs = slice(hh * QBLK, (hh + 1) * QBLK)
        piece = sum(w_[:, 0:1] * o_[rs] for w_, o_ in zip(weights(h), o_g))
        if h % 2 != g:
            piece = pltpu.roll(piece, HEAD_DIM, axis=1)
        piece = jnp.where((lane >> 6) == (h % 2), piece, 0.0)
        outs[h // 2] = piece if outs[h // 2] is None else outs[h // 2] + piece


def _nsa_dense_kernel(q_ref, gt_ref, kvc_ref, kw_ref, vw_ref, part_ref, sel_ref, *, t_len):
    i = pl.program_id(1)
    t0 = i * QBLK
    nc = t_len // CMP_STRIDE
    ns = t_len // SEL_BLOCK
    wl = min(WINDOW + QBLK, t_len)
    q = q_ref[0] * (HEAD_DIM ** -0.5)
    gt = gt_ref[0]
    pos_q = t0 + lax.broadcasted_iota(I32, (QBLK, 1), 0)

    oj = lax.broadcasted_iota(I32, (ns, nc), 0)
    on = lax.broadcasted_iota(I32, (ns, nc), 1)
    ovl_t = jnp.where((on * CMP_STRIDE + CMP_LEN - 1 >= oj * SEL_BLOCK)
                      & (on * CMP_STRIDE <= oj * SEL_BLOCK + SEL_BLOCK - 1), 1.0, 0.0).astype(BF16)
    jb = lax.broadcasted_iota(I32, (ns, QBLK), 0)
    jq = (t0 + lax.broadcasted_iota(I32, (ns, QBLK), 1)) >> 6
    forced = (jb == 0) | (jb == jq) | (jb == jq - 1)

    kc_all = kvc_ref[0, :, 0:LANE].astype(BF16)
    vc_all = kvc_ref[0, :, LANE:2 * LANE].astype(BF16)
    ws = pl.multiple_of(jnp.maximum(t0 - WINDOW, 0), QBLK)
    kwc = kw_ref[0, :, pl.ds(ws, wl)].astype(BF16)
    vwc = vw_ref[0, :, pl.ds(ws, wl)].astype(BF16)

    ncol = lax.broadcasted_iota(I32, (1, nc), 1)
    bias_c = jnp.where((ncol * CMP_STRIDE + CMP_LEN - 1) <= pos_q, 0.0, NEG)
    bias_c = jnp.concatenate([bias_c] * GROUP, axis=0)
    dpos = pos_q - (ws + lax.broadcasted_iota(I32, (1, wl), 1))
    bias_w = jnp.where((dpos >= 0) & (dpos <= WINDOW), 0.0, NEG)
    bias_w = jnp.concatenate([bias_w] * GROUP, axis=0)

    def softmax_parts(s):
        m = jnp.max(s, axis=-1, keepdims=True)
        e = jnp.exp(s - m)
        return e, jnp.where(m > 0.5 * NEG, 1.0 / jnp.sum(e, axis=-1, keepdims=True), 0.0)

    outs = [None] * (NSA_WIDTH // LANE)
    for g in range(KV_HEADS):
        qg = _stack_group_queries(q, g)

        e_c, inv_c = softmax_parts(_dot_nt(qg, kc_all) + bias_c)
        o_c = _dot(e_c, vc_all) * inv_c

        p_c = e_c * inv_c
        psum = p_c[0:QBLK] + p_c[QBLK:2 * QBLK] + p_c[2 * QBLK:3 * QBLK] + p_c[3 * QBLK:4 * QBLK]
        imp_t = sum(lax.dot_general(ovl_t, part, (((1,), (1,)), ((), ())), preferred_element_type=F32)
                    for part in _split3(psum))
        x = jnp.where(forced, jnp.inf, jnp.where(jb > jq, -jnp.inf, imp_t))
        cnt = jnp.zeros((ns, QBLK), I32)
        for jp in range(ns):
            row = x[jp:jp + 1, :]
            beats = (row > x) | ((row == x) & (jb > jp))
            cnt = cnt + jnp.where(beats, 1, 0)
        sel_ref[0, 0, g * ns:(g + 1) * ns, :] = jnp.where(cnt < N_SELECT, 1.0, 0.0)

        e_w, inv_w = softmax_parts(_dot(qg, kwc) + bias_w)
        o_w = _dot_nt(e_w, vwc) * inv_w

        _place_heads((o_c, o_w), lambda h: (gt[:, 3 * h:3 * h + 1], gt[:, 3 * h + 2:3 * h + 3]), g, outs)
    part_ref[0] = jnp.concatenate(outs, axis=1)


def _nsa_select_kernel(q_ref, gt_ref, sel_ref, part_ref, ks_ref, vs_ref, o_ref, *, t_len, kc):
    i = pl.program_id(1)
    nqry = SEL_TILES * QBLK
    t0 = i * nqry
    ns = t_len // SEL_BLOCK
    rows = GROUP * nqry
    kcs = kc
    n_small = (t0 + nqry + kcs[-1] - 1) // kcs[-1]
    gt_t = gt_ref[0].T
    blocks = []
    q_t = (q_ref[0] * (HEAD_DIM ** -0.5)).T
    zero_half = jnp.zeros((HEAD_DIM, nqry), F32)
    for g in range(KV_HEADS):
        heads = []
        for hh in range(GROUP):
            q_h = q_t[(GROUP * g + hh) * HEAD_DIM:(GROUP * g + hh + 1) * HEAD_DIM]
            heads.append(jnp.concatenate([q_h, zero_half] if g == 0 else [zero_half, q_h], axis=0))
        qg_t = jnp.concatenate(heads, axis=1).astype(BF16)

        def sel_chunk(kcz, base, c, carry):
            m_prev, l_prev, acc = carry
            k0 = pl.multiple_of(base + c * kcz, kcs[-1])
            nblk = kcz // SEL_BLOCK
            s = _dot(ks_ref[0, pl.ds(k0, kcz), :], qg_t)
            j0 = pl.multiple_of(g * ns + k0 // SEL_BLOCK, kcs[-1] // SEL_BLOCK)
            chosen = []
            for tile in range(SEL_TILES):
                rows_j = sel_ref[0, tile, pl.ds(j0, nblk), :]
                chosen.append(jnp.concatenate([jnp.broadcast_to(rows_j[j:j + 1], (SEL_BLOCK, QBLK))
                                               for j in range(nblk)], axis=0))
            chosen = jnp.concatenate(chosen, axis=1)
            key_pos = k0 + lax.broadcasted_iota(I32, (kcz, nqry), 0)
            qry_pos = t0 + lax.broadcasted_iota(I32, (kcz, nqry), 1)
            bias = jnp.where((chosen > 0.5) & (key_pos <= qry_pos), 0.0, NEG)
            s = s + jnp.concatenate([bias] * GROUP, axis=1)
            m_new = jnp.maximum(m_prev, jnp.max(s, axis=0, keepdims=True))
            alpha = jnp.exp(m_prev - m_new)
            p = jnp.exp(s - m_new)
            l_new = alpha * l_prev + jnp.sum(p, axis=0, keepdims=True)
            return m_new, l_new, alpha * acc + _dot(vs_ref[0, :, pl.ds(k0, kcz)], p)

        carry = (jnp.full((1, rows), NEG, F32), jnp.zeros((1, rows), F32), jnp.zeros((LANE, rows), F32))
        done = 0
        for kcz in kcs:
            count = (n_small * kcs[-1] - done) // kcz
            carry = lax.fori_loop(0, count, functools.partial(sel_chunk, kcz, done), carry)
            done = done + count * kcz
        _, l_s, acc_s = carry
        o_t = acc_s[g * HEAD_DIM:(g + 1) * HEAD_DIM] / l_s
        for hh in range(GROUP):
            h = GROUP * g + hh
            blocks.append(o_t[:, hh * nqry:(hh + 1) * nqry] * gt_t[3 * h + 1:3 * h + 2])
    o_ref[0] = part_ref[0] + jnp.concatenate(blocks, axis=0).T


def _nsa_prompt(q3, gt3, kvc, ks_tok, kv4t, kvwt, kc):
    b, t, _ = q3.shape
    kc = tuple(k for k in kc if k <= t)
    assert all(t % k == 0 for k in kc) and t % QBLK == 0
    nq = t // QBLK
    ns = t // SEL_BLOCK
    slab = lambda c: pl.BlockSpec((1, LANE, t), lambda i, j: (i, c, 0))
    tile = lambda w_: pl.BlockSpec((1, QBLK, w_), lambda i, j: (i, j, 0))
    sel_spec = pl.BlockSpec((1, 1, KV_HEADS * ns, QBLK), lambda i, j: (i, j, 0, 0))
    part, sel = pl.pallas_call(
        functools.partial(_nsa_dense_kernel, t_len=t),
        grid=(b, nq),
        in_specs=[tile(NSA_WIDTH), tile(GATE_PAD),
                  pl.BlockSpec((1, t // CMP_STRIDE, KV_W), lambda i, j: (i, 0, 0)),
                  slab(0), slab(1)],
        out_specs=[tile(NSA_WIDTH), sel_spec],
        out_shape=[jax.ShapeDtypeStruct((b, t, NSA_WIDTH), F32),
                   jax.ShapeDtypeStruct((b, nq, KV_HEADS * ns, QBLK), F32)],
        compiler_params=_params(("arbitrary", "arbitrary")),
        name="nsa_dense",
    )(q3, gt3, kvc, kvwt, kvwt)
    assert nq % SEL_TILES == 0
    wide = lambda w_: pl.BlockSpec((1, SEL_TILES * QBLK, w_), lambda i, j: (i, j, 0))
    return pl.pallas_call(
        functools.partial(_nsa_select_kernel, t_len=t, kc=kc),
        grid=(b, nq // SEL_TILES),
        in_specs=[wide(NSA_WIDTH), wide(GATE_PAD),
                  pl.BlockSpec((1, SEL_TILES, KV_HEADS * ns, QBLK), lambda i, j: (i, j, 0, 0)), wide(NSA_WIDTH),
                  pl.BlockSpec((1, t, LANE), lambda i, j: (i, 0, 0)), slab(3)],
        out_specs=wide(NSA_WIDTH),
        out_shape=jax.ShapeDtypeStruct((b, t, NSA_WIDTH), F32),
        compiler_params=_params(("arbitrary", "arbitrary")),
        name="nsa_select",
    )(q3, gt3, sel, part, ks_tok, kv4t)


def _xattn_prompt_kernel(x_ref, yp_ref, yn_ref, wo_ref, g_ref, wq_ref, mem_ref, wxo_ref, o_ref):
    y = jnp.concatenate([yp_ref[0], yn_ref[0]], axis=1)
    h1 = x_ref[0] + _dot(y, wo_ref[...])
    qx = _dot(_rms(h1, g_ref[...]), wq_ref[...]) * (X_HEAD_DIM ** -0.5)
    xw = X_HEADS * X_HEAD_DIM
    outs = []
    for h in range(X_HEADS):
        cs = slice(h * X_HEAD_DIM, (h + 1) * X_HEAD_DIM)
        s = _dot_nt(qx[:, cs], mem_ref[0, :, cs])
        e = jnp.exp(s - jnp.max(s, axis=-1, keepdims=True))
        p = e / jnp.sum(e, axis=-1, keepdims=True)
        outs.append(_dot(p, mem_ref[0, :, xw + h * X_HEAD_DIM:xw + (h + 1) * X_HEAD_DIM]))
    o_ref[0] = h1 + _dot(jnp.concatenate(outs, axis=1), wxo_ref[...])


def _xattn_prompt(x3, yp, yn, w_out, g_x, w_xq, mem_kv, w_xo, tm):
    b, t, d = x3.shape
    m = mem_kv.shape[1]
    full = lambda a: pl.BlockSpec(a.shape, lambda i, j: (0,) * a.ndim)
    return pl.pallas_call(
        _xattn_prompt_kernel,
        grid=(b, t // tm),
        in_specs=[pl.BlockSpec((1, tm, d), lambda i, j: (i, j, 0)),
                  pl.BlockSpec((1, tm, POOL_WIDTH), lambda i, j: (i, j, 0)),
                  pl.BlockSpec((1, tm, NSA_WIDTH), lambda i, j: (i, j, 0)),
                  full(w_out), full(g_x), full(w_xq),
                  pl.BlockSpec((1, m, mem_kv.shape[2]), lambda i, j: (i, 0, 0)),
                  full(w_xo)],
        out_specs=pl.BlockSpec((1, tm, d), lambda i, j: (i, j, 0)),
        out_shape=jax.ShapeDtypeStruct((b, t, d), F32),
        compiler_params=_params(("arbitrary", "arbitrary")),
        name="xattn_prompt",
    )(x3, yp, yn, w_out, g_x, w_xq, mem_kv, w_xo)


def _ffn_core(j, nj, h_ref, gf_ref, wg_ref, wv_ref, cw_ref, cb_ref, wd_ref, gfin_ref, y_ref,
              xn_ref, acc_ref, prev_rows):
    @pl.when(j == 0)
    def _():
        xn_ref[...] = _rms(h_ref[...], gf_ref[...]).astype(BF16)
        acc_ref[...] = jnp.zeros(acc_ref.shape, F32)

    xn = xn_ref[...]
    up_g = jnp.dot(xn, wg_ref[...], preferred_element_type=F32)
    up_v = jnp.dot(xn, wv_ref[...], preferred_element_type=F32)
    g1, g2, v1, v2 = prev_rows(up_g, up_v)
    cg = cb_ref[0:1, :] + cw_ref[0, 0:1, :] * g2 + cw_ref[1, 0:1, :] * g1 + cw_ref[2, 0:1, :] * up_g
    cv = cb_ref[1:2, :] + cw_ref[0, 1:2, :] * v2 + cw_ref[1, 1:2, :] * v1 + cw_ref[2, 1:2, :] * up_v
    act = cg / (1.0 + jnp.exp(-cg)) * cv
    acc_ref[...] += _dot(act, wd_ref[...])

    @pl.when(j == nj - 1)
    def _():
        y_ref[...] = _rms(h_ref[...] + acc_ref[...], gfin_ref[...])

    return up_g, up_v


def _ffn_prompt_kernel(h_ref, gf_ref, wg_ref, wv_ref, cw_ref, cb_ref, wd_ref, gfin_ref,
                       y_ref, st_ref, xn_ref, acc_ref, carry_ref, *, tiles_per_seq, nj, tm):
    r = pl.program_id(0)
    j = pl.program_id(1)
    row = lax.broadcasted_iota(I32, (tm, 1), 0)

    @pl.when((r % tiles_per_seq) == 0)
    def _():
        carry_ref[j] = jnp.zeros(carry_ref.shape[1:], F32)

    def prev_rows(up_g, up_v):
        res = []
        for half, up in enumerate((up_g, up_v)):
            prev = carry_ref[j, half]
            m1 = jnp.where(row == 0, prev[7:8], pltpu.roll(up, 1, axis=0))
            m2 = jnp.where(row == 0, prev[6:7], jnp.where(row == 1, prev[7:8], pltpu.roll(up, 2, axis=0)))
            res += [m1, m2]
        return res

    up_g, up_v = _ffn_core(j, nj, h_ref, gf_ref, wg_ref, wv_ref, cw_ref, cb_ref, wd_ref, gfin_ref,
                           y_ref, xn_ref, acc_ref, prev_rows)
    for half, up in enumerate((up_g, up_v)):
        carry_ref[j, half] = up[tm - 8:tm]
        st_ref[0, half] = up[tm - 8:tm]


def _ffn_specs(d, tf, nj):
    return [pl.BlockSpec((1, d), lambda r, j: (0, 0)),
            pl.BlockSpec((d, tf), lambda r, j: (0, j)),
            pl.BlockSpec((d, tf), lambda r, j: (0, nj + j)),
            pl.BlockSpec((3, 2, tf), lambda r, j: (0, 0, j)),
            pl.BlockSpec((2, tf), lambda r, j: (0, j)),
            pl.BlockSpec((tf, d), lambda r, j: (j, 0)),
            pl.BlockSpec((1, d), lambda r, j: (0, 0))]


def _ffn_prompt(h2, g_ffn, w_up, cw3, cb2, w_down, g_final, batch, tm, tf):
    n, d = h2.shape
    nj = D_FF // tf
    tps = n // batch // tm
    return pl.pallas_call(
        functools.partial(_ffn_prompt_kernel, tiles_per_seq=tps, nj=nj, tm=tm),
        grid=(n // tm, nj),
        in_specs=[pl.BlockSpec((tm, d), lambda r, j: (r, 0))] + _ffn_specs(d, tf, nj),
        out_specs=[pl.BlockSpec((tm, d), lambda r, j: (r, 0)),
                   pl.BlockSpec((1, 2, 8, tf), lambda r, j: (r, 0, 0, j))],
        out_shape=[jax.ShapeDtypeStruct((n, d), F32),
                   jax.ShapeDtypeStruct((n // tm, 2, 8, D_FF), F32)],
        scratch_shapes=[pltpu.VMEM((tm, d), BF16), pltpu.VMEM((tm, d), F32),
                        pltpu.VMEM((nj, 2, 8, tf), F32)],
        compiler_params=_params(("arbitrary", "arbitrary")),
        name="ffn_prompt",
    )(h2, g_ffn, w_up, w_up, cw3, cb2, w_down, g_final)


def _ffn_sample_kernel(h_ref, gf_ref, wg_ref, wv_ref, cw_ref, cb_ref, wd_ref, gfin_ref, sg_ref, sv_ref,
                       y_ref, up_ref, xn_ref, acc_ref, *, nj):
    j = pl.program_id(1)
    prev_rows = lambda up_g, up_v: (sg_ref[1], sg_ref[0], sv_ref[1], sv_ref[0])
    up_g, up_v = _ffn_core(j, nj, h_ref, gf_ref, wg_ref, wv_ref, cw_ref, cb_ref, wd_ref, gfin_ref,
                           y_ref, xn_ref, acc_ref, prev_rows)
    up_ref[0] = up_g
    up_ref[1] = up_v


def _ffn_sample(h2, g_ffn, w_up, cw3, cb2, w_down, g_final, st_t, tf):
    n, d = h2.shape
    nj = D_FF // tf
    return pl.pallas_call(
        functools.partial(_ffn_sample_kernel, nj=nj),
        grid=(1, nj),
        in_specs=[pl.BlockSpec((n, d), lambda r, j: (0, 0))] + _ffn_specs(d, tf, nj)
        + [pl.BlockSpec((2, n, tf), lambda r, j: (0, 0, j)),
           pl.BlockSpec((2, n, tf), lambda r, j: (0, 0, nj + j))],
        out_specs=[pl.BlockSpec((n, d), lambda r, j: (0, 0)),
                   pl.BlockSpec((2, n, tf), lambda r, j: (0, 0, j))],
        out_shape=[jax.ShapeDtypeStruct((n, d), F32),
                   jax.ShapeDtypeStruct((2, n, D_FF), F32)],
        scratch_shapes=[pltpu.VMEM((n, d), BF16), pltpu.VMEM((n, d), F32)],
        compiler_params=_params(("arbitrary", "arbitrary")),
        name="ffn_sample",
    )(h2, g_ffn, w_up, w_up, cw3, cb2, w_down, g_final, st_t, st_t)


def _head_rows(qrow):
    rowi = lax.broadcasted_iota(I32, (N_HEADS, LANE), 0)
    lane = lax.broadcasted_iota(I32, (N_HEADS, LANE), 1)
    x = jnp.zeros((N_HEADS, LANE), F32)
    for c in range(NSA_WIDTH // LANE):
        x = jnp.where((rowi >> 1) == c, jnp.broadcast_to(qrow[:, c * LANE:(c + 1) * LANE], (N_HEADS, LANE)), x)
    x = jnp.where((rowi & 1) != (rowi >> 2), pltpu.roll(x, HEAD_DIM, axis=1), x)
    return jnp.where((lane >> 6) == (rowi >> 2), x, 0.0)


def _page_copy(cache_ref, raw_ref, sem_ref, phys, p, slot):
    return pltpu.make_async_copy(cache_ref.at[phys, pl.ds(0, KV_W), :], raw_ref.at[slot, p], sem_ref.at[slot, p])


def _nsa_sample_cmp_kernel(pt_ref, q_ref, cache_ref, wbig_ref, pe_ref, wpe_ref, oc_ref, imp_ref,
                           raw_ref, tok_ref, sem_ref, *, n_pages, n_batch, page):
    b = pl.program_id(0)
    slot = b % 2
    n_tok = n_pages * page
    n_chunk = n_tok // CMP_STRIDE

    def start_all(bb, sl):
        def body(p, _):
            _page_copy(cache_ref, raw_ref, sem_ref, pt_ref[bb * n_pages + p], p, sl).start()
            return 0
        lax.fori_loop(0, n_pages, body, 0)

    @pl.when(b == 0)
    def _():
        start_all(0, 0)

    @pl.when(b + 1 < n_batch)
    def _():
        start_all(b + 1, 1 - slot)

    chunks_per_page = page // CMP_STRIDE

    def to_token_major(pg, _):
        for pp in range(PAGE_UNROLL):
            _page_copy(cache_ref, raw_ref, sem_ref, 0, pg * PAGE_UNROLL + pp, slot).wait()
        for pp in range(PAGE_UNROLL):
            p = pg * PAGE_UNROLL + pp
            xt = raw_ref[slot, p]
            for half in range(2):
                x = xt[half * LANE:(half + 1) * LANE].T
                for nl in range(chunks_per_page):
                    r0 = pl.multiple_of((p * chunks_per_page + nl) * CHUNK_PITCH, 8)
                    tok_ref[half, pl.ds(r0, CMP_STRIDE), :] = x[nl * CMP_STRIDE:(nl + 1) * CMP_STRIDE]
        return 0
    lax.fori_loop(0, n_pages // PAGE_UNROLL, to_token_major, 0)

    load = lambda c, s: tok_ref[c, pl.ds(s, n_chunk, stride=CHUNK_PITCH), :]
    kvc = _compress_rows(load, wbig_ref, pe_ref, wpe_ref, n_chunk)
    q8 = _head_rows(q_ref[0] * (HEAD_DIM ** -0.5))
    s = _dot_nt(q8, kvc[:, 0:LANE])
    ncol = lax.broadcasted_iota(I32, (1, n_chunk), 1)
    p = _masked_softmax(s, (ncol * CMP_STRIDE + CMP_LEN - 1) <= n_tok)
    oc_ref[0] = _dot(p, kvc[:, LANE:2 * LANE])
    rowi = lax.broadcasted_iota(I32, (N_HEADS, n_chunk), 0)
    psum = jnp.where(rowi == 0, jnp.sum(p[0:GROUP], axis=0, keepdims=True),
                     jnp.where(rowi == 1, jnp.sum(p[GROUP:2 * GROUP], axis=0, keepdims=True), 0.0))
    ns_pad = imp_ref.shape[2]
    on = lax.broadcasted_iota(I32, (n_chunk, ns_pad), 0)
    oj = lax.broadcasted_iota(I32, (n_chunk, ns_pad), 1)
    ovl = jnp.where((on * CMP_STRIDE + CMP_LEN - 1 >= oj * SEL_BLOCK)
                    & (on * CMP_STRIDE <= oj * SEL_BLOCK + SEL_BLOCK - 1), 1.0, 0.0).astype(BF16)
    imp_ref[0] = sum(jnp.dot(part, ovl, preferred_element_type=F32) for part in _split3(psum))


def _nsa_sample_cmp(pt_flat, q3, cache3, wbig, pe8, wpe, n_pages, ns_pad):
    nb = q3.shape[0]
    page = cache3.shape[2]
    assert page == LANE
    n_tok = n_pages * page
    gs = pltpu.PrefetchScalarGridSpec(
        num_scalar_prefetch=1, grid=(nb,),
        in_specs=[pl.BlockSpec((1, 1, NSA_WIDTH), lambda i, pt: (i, 0, 0)),
                  pl.BlockSpec(memory_space=pl.ANY),
                  pl.BlockSpec(wbig.shape, lambda i, pt: (0, 0, 0)),
                  pl.BlockSpec(pe8.shape, lambda i, pt: (0, 0)),
                  pl.BlockSpec(wpe.shape, lambda i, pt: (0, 0))],
        out_specs=[pl.BlockSpec((1, N_HEADS, LANE), lambda i, pt: (i, 0, 0)),
                   pl.BlockSpec((1, N_HEADS, ns_pad), lambda i, pt: (i, 0, 0))],
        scratch_shapes=[pltpu.VMEM((2, n_pages, KV_W, page), F32), pltpu.VMEM((2, n_tok // CMP_STRIDE * CHUNK_PITCH, LANE), F32),
                        pltpu.SemaphoreType.DMA((2, n_pages))])
    return pl.pallas_call(
        functools.partial(_nsa_sample_cmp_kernel, n_pages=n_pages, n_batch=nb, page=page),
        grid_spec=gs,
        out_shape=[jax.ShapeDtypeStruct((nb, N_HEADS, LANE), F32),
                   jax.ShapeDtypeStruct((nb, N_HEADS, ns_pad), F32)],
        compiler_params=_params(("arbitrary",)),
        name="nsa_sample_cmp",
    )(pt_flat, q3, cache3, wbig, pe8, wpe)


def _topk_sample_kernel(imp_ref, idx_ref, *, jq, k_past):
    x = imp_ref[...]
    col = lax.broadcasted_iota(I32, x.shape, 1)
    x = jnp.where((col == 0) | (col == jq - 1), jnp.inf, jnp.where(col >= jq, -jnp.inf, x))
    lane = lax.broadcasted_iota(I32, idx_ref.shape, 1)
    out = jnp.zeros(idx_ref.shape, I32)
    big = jnp.int32(1 << 30)
    for k in range(k_past):
        m = jnp.max(x, axis=-1, keepdims=True)
        idx = jnp.min(jnp.where(x == m, col, big), axis=-1, keepdims=True)
        out = jnp.where(lane == k, idx, out)
        x = jnp.where(col == idx, -jnp.inf, x)
    idx_ref[...] = out


def _topk_sample(imp2, jq, k_past):
    n, w = imp2.shape
    return pl.pallas_call(
        functools.partial(_topk_sample_kernel, jq=jq, k_past=k_past),
        grid=(1,),
        in_specs=[pl.BlockSpec((n, w), lambda i: (0, 0))],
        out_specs=pl.BlockSpec((n, LANE), lambda i: (0, 0)),
        out_shape=jax.ShapeDtypeStruct((n, LANE), I32),
        compiler_params=_params(("arbitrary",)),
        name="topk_sample",
    )(imp2)


def _blk_copy(cache_ref, buf_ref, sem_ref, phys, kv, g, r, slot):
    row0 = KV_W + kv * LANE + g * HEAD_DIM
    return pltpu.make_async_copy(cache_ref.at[phys, pl.ds(row0, HEAD_DIM), :],
                                 buf_ref.at[slot, kv, pl.ds(g * HEAD_DIM, HEAD_DIM), pl.ds(r * LANE, LANE)],
                                 sem_ref.at[slot])


def _nsa_sample_sel_kernel(idx_ref, pt_ref, q_ref, gt_ref, oc_ref, win_ref, kvn4_ref, kvnw_ref, cache_ref,
                           o_ref, wout_ref, buf_ref, sem_ref, *, n_pages, n_batch, page, k_past):
    b = pl.program_id(0)
    slot = b % 2
    per_page = page // SEL_BLOCK
    nl = k_past * page

    def for_blocks(bb, fn):
        def body(r, _):
            for g in range(KV_HEADS):
                j = idx_ref[(bb * KV_HEADS + g) * k_past + r]
                phys = pt_ref[bb * n_pages + j // per_page]
                for kv in range(2):
                    fn(phys, kv, g, r)
            return 0
        lax.fori_loop(0, k_past, body, 0)

    def start_all(bb, sl):
        for_blocks(bb, lambda phys, kv, g, r: _blk_copy(cache_ref, buf_ref, sem_ref, phys, kv, g, r, sl).start())

    @pl.when(b == 0)
    def _():
        start_all(0, 0)

    @pl.when(b + 1 < n_batch)
    def _():
        start_all(b + 1, 1 - slot)

    def wait_body(r, _):
        for g in range(KV_HEADS):
            for kv in range(2):
                _blk_copy(cache_ref, buf_ref, sem_ref, 0, kv, g, r, slot).wait()
        return 0
    lax.fori_loop(0, k_past, wait_body, 0)

    q8 = _head_rows(q_ref[0] * (HEAD_DIM ** -0.5))
    rowi = lax.broadcasted_iota(I32, (N_HEADS, 1), 0)
    lane = lax.broadcasted_iota(I32, (N_HEADS, LANE), 1)
    own = (lane >> 6) == (rowi >> 2)
    mine = jnp.where(lax.broadcasted_iota(I32, (N_HEADS, n_batch), 1) == b, 0.0, -jnp.inf)

    def attend(k_t, v_t, bias):
        s = _dot(q8, k_t) + bias
        e = jnp.exp(s - jnp.max(s, axis=-1, keepdims=True))
        return _dot_nt(e, v_t) / jnp.sum(e, axis=-1, keepdims=True)

    lane_s = lax.broadcasted_iota(I32, (1, nl), 1)
    halves = []
    for g in range(KV_HEADS):
        hv = jnp.zeros((1, nl), I32)
        for r in range(k_past):
            hv = jnp.where((lane_s // page) == r, idx_ref[(b * KV_HEADS + g) * k_past + r] % per_page, hv)
        halves.append(hv)
    vis = ((lane_s % page) // SEL_BLOCK) == jnp.where(rowi < GROUP, halves[0], halves[1])
    o_s = attend(jnp.concatenate([buf_ref[slot, 0], kvn4_ref[0, KV_W:KV_W + LANE]], axis=1),
                 jnp.concatenate([buf_ref[slot, 1], kvn4_ref[0, KV_W + LANE:2 * KV_W]], axis=1),
                 jnp.concatenate([jnp.where(vis, 0.0, -jnp.inf), mine], axis=1))

    cw = win_ref[0]
    wlen = cw.shape[1]
    o_w = attend(jnp.concatenate([cw[0:LANE], kvnw_ref[0, 0:LANE]], axis=1),
                 jnp.concatenate([cw[LANE:2 * LANE], kvnw_ref[0, LANE:2 * LANE]], axis=1),
                 jnp.concatenate([jnp.zeros((N_HEADS, wlen), F32), mine], axis=1))
    lane_n = lax.broadcasted_iota(I32, (KV_W, n_batch), 1)
    new_col = jnp.sum(jnp.where(lane_n == b, kvnw_ref[0], 0.0), axis=1, keepdims=True)
    lane_w = lax.broadcasted_iota(I32, (KV_W, wlen), 1)
    wout_ref[0] = jnp.where(lane_w == wlen - 1, new_col, pltpu.roll(cw, wlen - 1, axis=1))

    gt = jnp.broadcast_to(gt_ref[0], (N_HEADS, GATE_PAD))
    gate = lambda br: jnp.sum(jnp.where(lane == rowi * N_BRANCH + br, gt, 0.0), axis=-1, keepdims=True)
    o = gate(0) * oc_ref[0] + gate(1) * o_s + gate(2) * o_w
    o_ref[0] = jnp.where(own, o, 0.0)


def _nsa_sample_sel(idx_flat, pt_flat, q3, gt3, oc, win_t, kvn4_t, kvnw_t, cache3, n_pages, k_past):
    nb = q3.shape[0]
    page = cache3.shape[2]
    wlen = win_t.shape[2]
    assert page == LANE
    per = lambda *blk: pl.BlockSpec((1,) + blk, lambda i, a, c: (i, 0, 0))
    whole = lambda arr: pl.BlockSpec(arr.shape, lambda i, a, c: (0, 0, 0))
    gs = pltpu.PrefetchScalarGridSpec(
        num_scalar_prefetch=2, grid=(nb,),
        in_specs=[per(1, NSA_WIDTH), per(1, GATE_PAD), per(N_HEADS, LANE), per(KV_W, wlen),
                  whole(kvn4_t), whole(kvnw_t), pl.BlockSpec(memory_space=pl.ANY)],
        out_specs=[per(N_HEADS, LANE), per(KV_W, wlen)],
        scratch_shapes=[pltpu.VMEM((2, 2, LANE, k_past * page), F32), pltpu.SemaphoreType.DMA((2,))])
    return pl.pallas_call(
        functools.partial(_nsa_sample_sel_kernel, n_pages=n_pages, n_batch=nb, page=page, k_past=k_past),
        grid_spec=gs,
        out_shape=[jax.ShapeDtypeStruct((nb, N_HEADS, LANE), F32),
                   jax.ShapeDtypeStruct((nb, KV_W, wlen), F32)],
        compiler_params=_params(("arbitrary",)),
        name="nsa_sample_sel",
    )(idx_flat, pt_flat, q3, gt3, oc, win_t, kvn4_t, kvnw_t, cache3)


def _mix_sample_kernel(x_ref, yp_ref, yn_ref, wo_ref, g_ref, wq_ref, h1_ref, qx_ref):
    y = jnp.concatenate([yp_ref[...], yn_ref[...]], axis=1)
    h1 = x_ref[...] + _dot(y, wo_ref[...])
    h1_ref[...] = h1
    qx_ref[...] = _dot(_rms(h1, g_ref[...]), wq_ref[...]) * (X_HEAD_DIM ** -0.5)


def _mix_sample(x, yp, yn, w_out, g_x, w_xq):
    n, d = x.shape
    full = lambda a: pl.BlockSpec(a.shape, lambda i: (0,) * a.ndim)
    args = (x, yp, yn, w_out, g_x, w_xq)
    return pl.pallas_call(
        _mix_sample_kernel, grid=(1,),
        in_specs=[full(a) for a in args],
        out_specs=[pl.BlockSpec((n, d), lambda i: (0, 0))] * 2,
        out_shape=[jax.ShapeDtypeStruct((n, d), F32)] * 2,
        compiler_params=_params(("arbitrary",)),
        name="mix_sample",
    )(*args)


def _xattn_sample_kernel(qx_ref, mem_ref, o_ref):
    n_mem = mem_ref.shape[1]
    n = n_mem * 8
    q = qx_ref[0]
    order = [(half, h) for half in range(X_HEAD_DIM // LANE) for h in range(X_HEADS)]
    col = lambda half, h: slice(h * X_HEAD_DIM + half * LANE, h * X_HEAD_DIM + (half + 1) * LANE)
    qt = jnp.concatenate([q[:, col(half, h)] for half, h in order], axis=0)
    x = mem_ref[0]
    a = _dot_nt(qt, x[:, 0].reshape(n, LANE))
    rowi = lax.broadcasted_iota(I32, (8, n), 0)
    lane = lax.broadcasted_iota(I32, (8, n), 1)
    own = (lane & 7) == rowi
    a = jnp.where(own, a, 0.0)
    other = pltpu.roll(a, X_HEADS, axis=0)
    s = a + jnp.where(rowi < X_HEADS, pltpu.roll(other, n - X_HEADS, axis=1), pltpu.roll(other, X_HEADS, axis=1))
    s = jnp.where(own, s, -jnp.inf)
    e = jnp.exp(s - jnp.max(s, axis=-1, keepdims=True))
    p = e / jnp.sum(e, axis=-1, keepdims=True)
    o8 = _dot(p, x[:, 1].reshape(n, LANE))
    o_ref[0] = jnp.concatenate([o8[half * X_HEADS + h:half * X_HEADS + h + 1]
                                for h in range(X_HEADS) for half in range(X_HEAD_DIM // LANE)], axis=1)


def _xattn_sample(qx3, mem5):
    nb, _, xw = qx3.shape
    return pl.pallas_call(
        _xattn_sample_kernel, grid=(nb,),
        in_specs=[pl.BlockSpec((1, 1, xw), lambda i: (i, 0, 0)),
                  pl.BlockSpec((1,) + mem5.shape[1:], lambda i: (i, 0, 0, 0, 0))],
        out_specs=pl.BlockSpec((1, 1, xw), lambda i: (i, 0, 0)),
        out_shape=jax.ShapeDtypeStruct((nb, 1, xw), F32),
        compiler_params=_params(("arbitrary",)),
        name="xattn_sample",
    )(qx3, mem5)


def _resid_proj_kernel(h_ref, o_ref, w_ref, y_ref):
    y_ref[...] = h_ref[...] + _dot(o_ref[...], w_ref[...])


def _resid_proj(h, o, w):
    n, d = h.shape
    full = lambda a: pl.BlockSpec(a.shape, lambda i: (0,) * a.ndim)
    return pl.pallas_call(
        _resid_proj_kernel, grid=(1,),
        in_specs=[full(h), full(o), full(w)],
        out_specs=pl.BlockSpec((n, d), lambda i: (0, 0)),
        out_shape=jax.ShapeDtypeStruct((n, d), F32),
        compiler_params=_params(("arbitrary",)),
        name="resid_proj",
    )(h, o, w)


def _prep_weights(g_mix, w_in, w_pool, pool_scale, w_cmp, pe_cmp, w_out, g_xattn, g_mem, w_xq, w_xkv,
                  w_xo, g_ffn, w_up, conv_w, conv_b, w_down, g_final):
    o_kv = POOL_WIDTH + NSA_WIDTH
    o_gate = o_kv + N_BRANCH * KV_W
    w_tok = jnp.concatenate([w_in[:, :o_kv + KV_W + LANE], jnp.pad(w_in[:, o_gate:], ((0, 0), (0, GATE_PAD - N_GATE)))],
                            axis=1)
    r = CMP_LEN // CMP_STRIDE
    w5 = w_cmp.reshape(2, r, CMP_STRIDE, HEAD_DIM, HEAD_DIM)
    eye = jnp.eye(2, dtype=F32)
    wbig = jnp.einsum('cisde,gG->csgdiGe', w5, eye).reshape(2, CMP_STRIDE * LANE, r * LANE)
    wpe = jnp.einsum('cke,cC->ckCe', w_cmp.reshape(2, CMP_LEN * HEAD_DIM, HEAD_DIM), eye)
    wpe = jnp.broadcast_to(wpe[:, :, :, None, :], (2, CMP_LEN * HEAD_DIM, 2, KV_HEADS, HEAD_DIM))
    row = lambda v: v.reshape(1, -1)
    return dict(
        g_mix=row(g_mix), w_tok=w_tok.astype(BF16), w_feat=w_in[:, o_kv:o_gate].T.astype(BF16),
        w_pool=w_pool.astype(BF16), pool_scale=row(pool_scale),
        wbig=wbig.astype(BF16), wpe=wpe.reshape(2 * CMP_LEN * HEAD_DIM, KV_W).astype(BF16),
        pe8=jnp.broadcast_to(pe_cmp.reshape(1, -1), (8, 2 * CMP_LEN * HEAD_DIM)),
        w_out=w_out.astype(BF16), g_xattn=row(g_xattn), g_mem=row(g_mem), w_xq=w_xq.astype(BF16),
        w_xkv=w_xkv.astype(BF16), w_xo=w_xo.astype(BF16), g_ffn=row(g_ffn), w_up=w_up.astype(BF16),
        cw3=conv_w.reshape(3, 2, D_FF), cb2=conv_b.reshape(2, D_FF), w_down=w_down.astype(BF16),
        g_final=row(g_final))


def _token_major(a_t, lead):
    b, _, t = a_t.shape
    nd = len(lead)
    return jnp.transpose(a_t.reshape((b,) + lead + (t,)), (0, nd + 1) + tuple(range(1, nd + 1)))


def _feature_major(a, nfeat):
    nd = a.ndim
    return jnp.transpose(a, (0,) + tuple(range(2, nd)) + (1,)).reshape(a.shape[0], nfeat, a.shape[1])


def _prompt_layer(x, mem, w, *, tm=512, tm_x=1024, tf=D_FF, kc=(2048, 1024, 512)):
    b, t, d = x.shape
    n = b * t
    u, q, kvc_tok, ks_tok, gt, kv4t, kvwt = _mix_in(x, w['g_mix'], w['w_tok'], w['w_feat'], tm, "mix_in_prompt")
    r3 = lambda a: a.reshape(b, t, a.shape[-1])
    u, q, kvc_tok, ks_tok, gt = map(r3, (u, q, kvc_tok, ks_tok, gt))
    y_pool = _pool_prompt(u, w['w_pool'], w['pool_scale'], tm)
    kvc = _compress_prompt(kvc_tok, w['wbig'], w['pe8'], w['wpe'])
    y_nsa = _nsa_prompt(q, gt, kvc, ks_tok, kv4t, kvwt, kc)
    m = mem.shape[1]
    (mem_kv,) = _rms_proj(mem.reshape(b * m, d), w['g_mem'], w['w_xkv'], (2 * X_HEADS * X_HEAD_DIM,), m,
                          name="mem_project")
    mem_kv = mem_kv.reshape(b, m, 2 * X_HEADS * X_HEAD_DIM)
    h2 = _xattn_prompt(x, y_pool, y_nsa, w['w_out'], w['g_xattn'], w['w_xq'], mem_kv, w['w_xo'], tm_x)
    y, st = _ffn_prompt(h2.reshape(n, d), w['g_ffn'], w['w_up'], w['cw3'], w['cb2'], w['w_down'],
                        w['g_final'], b, tm, tf)
    wlen = min(WINDOW, t)
    return (y.reshape(b, t, d),
            _token_major(kv4t, (4, KV_HEADS, HEAD_DIM)),
            _token_major(kvwt[:, :, t - wlen:], (2, KV_HEADS, HEAD_DIM)),
            u[:, t - POOL_STATE:],
            jnp.swapaxes(st[t // tm - 1::t // tm, :, 6:8], 1, 2).reshape(b, 2, 2 * D_FF),
            mem_kv.reshape(b, m, 2, X_HEADS, X_HEAD_DIM))


def _sample_layer(x, cache_kv, page_table, cache_win, state_pool, state_ffn, cache_mem, w, *, tf=1408):
    nb, _, d = x.shape
    n_phys, page = cache_kv.shape[:2]
    n_pages = page_table.shape[1]
    past = n_pages * page
    u, q, _, _, gt, kv4t, kvwt = _mix_in(x.reshape(1, nb, d), w['g_mix'], w['w_tok'], w['w_feat'], nb, "mix_in_sample")
    y_pool = _pool_sample(jnp.swapaxes(state_pool, 0, 1), u, w['w_pool'], w['pool_scale'], past)
    cache3 = _feature_major(cache_kv, 2 * KV_W)
    pt_flat = page_table.reshape(-1)
    jq = past // SEL_BLOCK
    ns_pad = -(-(jq + 1) // LANE) * LANE
    k_past = min(N_SELECT, jq + 1) - 1
    q3 = q.reshape(nb, 1, NSA_WIDTH)
    oc, imp = _nsa_sample_cmp(pt_flat, q3, cache3, w['wbig'], w['pe8'], w['wpe'], n_pages, ns_pad)
    idx = _topk_sample(imp[:, :KV_HEADS].reshape(nb * KV_HEADS, ns_pad), jq, k_past)
    idx_flat = idx[:, :k_past].reshape(-1)
    o8, win_out = _nsa_sample_sel(idx_flat, pt_flat, q3, gt.reshape(nb, 1, GATE_PAD), oc,
                                  _feature_major(cache_win, KV_W), kv4t, kvwt, cache3, n_pages, k_past)
    y_nsa = jnp.concatenate([o8[:, :GROUP, :HEAD_DIM].reshape(nb, GROUP * HEAD_DIM),
                             o8[:, GROUP:, HEAD_DIM:].reshape(nb, GROUP * HEAD_DIM)], axis=1)
    h1, qx = _mix_sample(x.reshape(nb, d), y_pool, y_nsa, w['w_out'], w['g_xattn'], w['w_xq'])
    m = cache_mem.shape[1]
    halves = X_HEAD_DIM // LANE
    mem5 = jnp.transpose(cache_mem.reshape(nb, m, 2, X_HEADS, halves, LANE), (0, 1, 2, 4, 3, 5))
    ox = _xattn_sample(qx.reshape(nb, 1, d), mem5.reshape(nb, m, 2, halves * X_HEADS, LANE))
    h2 = _resid_proj(h1, ox.reshape(nb, d), w['w_xo'])
    y, up2 = _ffn_sample(h2, w['g_ffn'], w['w_up'], w['cw3'], w['cb2'], w['w_down'], w['g_final'],
                         jnp.swapaxes(state_ffn, 0, 1), tf)
    up = jnp.swapaxes(up2, 0, 1).reshape(nb, 1, 2 * D_FF)
    return (y.reshape(nb, 1, d),
            jnp.swapaxes(_token_major(kv4t, (4, KV_HEADS, HEAD_DIM)), 0, 1),
            _token_major(win_out, (2, KV_HEADS, HEAD_DIM)),
            jnp.concatenate([state_pool[:, 1:], u[:, None, :]], axis=1),
            jnp.concatenate([state_ffn[:, 1:], up], axis=1),
            )


def kernel(x_prompt, x_sample, mem_prompt, cache_kv, page_table, cache_win, state_pool, state_ffn, cache_mem, g_mix, w_in, w_pool, pool_scale, w_cmp, pe_cmp, w_out, g_xattn, g_mem, w_xq, w_xkv, w_xo, g_ffn, w_up, conv_w, conv_b, w_down, g_final):
    assert g_mix.shape[0] == 1, "single layer"
    w = _prep_weights(g_mix[0], w_in[0], w_pool[0], pool_scale[0], w_cmp[0], pe_cmp[0], w_out[0], g_xattn[0],
                      g_mem[0], w_xq[0], w_xkv[0], w_xo[0], g_ffn[0], w_up[0], conv_w[0], conv_b[0], w_down[0],
                      g_final)
    yp, kv_p, win_p, pool_p, ffn_p, mem_p = _prompt_layer(x_prompt, mem_prompt, w)
    ys, kv_s, win_s, pool_s, ffn_s = _sample_layer(x_sample, cache_kv[0], page_table, cache_win[0],
                                                    state_pool[0], state_ffn[0], cache_mem[0], w)
    lead = lambda a: a[None]
    return (yp, ys, lead(kv_p), lead(kv_s), lead(win_p), lead(win_s), lead(pool_p), lead(pool_s),
            lead(ffn_p), lead(ffn_s), lead(mem_p))
```

```python
import functools

import jax
import jax.numpy as jnp
from jax import lax
from jax.experimental import pallas as pl
from jax.experimental.pallas import tpu as pltpu

F32 = jnp.float32
BF16 = jnp.bfloat16
I32 = jnp.int32

D_MODEL = 1024
POOL_WIDTH = 512
POOL_WINDOWS = (2, 4, 8, 16)
POOL_GC = 128
POOL_STATE = 15
NSA_WIDTH = 512
HEAD_DIM = 64
N_HEADS = 8
KV_HEADS = 2
GROUP = 4
N_BRANCH = 3
CMP_LEN = 32
CMP_STRIDE = 16
SEL_BLOCK = 64
N_SELECT = 16
WINDOW = 512
QBLK = 128
KV_W = 2 * KV_HEADS * HEAD_DIM
N_GATE = N_HEADS * N_BRANCH
LANE = 128
GATE_PAD = LANE
X_HEADS = 4
X_HEAD_DIM = 256
D_FF = 2816
EPS = 1e-6
NEG = -1e30
VMEM_LIMIT = 56 * 1024 * 1024
CHUNK_PITCH = 24
PAGE_UNROLL = 16
SEL_TILES = 4

def _dot(a, b):
    return jnp.dot(a.astype(BF16), b.astype(BF16), preferred_element_type=F32)


def _dot_nt(a, b):
    return lax.dot_general(a.astype(BF16), b.astype(BF16), (((1,), (1,)), ((), ())),
                           preferred_element_type=F32)


def _rms(x, g):
    return x * lax.rsqrt(jnp.mean(x * x, axis=-1, keepdims=True) + EPS) * g


def _split3(x):
    p0 = x.astype(BF16)
    r = x - p0.astype(F32)
    p1 = r.astype(BF16)
    p2 = (r - p1.astype(F32)).astype(BF16)
    return p0, p1, p2


def _masked_softmax(s, mask):
    s = jnp.where(mask, s, -jnp.inf)
    m = jnp.max(s, axis=-1, keepdims=True)
    m = jnp.where(m == -jnp.inf, 0.0, m)
    e = jnp.exp(s - m)
    return e / jnp.maximum(jnp.sum(e, axis=-1, keepdims=True), 1e-30)


def _params(sem):
    return pltpu.CompilerParams(dimension_semantics=sem, vmem_limit_bytes=VMEM_LIMIT)


def _rms_proj_kernel(x_ref, g_ref, w_ref, *out_refs, widths, sigmoid_last):
    xn = _rms(x_ref[...], g_ref[...]).astype(BF16)
    off = 0
    for k, (o_ref, wd) in enumerate(zip(out_refs, widths)):
        y = jnp.dot(xn, w_ref[:, off:off + wd], preferred_element_type=F32)
        if sigmoid_last and k == len(widths) - 1:
            y = 1.0 / (1.0 + jnp.exp(-y))
        o_ref[...] = y
        off += wd


def _rms_proj(x, g, w, widths, tm, sigmoid_last=False, name="rms_proj"):
    n, d = x.shape
    ntot = w.shape[1]
    assert sum(widths) == ntot and n % tm == 0
    return pl.pallas_call(
        functools.partial(_rms_proj_kernel, widths=tuple(widths), sigmoid_last=sigmoid_last),
        grid=(n // tm,),
        in_specs=[pl.BlockSpec((tm, d), lambda i: (i, 0)),
                  pl.BlockSpec((1, d), lambda i: (0, 0)),
                  pl.BlockSpec((d, ntot), lambda i: (0, 0))],
        out_specs=[pl.BlockSpec((tm, wd), lambda i: (i, 0)) for wd in widths],
        out_shape=[jax.ShapeDtypeStruct((n, wd), F32) for wd in widths],
        compiler_params=_params(("arbitrary",)),
        name=name,
    )(x, g, w)


def _mix_in_kernel(x_ref, g_ref, wt_ref, wf_ref, u_ref, q_ref, kc_ref, ks_ref, gt_ref, kv4_ref, kvw_ref):
    xn = _rms(x_ref[...], g_ref[...]).astype(BF16)
    off = 0
    for o_ref in (u_ref, q_ref, kc_ref, ks_ref, gt_ref):
        wd = o_ref.shape[-1]
        y = jnp.dot(xn, wt_ref[:, off:off + wd], preferred_element_type=F32)
        o_ref[...] = 1.0 / (1.0 + jnp.exp(-y)) if o_ref is gt_ref else y
        off += wd
    kv_t = lax.dot_general(wf_ref[...], xn, (((1,), (1,)), ((), ())), preferred_element_type=F32)
    kv4_ref[0] = kv_t[0:2 * KV_W]
    kvw_ref[0] = kv_t[2 * KV_W:3 * KV_W]


def _mix_in(x3, g, w_tok, w_feat, tm, name):
    b, t, d = x3.shape
    n = b * t
    tps = t // tm
    widths = (POOL_WIDTH, NSA_WIDTH, KV_W, LANE, GATE_PAD)
    assert sum(widths) == w_tok.shape[1] and t % tm == 0
    return pl.pallas_call(
        _mix_in_kernel,
        grid=(n // tm,),
        in_specs=[pl.BlockSpec((tm, d), lambda i: (i, 0)),
                  pl.BlockSpec((1, d), lambda i: (0, 0)),
                  pl.BlockSpec(w_tok.shape, lambda i: (0, 0)),
                  pl.BlockSpec(w_feat.shape, lambda i: (0, 0))],
        out_specs=[pl.BlockSpec((tm, wd), lambda i: (i, 0)) for wd in widths]
        + [pl.BlockSpec((1, 2 * KV_W, tm), lambda i: (i // tps, 0, i % tps)),
           pl.BlockSpec((1, KV_W, tm), lambda i: (i // tps, 0, i % tps))],
        out_shape=[jax.ShapeDtypeStruct((n, wd), F32) for wd in widths]
        + [jax.ShapeDtypeStruct((b, 2 * KV_W, t), F32), jax.ShapeDtypeStruct((b, KV_W, t), F32)],
        compiler_params=_params(("arbitrary",)),
        name=name,
    )(x3.reshape(n, d), g, w_tok, w_feat)


def _pool_tail(s_list, u, cnts, w_ref, sc_ref, y_ref, lead):
    for gi in range(len(POOL_WINDOWS)):
        cols = slice(gi * POOL_GC, (gi + 1) * POOL_GC)
        d = s_list[gi] / cnts[gi] - u[:, cols]
        y = _dot(d, w_ref[gi]) * sc_ref[:, cols]
        if lead:
            y_ref[0, :, cols] = y
        else:
            y_ref[:, cols] = y


def _pool_prompt_kernel(u_ref, w_ref, sc_ref, y_ref, ext_ref, *, tm):
    t = pl.program_id(1)
    hist = 16

    @pl.when(t == 0)
    def _():
        ext_ref[0:hist, :] = jnp.zeros((hist, POOL_WIDTH), F32)

    @pl.when(t > 0)
    def _():
        ext_ref[0:hist, :] = ext_ref[tm:tm + hist, :]

    u = u_ref[0]
    ext_ref[hist:hist + tm, :] = u
    pos = t * tm + lax.broadcasted_iota(I32, (tm, 1), 0)
    s_list, cnts = [], []
    for gi, w in enumerate(POOL_WINDOWS):
        cols = slice(gi * POOL_GC, (gi + 1) * POOL_GC)
        s = u[:, cols]
        for k in range(1, w):
            s = s + ext_ref[hist - k:hist - k + tm, cols]
        s_list.append(s)
        cnts.append(jnp.minimum(pos + 1, w).astype(F32))
    _pool_tail(s_list, u, cnts, w_ref, sc_ref, y_ref, True)


def _pool_prompt(u3, w_pool, pool_scale, tm):
    b, t, _ = u3.shape
    return pl.pallas_call(
        functools.partial(_pool_prompt_kernel, tm=tm),
        grid=(b, t // tm),
        in_specs=[pl.BlockSpec((1, tm, POOL_WIDTH), lambda i, j: (i, j, 0)),
                  pl.BlockSpec((4, POOL_GC, POOL_GC), lambda i, j: (0, 0, 0)),
                  pl.BlockSpec((1, POOL_WIDTH), lambda i, j: (0, 0))],
        out_specs=pl.BlockSpec((1, tm, POOL_WIDTH), lambda i, j: (i, j, 0)),
        out_shape=jax.ShapeDtypeStruct((b, t, POOL_WIDTH), F32),
        scratch_shapes=[pltpu.VMEM((tm + 16, POOL_WIDTH), F32)],
        compiler_params=_params(("arbitrary", "arbitrary")),
        name="pool_prompt",
    )(u3, w_pool, pool_scale)


def _pool_sample_kernel(st_ref, u_ref, w_ref, sc_ref, y_ref, *, pos0):
    u = u_ref[...]
    s_list, cnts = [], []
    for gi, w in enumerate(POOL_WINDOWS):
        cols = slice(gi * POOL_GC, (gi + 1) * POOL_GC)
        s = u[:, cols]
        for k in range(1, w):
            s = s + st_ref[POOL_STATE - k, :, cols]
        s_list.append(s)
        cnts.append(float(min(pos0 + 1, w)))
    _pool_tail(s_list, u, cnts, w_ref, sc_ref, y_ref, False)


def _pool_sample(st_t, u, w_pool, pool_scale, pos0):
    n = u.shape[0]
    return pl.pallas_call(
        functools.partial(_pool_sample_kernel, pos0=pos0),
        grid=(1,),
        in_specs=[pl.BlockSpec((POOL_STATE, n, POOL_WIDTH), lambda i: (0, 0, 0)),
                  pl.BlockSpec((n, POOL_WIDTH), lambda i: (0, 0)),
                  pl.BlockSpec((4, POOL_GC, POOL_GC), lambda i: (0, 0, 0)),
                  pl.BlockSpec((1, POOL_WIDTH), lambda i: (0, 0))],
        out_specs=pl.BlockSpec((n, POOL_WIDTH), lambda i: (0, 0)),
        out_shape=jax.ShapeDtypeStruct((n, POOL_WIDTH), F32),
        compiler_params=_params(("arbitrary",)),
        name="pool_sample",
    )(st_t, u, w_pool, pool_scale)


def _compress_rows(load_half, wbig_ref, pe_ref, wpe_ref, n_chunk):
    pe_row = _dot(pe_ref[...], wpe_ref[...])[0:1]
    halves = []
    for c in range(2):
        x = jnp.concatenate([load_half(c, s).astype(BF16) for s in range(CMP_STRIDE)], axis=1)
        y = jnp.dot(x, wbig_ref[c], preferred_element_type=F32)
        halves.append(y[:, :LANE] + pltpu.roll(y[:, LANE:], n_chunk - 1, axis=0))
    return jnp.concatenate(halves, axis=1) + pe_row


def _compress_prompt_kernel(k_ref, v_ref, wbig_ref, pe_ref, wpe_ref, out_ref, *, n_chunk):
    load = lambda c, s: (k_ref, v_ref)[c][0, pl.ds(s, n_chunk, stride=CMP_STRIDE), :]
    out_ref[0] = _compress_rows(load, wbig_ref, pe_ref, wpe_ref, n_chunk)


def _compress_prompt(kv4, wbig, pe8, wpe):
    b, t, _ = kv4.shape
    n_chunk = t // CMP_STRIDE
    return pl.pallas_call(
        functools.partial(_compress_prompt_kernel, n_chunk=n_chunk),
        grid=(b,),
        in_specs=[pl.BlockSpec((1, t, LANE), lambda i: (i, 0, 0)),
                  pl.BlockSpec((1, t, LANE), lambda i: (i, 0, 1)),
                  pl.BlockSpec(wbig.shape, lambda i: (0, 0, 0)),
                  pl.BlockSpec(pe8.shape, lambda i: (0, 0)),
                  pl.BlockSpec(wpe.shape, lambda i: (0, 0))],
        out_specs=pl.BlockSpec((1, n_chunk, KV_W), lambda i: (i, 0, 0)),
        out_shape=jax.ShapeDtypeStruct((b, n_chunk, KV_W), F32),
        compiler_params=_params(("arbitrary",)),
        name="compress_prompt",
    )(kv4, kv4, wbig, pe8, wpe)


def _stack_group_queries(q, g):
    lane = lax.broadcasted_iota(I32, (q.shape[0], LANE), 1)
    parts = []
    for hh in range(GROUP):
        h = GROUP * g + hh
        ch = q[:, (h // 2) * LANE:(h // 2 + 1) * LANE]
        if h % 2 != g:
            ch = pltpu.roll(ch, HEAD_DIM, axis=1)
        parts.append(jnp.where((lane >> 6) == g, ch, 0.0))
    return jnp.concatenate(parts, axis=0).astype(BF16)


def _place_heads(o_g, weights, g, outs):
    lane = lax.broadcasted_iota(I32, (QBLK, LANE), 1)
    for hh in range(GROUP):
        h = GROUP * g + hh
        rs = slice(hh * QBLK, (hh + 1) * QBLK)
        piece = sum(w_[:, 0:1] * o_[rs] for w_, o_ in zip(weights(h), o_g))
        if h % 2 != g:
            piece = pltpu.roll(piece, HEAD_DIM, axis=1)
        piece = jnp.where((lane >> 6) == (h % 2), piece, 0.0)
        outs[h // 2] = piece if outs[h // 2] is None else outs[h // 2] + piece


def _nsa_dense_kernel(q_ref, gt_ref, kvc_ref, kw_ref, vw_ref, part_ref, sel_ref, *, t_len):
    i = pl.program_id(1)
    t0 = i * QBLK
    nc = t_len // CMP_STRIDE
    ns = t_len // SEL_BLOCK
    wl = min(WINDOW + QBLK, t_len)
    q = q_ref[0] * (HEAD_DIM ** -0.5)
    gt = gt_ref[0]
    pos_q = t0 + lax.broadcasted_iota(I32, (QBLK, 1), 0)

    oj = lax.broadcasted_iota(I32, (ns, nc), 0)
    on = lax.broadcasted_iota(I32, (ns, nc), 1)
    ovl_t = jnp.where((on * CMP_STRIDE + CMP_LEN - 1 >= oj * SEL_BLOCK)
                      & (on * CMP_STRIDE <= oj * SEL_BLOCK + SEL_BLOCK - 1), 1.0, 0.0).astype(BF16)
    jb = lax.broadcasted_iota(I32, (ns, QBLK), 0)
    jq = (t0 + lax.broadcasted_iota(I32, (ns, QBLK), 1)) >> 6
    forced = (jb == 0) | (jb == jq) | (jb == jq - 1)

    kc_all = kvc_ref[0, :, 0:LANE].astype(BF16)
    vc_all = kvc_ref[0, :, LANE:2 * LANE].astype(BF16)
    ws = pl.multiple_of(jnp.maximum(t0 - WINDOW, 0), QBLK)
    kwc = kw_ref[0, :, pl.ds(ws, wl)].astype(BF16)
    vwc = vw_ref[0, :, pl.ds(ws, wl)].astype(BF16)

    ncol = lax.broadcasted_iota(I32, (1, nc), 1)
    bias_c = jnp.where((ncol * CMP_STRIDE + CMP_LEN - 1) <= pos_q, 0.0, NEG)
    bias_c = jnp.concatenate([bias_c] * GROUP, axis=0)
    dpos = pos_q - (ws + lax.broadcasted_iota(I32, (1, wl), 1))
    bias_w = jnp.where((dpos >= 0) & (dpos <= WINDOW), 0.0, NEG)
    bias_w = jnp.concatenate([bias_w] * GROUP, axis=0)

    def softmax_parts(s):
        m = jnp.max(s, axis=-1, keepdims=True)
        e = jnp.exp(s - m)
        return e, jnp.where(m > 0.5 * NEG, 1.0 / jnp.sum(e, axis=-1, keepdims=True), 0.0)

    outs = [None] * (NSA_WIDTH // LANE)
    for g in range(KV_HEADS):
        qg = _stack_group_queries(q, g)

        e_c, inv_c = softmax_parts(_dot_nt(qg, kc_all) + bias_c)
        o_c = _dot(e_c, vc_all) * inv_c

        p_c = e_c * inv_c
        psum = p_c[0:QBLK] + p_c[QBLK:2 * QBLK] + p_c[2 * QBLK:3 * QBLK] + p_c[3 * QBLK:4 * QBLK]
        imp_t = sum(lax.dot_general(ovl_t, part, (((1,), (1,)), ((), ())), preferred_element_type=F32)
                    for part in _split3(psum))
        x = jnp.where(forced, jnp.inf, jnp.where(jb > jq, -jnp.inf, imp_t))
        cnt = jnp.zeros((ns, QBLK), I32)
        for jp in range(ns):
            row = x[jp:jp + 1, :]
            beats = (row > x) | ((row == x) & (jb > jp))
            cnt = cnt + jnp.where(beats, 1, 0)
        sel_ref[0, 0, g * ns:(g + 1) * ns, :] = jnp.where(cnt < N_SELECT, 1.0, 0.0)

        e_w, inv_w = softmax_parts(_dot(qg, kwc) + bias_w)
        o_w = _dot_nt(e_w, vwc) * inv_w

        _place_heads((o_c, o_w), lambda h: (gt[:, 3 * h:3 * h + 1], gt[:, 3 * h + 2:3 * h + 3]), g, outs)
    part_ref[0] = jnp.concatenate(outs, axis=1)


def _nsa_select_kernel(q_ref, gt_ref, sel_ref, part_ref, ks_ref, vs_ref, o_ref, *, t_len, kc):
    i = pl.program_id(1)
    nqry = SEL_TILES * QBLK
    t0 = i * nqry
    ns = t_len // SEL_BLOCK
    rows = GROUP * nqry
    kcs = kc
    n_small = (t0 + nqry + kcs[-1] - 1) // kcs[-1]
    gt_t = gt_ref[0].T
    blocks = []
    q_t = (q_ref[0] * (HEAD_DIM ** -0.5)).T
    zero_half = jnp.zeros((HEAD_DIM, nqry), F32)
    for g in range(KV_HEADS):
        heads = []
        for hh in range(GROUP):
            q_h = q_t[(GROUP * g + hh) * HEAD_DIM:(GROUP * g + hh + 1) * HEAD_DIM]
            heads.append(jnp.concatenate([q_h, zero_half] if g == 0 else [zero_half, q_h], axis=0))
        qg_t = jnp.concatenate(heads, axis=1).astype(BF16)

        def sel_chunk(kcz, base, c, carry):
            m_prev, l_prev, acc = carry
            k0 = pl.multiple_of(base + c * kcz, kcs[-1])
            nblk = kcz // SEL_BLOCK
            s = _dot(ks_ref[0, pl.ds(k0, kcz), :], qg_t)
            j0 = pl.multiple_of(g * ns + k0 // SEL_BLOCK, kcs[-1] // SEL_BLOCK)
            chosen = []
            for tile in range(SEL_TILES):
                rows_j = sel_ref[0, tile, pl.ds(j0, nblk), :]
                chosen.append(jnp.concatenate([jnp.broadcast_to(rows_j[j:j + 1], (SEL_BLOCK, QBLK))
                                               for j in range(nblk)], axis=0))
            chosen = jnp.concatenate(chosen, axis=1)
            key_pos = k0 + lax.broadcasted_iota(I32, (kcz, nqry), 0)
            qry_pos = t0 + lax.broadcasted_iota(I32, (kcz, nqry), 1)
            bias = jnp.where((chosen > 0.5) & (key_pos <= qry_pos), 0.0, NEG)
            s = s + jnp.concatenate([bias] * GROUP, axis=1)
            m_new = jnp.maximum(m_prev, jnp.max(s, axis=0, keepdims=True))
            alpha = jnp.exp(m_prev - m_new)
            p = jnp.exp(s - m_new)
            l_new = alpha * l_prev + jnp.sum(p, axis=0, keepdims=True)
            return m_new, l_new, alpha * acc + _dot(vs_ref[0, :, pl.ds(k0, kcz)], p)

        carry = (jnp.full((1, rows), NEG, F32), jnp.zeros((1, rows), F32), jnp.zeros((LANE, rows), F32))
        done = 0
        for kcz in kcs:
            count = (n_small * kcs[-1] - done) // kcz
            carry = lax.fori_loop(0, count, functools.partial(sel_chunk, kcz, done), carry)
            done = done + count * kcz
        _, l_s, acc_s = carry
        o_t = acc_s[g * HEAD_DIM:(g + 1) * HEAD_DIM] / l_s
        for hh in range(GROUP):
            h = GROUP * g + hh
            blocks.append(o_t[:, hh * nqry:(hh + 1) * nqry] * gt_t[3 * h + 1:3 * h + 2])
    o_ref[0] = part_ref[0] + jnp.concatenate(blocks, axis=0).T


def _nsa_prompt(q3, gt3, kvc, ks_tok, kv4t, kvwt, kc):
    b, t, _ = q3.shape
    kc = tuple(k for k in kc if k <= t)
    assert all(t % k == 0 for k in kc) and t % QBLK == 0
    nq = t // QBLK
    ns = t // SEL_BLOCK
    slab = lambda c: pl.BlockSpec((1, LANE, t), lambda i, j: (i, c, 0))
    tile = lambda w_: pl.BlockSpec((1, QBLK, w_), lambda i, j: (i, j, 0))
    sel_spec = pl.BlockSpec((1, 1, KV_HEADS * ns, QBLK), lambda i, j: (i, j, 0, 0))
    part, sel = pl.pallas_call(
        functools.partial(_nsa_dense_kernel, t_len=t),
        grid=(b, nq),
        in_specs=[tile(NSA_WIDTH), tile(GATE_PAD),
                  pl.BlockSpec((1, t // CMP_STRIDE, KV_W), lambda i, j: (i, 0, 0)),
                  slab(0), slab(1)],
        out_specs=[tile(NSA_WIDTH), sel_spec],
        out_shape=[jax.ShapeDtypeStruct((b, t, NSA_WIDTH), F32),
                   jax.ShapeDtypeStruct((b, nq, KV_HEADS * ns, QBLK), F32)],
        compiler_params=_params(("arbitrary", "arbitrary")),
        name="nsa_dense",
    )(q3, gt3, kvc, kvwt, kvwt)
    assert nq % SEL_TILES == 0
    wide = lambda w_: pl.BlockSpec((1, SEL_TILES * QBLK, w_), lambda i, j: (i, j, 0))
    return pl.pallas_call(
        functools.partial(_nsa_select_kernel, t_len=t, kc=kc),
        grid=(b, nq // SEL_TILES),
        in_specs=[wide(NSA_WIDTH), wide(GATE_PAD),
                  pl.BlockSpec((1, SEL_TILES, KV_HEADS * ns, QBLK), lambda i, j: (i, j, 0, 0)), wide(NSA_WIDTH),
                  pl.BlockSpec((1, t, LANE), lambda i, j: (i, 0, 0)), slab(3)],
        out_specs=wide(NSA_WIDTH),
        out_shape=jax.ShapeDtypeStruct((b, t, NSA_WIDTH), F32),
        compiler_params=_params(("arbitrary", "arbitrary")),
        name="nsa_select",
    )(q3, gt3, sel, part, ks_tok, kv4t)


def _xattn_prompt_kernel(x_ref, yp_ref, yn_ref, wo_ref, g_ref, wq_ref, mem_ref, wxo_ref, o_ref):
    y = jnp.concatenate([yp_ref[0], yn_ref[0]], axis=1)
    h1 = x_ref[0] + _dot(y, wo_ref[...])
    qx = _dot(_rms(h1, g_ref[...]), wq_ref[...]) * (X_HEAD_DIM ** -0.5)
    xw = X_HEADS * X_HEAD_DIM
    outs = []
    for h in range(X_HEADS):
        cs = slice(h * X_HEAD_DIM, (h + 1) * X_HEAD_DIM)
        s = _dot_nt(qx[:, cs], mem_ref[0, :, cs])
        e = jnp.exp(s - jnp.max(s, axis=-1, keepdims=True))
        p = e / jnp.sum(e, axis=-1, keepdims=True)
        outs.append(_dot(p, mem_ref[0, :, xw + h * X_HEAD_DIM:xw + (h + 1) * X_HEAD_DIM]))
    o_ref[0] = h1 + _dot(jnp.concatenate(outs, axis=1), wxo_ref[...])


def _xattn_prompt(x3, yp, yn, w_out, g_x, w_xq, mem_kv, w_xo, tm):
    b, t, d = x3.shape
    m = mem_kv.shape[1]
    full = lambda a: pl.BlockSpec(a.shape, lambda i, j: (0,) * a.ndim)
    return pl.pallas_call(
        _xattn_prompt_kernel,
        grid=(b, t // tm),
        in_specs=[pl.BlockSpec((1, tm, d), lambda i, j: (i, j, 0)),
                  pl.BlockSpec((1, tm, POOL_WIDTH), lambda i, j: (i, j, 0)),
                  pl.BlockSpec((1, tm, NSA_WIDTH), lambda i, j: (i, j, 0)),
                  full(w_out), full(g_x), full(w_xq),
                  pl.BlockSpec((1, m, mem_kv.shape[2]), lambda i, j: (i, 0, 0)),
                  full(w_xo)],
        out_specs=pl.BlockSpec((1, tm, d), lambda i, j: (i, j, 0)),
        out_shape=jax.ShapeDtypeStruct((b, t, d), F32),
        compiler_params=_params(("arbitrary", "arbitrary")),
        name="xattn_prompt",
    )(x3, yp, yn, w_out, g_x, w_xq, mem_kv, w_xo)


def _ffn_core(j, nj, h_ref, gf_ref, wg_ref, wv_ref, cw_ref, cb_ref, wd_ref, gfin_ref, y_ref,
              xn_ref, acc_ref, prev_rows):
    @pl.when(j == 0)
    def _():
        xn_ref[...] = _rms(h_ref[...], gf_ref[...]).astype(BF16)
        acc_ref[...] = jnp.zeros(acc_ref.shape, F32)

    xn = xn_ref[...]
    up_g = jnp.dot(xn, wg_ref[...], preferred_element_type=F32)
    up_v = jnp.dot(xn, wv_ref[...], preferred_element_type=F32)
    g1, g2, v1, v2 = prev_rows(up_g, up_v)
    cg = cb_ref[0:1, :] + cw_ref[0, 0:1, :] * g2 + cw_ref[1, 0:1, :] * g1 + cw_ref[2, 0:1, :] * up_g
    cv = cb_ref[1:2, :] + cw_ref[0, 1:2, :] * v2 + cw_ref[1, 1:2, :] * v1 + cw_ref[2, 1:2, :] * up_v
    act = cg / (1.0 + jnp.exp(-cg)) * cv
    acc_ref[...] += _dot(act, wd_ref[...])

    @pl.when(j == nj - 1)
    def _():
        y_ref[...] = _rms(h_ref[...] + acc_ref[...], gfin_ref[...])

    return up_g, up_v


def _ffn_prompt_kernel(h_ref, gf_ref, wg_ref, wv_ref, cw_ref, cb_ref, wd_ref, gfin_ref,
                       y_ref, st_ref, xn_ref, acc_ref, carry_ref, *, tiles_per_seq, nj, tm):
    r = pl.program_id(0)
    j = pl.program_id(1)
    row = lax.broadcasted_iota(I32, (tm, 1), 0)

    @pl.when((r % tiles_per_seq) == 0)
    def _():
        carry_ref[j] = jnp.zeros(carry_ref.shape[1:], F32)

    def prev_rows(up_g, up_v):
        res = []
        for half, up in enumerate((up_g, up_v)):
            prev = carry_ref[j, half]
            m1 = jnp.where(row == 0, prev[7:8], pltpu.roll(up, 1, axis=0))
            m2 = jnp.where(row == 0, prev[6:7], jnp.where(row == 1, prev[7:8], pltpu.roll(up, 2, axis=0)))
            res += [m1, m2]
        return res

    up_g, up_v = _ffn_core(j, nj, h_ref, gf_ref, wg_ref, wv_ref, cw_ref, cb_ref, wd_ref, gfin_ref,
                           y_ref, xn_ref, acc_ref, prev_rows)
    for half, up in enumerate((up_g, up_v)):
        carry_ref[j, half] = up[tm - 8:tm]
        st_ref[0, half] = up[tm - 8:tm]


def _ffn_specs(d, tf, nj):
    return [pl.BlockSpec((1, d), lambda r, j: (0, 0)),
            pl.BlockSpec((d, tf), lambda r, j: (0, j)),
            pl.BlockSpec((d, tf), lambda r, j: (0, nj + j)),
            pl.BlockSpec((3, 2, tf), lambda r, j: (0, 0, j)),
            pl.BlockSpec((2, tf), lambda r, j: (0, j)),
            pl.BlockSpec((tf, d), lambda r, j: (j, 0)),
            pl.BlockSpec((1, d), lambda r, j: (0, 0))]


def _ffn_prompt(h2, g_ffn, w_up, cw3, cb2, w_down, g_final, batch, tm, tf):
    n, d = h2.shape
    nj = D_FF // tf
    tps = n // batch // tm
    return pl.pallas_call(
        functools.partial(_ffn_prompt_kernel, tiles_per_seq=tps, nj=nj, tm=tm),
        grid=(n // tm, nj),
        in_specs=[pl.BlockSpec((tm, d), lambda r, j: (r, 0))] + _ffn_specs(d, tf, nj),
        out_specs=[pl.BlockSpec((tm, d), lambda r, j: (r, 0)),
                   pl.BlockSpec((1, 2, 8, tf), lambda r, j: (r, 0, 0, j))],
        out_shape=[jax.ShapeDtypeStruct((n, d), F32),
                   jax.ShapeDtypeStruct((n // tm, 2, 8, D_FF), F32)],
        scratch_shapes=[pltpu.VMEM((tm, d), BF16), pltpu.VMEM((tm, d), F32),
                        pltpu.VMEM((nj, 2, 8, tf), F32)],
        compiler_params=_params(("arbitrary", "arbitrary")),
        name="ffn_prompt",
    )(h2, g_ffn, w_up, w_up, cw3, cb2, w_down, g_final)


def _ffn_sample_kernel(h_ref, gf_ref, wg_ref, wv_ref, cw_ref, cb_ref, wd_ref, gfin_ref, sg_ref, sv_ref,
                       y_ref, up_ref, xn_ref, acc_ref, *, nj):
    j = pl.program_id(1)
    prev_rows = lambda up_g, up_v: (sg_ref[1], sg_ref[0], sv_ref[1], sv_ref[0])
    up_g, up_v = _ffn_core(j, nj, h_ref, gf_ref, wg_ref, wv_ref, cw_ref, cb_ref, wd_ref, gfin_ref,
                           y_ref, xn_ref, acc_ref, prev_rows)
    up_ref[0] = up_g
    up_ref[1] = up_v


def _ffn_sample(h2, g_ffn, w_up, cw3, cb2, w_down, g_final, st_t, tf):
    n, d = h2.shape
    nj = D_FF // tf
    return pl.pallas_call(
        functools.partial(_ffn_sample_kernel, nj=nj),
        grid=(1, nj),
        in_specs=[pl.BlockSpec((n, d), lambda r, j: (0, 0))] + _ffn_specs(d, tf, nj)
        + [pl.BlockSpec((2, n, tf), lambda r, j: (0, 0, j)),
           pl.BlockSpec((2, n, tf), lambda r, j: (0, 0, nj + j))],
        out_specs=[pl.BlockSpec((n, d), lambda r, j: (0, 0)),
                   pl.BlockSpec((2, n, tf), lambda r, j: (0, 0, j))],
        out_shape=[jax.ShapeDtypeStruct((n, d), F32),
                   jax.ShapeDtypeStruct((2, n, D_FF), F32)],
        scratch_shapes=[pltpu.VMEM((n, d), BF16), pltpu.VMEM((n, d), F32)],
        compiler_params=_params(("arbitrary", "arbitrary")),
        name="ffn_sample",
    )(h2, g_ffn, w_up, w_up, cw3, cb2, w_down, g_final, st_t, st_t)


def _head_rows(qrow):
    rowi = lax.broadcasted_iota(I32, (N_HEADS, LANE), 0)
    lane = lax.broadcasted_iota(I32, (N_HEADS, LANE), 1)
    x = jnp.zeros((N_HEADS, LANE), F32)
    for c in range(NSA_WIDTH // LANE):
        x = jnp.where((rowi >> 1) == c, jnp.broadcast_to(qrow[:, c * LANE:(c + 1) * LANE], (N_HEADS, LANE)), x)
    x = jnp.where((rowi & 1) != (rowi >> 2), pltpu.roll(x, HEAD_DIM, axis=1), x)
    return jnp.where((lane >> 6) == (rowi >> 2), x, 0.0)


def _page_copy(cache_ref, raw_ref, sem_ref, phys, p, slot):
    return pltpu.make_async_copy(cache_ref.at[phys, pl.ds(0, KV_W), :], raw_ref.at[slot, p], sem_ref.at[slot, p])


def _nsa_sample_cmp_kernel(pt_ref, q_ref, cache_ref, wbig_ref, pe_ref, wpe_ref, oc_ref, imp_ref,
                           raw_ref, tok_ref, sem_ref, *, n_pages, n_batch, page):
    b = pl.program_id(0)
    slot = b % 2
    n_tok = n_pages * page
    n_chunk = n_tok // CMP_STRIDE

    def start_all(bb, sl):
        def body(p, _):
            _page_copy(cache_ref, raw_ref, sem_ref, pt_ref[bb * n_pages + p], p, sl).start()
            return 0
        lax.fori_loop(0, n_pages, body, 0)

    @pl.when(b == 0)
    def _():
        start_all(0, 0)

    @pl.when(b + 1 < n_batch)
    def _():
        start_all(b + 1, 1 - slot)

    chunks_per_page = page // CMP_STRIDE

    def to_token_major(pg, _):
        for pp in range(PAGE_UNROLL):
            _page_copy(cache_ref, raw_ref, sem_ref, 0, pg * PAGE_UNROLL + pp, slot).wait()
        for pp in range(PAGE_UNROLL):
            p = pg * PAGE_UNROLL + pp
            xt = raw_ref[slot, p]
            for half in range(2):
                x = xt[half * LANE:(half + 1) * LANE].T
                for nl in range(chunks_per_page):
                    r0 = pl.multiple_of((p * chunks_per_page + nl) * CHUNK_PITCH, 8)
                    tok_ref[half, pl.ds(r0, CMP_STRIDE), :] = x[nl * CMP_STRIDE:(nl + 1) * CMP_STRIDE]
        return 0
    lax.fori_loop(0, n_pages // PAGE_UNROLL, to_token_major, 0)

    load = lambda c, s: tok_ref[c, pl.ds(s, n_chunk, stride=CHUNK_PITCH), :]
    kvc = _compress_rows(load, wbig_ref, pe_ref, wpe_ref, n_chunk)
    q8 = _head_rows(q_ref[0] * (HEAD_DIM ** -0.5))
    s = _dot_nt(q8, kvc[:, 0:LANE])
    ncol = lax.broadcasted_iota(I32, (1, n_chunk), 1)
    p = _masked_softmax(s, (ncol * CMP_STRIDE + CMP_LEN - 1) <= n_tok)
    oc_ref[0] = _dot(p, kvc[:, LANE:2 * LANE])
    rowi = lax.broadcasted_iota(I32, (N_HEADS, n_chunk), 0)
    psum = jnp.where(rowi == 0, jnp.sum(p[0:GROUP], axis=0, keepdims=True),
                     jnp.where(rowi == 1, jnp.sum(p[GROUP:2 * GROUP], axis=0, keepdims=True), 0.0))
    ns_pad = imp_ref.shape[2]
    on = lax.broadcasted_iota(I32, (n_chunk, ns_pad), 0)
    oj = lax.broadcasted_iota(I32, (n_chunk, ns_pad), 1)
    ovl = jnp.where((on * CMP_STRIDE + CMP_LEN - 1 >= oj * SEL_BLOCK)
                    & (on * CMP_STRIDE <= oj * SEL_BLOCK + SEL_BLOCK - 1), 1.0, 0.0).astype(BF16)
    imp_ref[0] = sum(jnp.dot(part, ovl, preferred_element_type=F32) for part in _split3(psum))


def _nsa_sample_cmp(pt_flat, q3, cache3, wbig, pe8, wpe, n_pages, ns_pad):
    nb = q3.shape[0]
    page = cache3.shape[2]
    assert page == LANE
    n_tok = n_pages * page
    gs = pltpu.PrefetchScalarGridSpec(
        num_scalar_prefetch=1, grid=(nb,),
        in_specs=[pl.BlockSpec((1, 1, NSA_WIDTH), lambda i, pt: (i, 0, 0)),
                  pl.BlockSpec(memory_space=pl.ANY),
                  pl.BlockSpec(wbig.shape, lambda i, pt: (0, 0, 0)),
                  pl.BlockSpec(pe8.shape, lambda i, pt: (0, 0)),
                  pl.BlockSpec(wpe.shape, lambda i, pt: (0, 0))],
        out_specs=[pl.BlockSpec((1, N_HEADS, LANE), lambda i, pt: (i, 0, 0)),
                   pl.BlockSpec((1, N_HEADS, ns_pad), lambda i, pt: (i, 0, 0))],
        scratch_shapes=[pltpu.VMEM((2, n_pages, KV_W, page), F32), pltpu.VMEM((2, n_tok // CMP_STRIDE * CHUNK_PITCH, LANE), F32),
                        pltpu.SemaphoreType.DMA((2, n_pages))])
    return pl.pallas_call(
        functools.partial(_nsa_sample_cmp_kernel, n_pages=n_pages, n_batch=nb, page=page),
        grid_spec=gs,
        out_shape=[jax.ShapeDtypeStruct((nb, N_HEADS, LANE), F32),
                   jax.ShapeDtypeStruct((nb, N_HEADS, ns_pad), F32)],
        compiler_params=_params(("arbitrary",)),
        name="nsa_sample_cmp",
    )(pt_flat, q3, cache3, wbig, pe8, wpe)


def _topk_sample_kernel(imp_ref, idx_ref, *, jq, k_past):
    x = imp_ref[...]
    col = lax.broadcasted_iota(I32, x.shape, 1)
    x = jnp.where((col == 0) | (col == jq - 1), jnp.inf, jnp.where(col >= jq, -jnp.inf, x))
    lane = lax.broadcasted_iota(I32, idx_ref.shape, 1)
    out = jnp.zeros(idx_ref.shape, I32)
    big = jnp.int32(1 << 30)
    for k in range(k_past):
        m = jnp.max(x, axis=-1, keepdims=True)
        idx = jnp.min(jnp.where(x == m, col, big), axis=-1, keepdims=True)
        out = jnp.where(lane == k, idx, out)
        x = jnp.where(col == idx, -jnp.inf, x)
    idx_ref[...] = out


def _topk_sample(imp2, jq, k_past):
    n, w = imp2.shape
    return pl.pallas_call(
        functools.partial(_topk_sample_kernel, jq=jq, k_past=k_past),
        grid=(1,),
        in_specs=[pl.BlockSpec((n, w), lambda i: (0, 0))],
        out_specs=pl.BlockSpec((n, LANE), lambda i: (0, 0)),
        out_shape=jax.ShapeDtypeStruct((n, LANE), I32),
        compiler_params=_params(("arbitrary",)),
        name="topk_sample",
    )(imp2)


def _blk_copy(cache_ref, buf_ref, sem_ref, phys, kv, g, r, slot):
    row0 = KV_W + kv * LANE + g * HEAD_DIM
    return pltpu.make_async_copy(cache_ref.at[phys, pl.ds(row0, HEAD_DIM), :],
                                 buf_ref.at[slot, kv, pl.ds(g * HEAD_DIM, HEAD_DIM), pl.ds(r * LANE, LANE)],
                                 sem_ref.at[slot])


def _nsa_sample_sel_kernel(idx_ref, pt_ref, q_ref, gt_ref, oc_ref, win_ref, kvn4_ref, kvnw_ref, cache_ref,
                           o_ref, wout_ref, buf_ref, sem_ref, *, n_pages, n_batch, page, k_past):
    b = pl.program_id(0)
    slot = b % 2
    per_page = page // SEL_BLOCK
    nl = k_past * page

    def for_blocks(bb, fn):
        def body(r, _):
            for g in range(KV_HEADS):
                j = idx_ref[(bb * KV_HEADS + g) * k_past + r]
                phys = pt_ref[bb * n_pages + j // per_page]
                for kv in range(2):
                    fn(phys, kv, g, r)
            return 0
        lax.fori_loop(0, k_past, body, 0)

    def start_all(bb, sl):
        for_blocks(bb, lambda phys, kv, g, r: _blk_copy(cache_ref, buf_ref, sem_ref, phys, kv, g, r, sl).start(priority=kv))

    @pl.when(b == 0)
    def _():
        start_all(0, 0)

    @pl.when(b + 1 < n_batch)
    def _():
        start_all(b + 1, 1 - slot)

    def wait_body(r, _):
        for g in range(KV_HEADS):
            for kv in range(2):
                _blk_copy(cache_ref, buf_ref, sem_ref, 0, kv, g, r, slot).wait()
        return 0
    lax.fori_loop(0, k_past, wait_body, 0)

    q8 = _head_rows(q_ref[0] * (HEAD_DIM ** -0.5))
    rowi = lax.broadcasted_iota(I32, (N_HEADS, 1), 0)
    lane = lax.broadcasted_iota(I32, (N_HEADS, LANE), 1)
    own = (lane >> 6) == (rowi >> 2)
    mine = jnp.where(lax.broadcasted_iota(I32, (N_HEADS, n_batch), 1) == b, 0.0, -jnp.inf)

    def attend(k_t, v_t, bias):
        s = _dot(q8, k_t) + bias
        e = jnp.exp(s - jnp.max(s, axis=-1, keepdims=True))
        return _dot_nt(e, v_t) / jnp.sum(e, axis=-1, keepdims=True)

    lane_s = lax.broadcasted_iota(I32, (1, nl), 1)
    halves = []
    for g in range(KV_HEADS):
        hv = jnp.zeros((1, nl), I32)
        for r in range(k_past):
            hv = jnp.where((lane_s // page) == r, idx_ref[(b * KV_HEADS + g) * k_past + r] % per_page, hv)
        halves.append(hv)
    vis = ((lane_s % page) // SEL_BLOCK) == jnp.where(rowi < GROUP, halves[0], halves[1])
    o_s = attend(jnp.concatenate([buf_ref[slot, 0], kvn4_ref[0, KV_W:KV_W + LANE]], axis=1),
                 jnp.concatenate([buf_ref[slot, 1], kvn4_ref[0, KV_W + LANE:2 * KV_W]], axis=1),
                 jnp.concatenate([jnp.where(vis, 0.0, -jnp.inf), mine], axis=1))

    cw = win_ref[0]
    wlen = cw.shape[1]
    o_w = attend(jnp.concatenate([cw[0:LANE], kvnw_ref[0, 0:LANE]], axis=1),
                 jnp.concatenate([cw[LANE:2 * LANE], kvnw_ref[0, LANE:2 * LANE]], axis=1),
                 jnp.concatenate([jnp.zeros((N_HEADS, wlen), F32), mine], axis=1))
    lane_n = lax.broadcasted_iota(I32, (KV_W, n_batch), 1)
    new_col = jnp.sum(jnp.where(lane_n == b, kvnw_ref[0], 0.0), axis=1, keepdims=True)
    lane_w = lax.broadcasted_iota(I32, (KV_W, wlen), 1)
    wout_ref[0] = jnp.where(lane_w == wlen - 1, new_col, pltpu.roll(cw, wlen - 1, axis=1))

    gt = jnp.broadcast_to(gt_ref[0], (N_HEADS, GATE_PAD))
    gate = lambda br: jnp.sum(jnp.where(lane == rowi * N_BRANCH + br, gt, 0.0), axis=-1, keepdims=True)
    o = gate(0) * oc_ref[0] + gate(1) * o_s + gate(2) * o_w
    o_ref[0] = jnp.where(own, o, 0.0)


def _nsa_sample_sel(idx_flat, pt_flat, q3, gt3, oc, win_t, kvn4_t, kvnw_t, cache3, n_pages, k_past):
    nb = q3.shape[0]
    page = cache3.shape[2]
    wlen = win_t.shape[2]
    assert page == LANE
    per = lambda *blk: pl.BlockSpec((1,) + blk, lambda i, a, c: (i, 0, 0))
    whole = lambda arr: pl.BlockSpec(arr.shape, lambda i, a, c: (0, 0, 0))
    gs = pltpu.PrefetchScalarGridSpec(
        num_scalar_prefetch=2, grid=(nb,),
        in_specs=[per(1, NSA_WIDTH), per(1, GATE_PAD), per(N_HEADS, LANE), per(KV_W, wlen),
                  whole(kvn4_t), whole(kvnw_t), pl.BlockSpec(memory_space=pl.ANY)],
        out_specs=[per(N_HEADS, LANE), per(KV_W, wlen)],
        scratch_shapes=[pltpu.VMEM((2, 2, LANE, k_past * page), F32), pltpu.SemaphoreType.DMA((2,))])
    return pl.pallas_call(
        functools.partial(_nsa_sample_sel_kernel, n_pages=n_pages, n_batch=nb, page=page, k_past=k_past),
        grid_spec=gs,
        out_shape=[jax.ShapeDtypeStruct((nb, N_HEADS, LANE), F32),
                   jax.ShapeDtypeStruct((nb, KV_W, wlen), F32)],
        compiler_params=_params(("arbitrary",)),
        name="nsa_sample_sel",
    )(idx_flat, pt_flat, q3, gt3, oc, win_t, kvn4_t, kvnw_t, cache3)


def _mix_sample_kernel(x_ref, yp_ref, yn_ref, wo_ref, g_ref, wq_ref, h1_ref, qx_ref):
    y = jnp.concatenate([yp_ref[...], yn_ref[...]], axis=1)
    h1 = x_ref[...] + _dot(y, wo_ref[...])
    h1_ref[...] = h1
    qx_ref[...] = _dot(_rms(h1, g_ref[...]), wq_ref[...]) * (X_HEAD_DIM ** -0.5)


def _mix_sample(x, yp, yn, w_out, g_x, w_xq):
    n, d = x.shape
    full = lambda a: pl.BlockSpec(a.shape, lambda i: (0,) * a.ndim)
    args = (x, yp, yn, w_out, g_x, w_xq)
    return pl.pallas_call(
        _mix_sample_kernel, grid=(1,),
        in_specs=[full(a) for a in args],
        out_specs=[pl.BlockSpec((n, d), lambda i: (0, 0))] * 2,
        out_shape=[jax.ShapeDtypeStruct((n, d), F32)] * 2,
        compiler_params=_params(("arbitrary",)),
        name="mix_sample",
    )(*args)


def _xattn_sample_kernel(qx_ref, mem_ref, o_ref):
    n_mem = mem_ref.shape[1]
    n = n_mem * 8
    q = qx_ref[0]
    order = [(half, h) for half in range(X_HEAD_DIM // LANE) for h in range(X_HEADS)]
    col = lambda half, h: slice(h * X_HEAD_DIM + half * LANE, h * X_HEAD_DIM + (half + 1) * LANE)
    qt = jnp.concatenate([q[:, col(half, h)] for half, h in order], axis=0)
    x = mem_ref[0]
    a = _dot_nt(qt, x[:, 0].reshape(n, LANE))
    rowi = lax.broadcasted_iota(I32, (8, n), 0)
    lane = lax.broadcasted_iota(I32, (8, n), 1)
    own = (lane & 7) == rowi
    a = jnp.where(own, a, 0.0)
    other = pltpu.roll(a, X_HEADS, axis=0)
    s = a + jnp.where(rowi < X_HEADS, pltpu.roll(other, n - X_HEADS, axis=1), pltpu.roll(other, X_HEADS, axis=1))
    s = jnp.where(own, s, -jnp.inf)
    e = jnp.exp(s - jnp.max(s, axis=-1, keepdims=True))
    p = e / jnp.sum(e, axis=-1, keepdims=True)
    o8 = _dot(p, x[:, 1].reshape(n, LANE))
    o_ref[0] = jnp.concatenate([o8[half * X_HEADS + h:half * X_HEADS + h + 1]
                                for h in range(X_HEADS) for half in range(X_HEAD_DIM // LANE)], axis=1)


def _xattn_sample(qx3, mem5):
    nb, _, xw = qx3.shape
    return pl.pallas_call(
        _xattn_sample_kernel, grid=(nb,),
        in_specs=[pl.BlockSpec((1, 1, xw), lambda i: (i, 0, 0)),
                  pl.BlockSpec((1,) + mem5.shape[1:], lambda i: (i, 0, 0, 0, 0))],
        out_specs=pl.BlockSpec((1, 1, xw), lambda i: (i, 0, 0)),
        out_shape=jax.ShapeDtypeStruct((nb, 1, xw), F32),
        compiler_params=_params(("arbitrary",)),
        name="xattn_sample",
    )(qx3, mem5)


def _resid_proj_kernel(h_ref, o_ref, w_ref, y_ref):
    y_ref[...] = h_ref[...] + _dot(o_ref[...], w_ref[...])


def _resid_proj(h, o, w):
    n, d = h.shape
    full = lambda a: pl.BlockSpec(a.shape, lambda i: (0,) * a.ndim)
    return pl.pallas_call(
        _resid_proj_kernel, grid=(1,),
        in_specs=[full(h), full(o), full(w)],
        out_specs=pl.BlockSpec((n, d), lambda i: (0, 0)),
        out_shape=jax.ShapeDtypeStruct((n, d), F32),
        compiler_params=_params(("arbitrary",)),
        name="resid_proj",
    )(h, o, w)


def _prep_weights(g_mix, w_in, w_pool, pool_scale, w_cmp, pe_cmp, w_out, g_xattn, g_mem, w_xq, w_xkv,
                  w_xo, g_ffn, w_up, conv_w, conv_b, w_down, g_final):
    o_kv = POOL_WIDTH + NSA_WIDTH
    o_gate = o_kv + N_BRANCH * KV_W
    w_tok = jnp.concatenate([w_in[:, :o_kv + KV_W + LANE], jnp.pad(w_in[:, o_gate:], ((0, 0), (0, GATE_PAD - N_GATE)))],
                            axis=1)
    r = CMP_LEN // CMP_STRIDE
    w5 = w_cmp.reshape(2, r, CMP_STRIDE, HEAD_DIM, HEAD_DIM)
    eye = jnp.eye(2, dtype=F32)
    wbig = jnp.einsum('cisde,gG->csgdiGe', w5, eye).reshape(2, CMP_STRIDE * LANE, r * LANE)
    wpe = jnp.einsum('cke,cC->ckCe', w_cmp.reshape(2, CMP_LEN * HEAD_DIM, HEAD_DIM), eye)
    wpe = jnp.broadcast_to(wpe[:, :, :, None, :], (2, CMP_LEN * HEAD_DIM, 2, KV_HEADS, HEAD_DIM))
    row = lambda v: v.reshape(1, -1)
    return dict(
        g_mix=row(g_mix), w_tok=w_tok.astype(BF16), w_feat=w_in[:, o_kv:o_gate].T.astype(BF16),
        w_pool=w_pool.astype(BF16), pool_scale=row(pool_scale),
        wbig=wbig.astype(BF16), wpe=wpe.reshape(2 * CMP_LEN * HEAD_DIM, KV_W).astype(BF16),
        pe8=jnp.broadcast_to(pe_cmp.reshape(1, -1), (8, 2 * CMP_LEN * HEAD_DIM)),
        w_out=w_out.astype(BF16), g_xattn=row(g_xattn), g_mem=row(g_mem), w_xq=w_xq.astype(BF16),
        w_xkv=w_xkv.astype(BF16), w_xo=w_xo.astype(BF16), g_ffn=row(g_ffn), w_up=w_up.astype(BF16),
        cw3=conv_w.reshape(3, 2, D_FF), cb2=conv_b.reshape(2, D_FF), w_down=w_down.astype(BF16),
        g_final=row(g_final))


def _token_major(a_t, lead):
    b, _, t = a_t.shape
    nd = len(lead)
    return jnp.transpose(a_t.reshape((b,) + lead + (t,)), (0, nd + 1) + tuple(range(1, nd + 1)))


def _feature_major(a, nfeat):
    nd = a.ndim
    return jnp.transpose(a, (0,) + tuple(range(2, nd)) + (1,)).reshape(a.shape[0], nfeat, a.shape[1])


def _prompt_layer(x, mem, w, *, tm=512, tm_x=1024, tf=D_FF, kc=(2048, 1024, 512)):
    b, t, d = x.shape
    n = b * t
    u, q, kvc_tok, ks_tok, gt, kv4t, kvwt = _mix_in(x, w['g_mix'], w['w_tok'], w['w_feat'], tm, "mix_in_prompt")
    r3 = lambda a: a.reshape(b, t, a.shape[-1])
    u, q, kvc_tok, ks_tok, gt = map(r3, (u, q, kvc_tok, ks_tok, gt))
    y_pool = _pool_prompt(u, w['w_pool'], w['pool_scale'], tm)
    kvc = _compress_prompt(kvc_tok, w['wbig'], w['pe8'], w['wpe'])
    y_nsa = _nsa_prompt(q, gt, kvc, ks_tok, kv4t, kvwt, kc)
    m = mem.shape[1]
    (mem_kv,) = _rms_proj(mem.reshape(b * m, d), w['g_mem'], w['w_xkv'], (2 * X_HEADS * X_HEAD_DIM,), m,
                          name="mem_project")
    mem_kv = mem_kv.reshape(b, m, 2 * X_HEADS * X_HEAD_DIM)
    h2 = _xattn_prompt(x, y_pool, y_nsa, w['w_out'], w['g_xattn'], w['w_xq'], mem_kv, w['w_xo'], tm_x)
    y, st = _ffn_prompt(h2.reshape(n, d), w['g_ffn'], w['w_up'], w['cw3'], w['cb2'], w['w_down'],
                        w['g_final'], b, tm, tf)
    wlen = min(WINDOW, t)
    return (y.reshape(b, t, d),
            _token_major(kv4t, (4, KV_HEADS, HEAD_DIM)),
            _token_major(kvwt[:, :, t - wlen:], (2, KV_HEADS, HEAD_DIM)),
            u[:, t - POOL_STATE:],
            jnp.swapaxes(st[t // tm - 1::t // tm, :, 6:8], 1, 2).reshape(b, 2, 2 * D_FF),
            mem_kv.reshape(b, m, 2, X_HEADS, X_HEAD_DIM))


def _sample_layer(x, cache_kv, page_table, cache_win, state_pool, state_ffn, cache_mem, w, *, tf=1408):
    nb, _, d = x.shape
    n_phys, page = cache_kv.shape[:2]
    n_pages = page_table.shape[1]
    past = n_pages * page
    u, q, _, _, gt, kv4t, kvwt = _mix_in(x.reshape(1, nb, d), w['g_mix'], w['w_tok'], w['w_feat'], nb, "mix_in_sample")
    y_pool = _pool_sample(jnp.swapaxes(state_pool, 0, 1), u, w['w_pool'], w['pool_scale'], past)
    cache3 = _feature_major(cache_kv, 2 * KV_W)
    pt_flat = page_table.reshape(-1)
    jq = past // SEL_BLOCK
    ns_pad = -(-(jq + 1) // LANE) * LANE
    k_past = min(N_SELECT, jq + 1) - 1
    q3 = q.reshape(nb, 1, NSA_WIDTH)
    oc, imp = _nsa_sample_cmp(pt_flat, q3, cache3, w['wbig'], w['pe8'], w['wpe'], n_pages, ns_pad)
    idx = _topk_sample(imp[:, :KV_HEADS].reshape(nb * KV_HEADS, ns_pad), jq, k_past)
    idx_flat = idx[:, :k_past].reshape(-1)
    o8, win_out = _nsa_sample_sel(idx_flat, pt_flat, q3, gt.reshape(nb, 1, GATE_PAD), oc,
                                  _feature_major(cache_win, KV_W), kv4t, kvwt, cache3, n_pages, k_past)
    y_nsa = jnp.concatenate([o8[:, :GROUP, :HEAD_DIM].reshape(nb, GROUP * HEAD_DIM),
                             o8[:, GROUP:, HEAD_DIM:].reshape(nb, GROUP * HEAD_DIM)], axis=1)
    h1, qx = _mix_sample(x.reshape(nb, d), y_pool, y_nsa, w['w_out'], w['g_xattn'], w['w_xq'])
    m = cache_mem.shape[1]
    halves = X_HEAD_DIM // LANE
    mem5 = jnp.transpose(cache_mem.reshape(nb, m, 2, X_HEADS, halves, LANE), (0, 1, 2, 4, 3, 5))
    ox = _xattn_sample(qx.reshape(nb, 1, d), mem5.reshape(nb, m, 2, halves * X_HEADS, LANE))
    h2 = _resid_proj(h1, ox.reshape(nb, d), w['w_xo'])
    y, up2 = _ffn_sample(h2, w['g_ffn'], w['w_up'], w['cw3'], w['cb2'], w['w_down'], w['g_final'],
                         jnp.swapaxes(state_ffn, 0, 1), tf)
    up = jnp.swapaxes(up2, 0, 1).reshape(nb, 1, 2 * D_FF)
    return (y.reshape(nb, 1, d),
            jnp.swapaxes(_token_major(kv4t, (4, KV_HEADS, HEAD_DIM)), 0, 1),
            _token_major(win_out, (2, KV_HEADS, HEAD_DIM)),
            jnp.concatenate([state_pool[:, 1:], u[:, None, :]], axis=1),
            jnp.concatenate([state_ffn[:, 1:], up], axis=1),
            )


def kernel(x_prompt, x_sample, mem_prompt, cache_kv, page_table, cache_win, state_pool, state_ffn, cache_mem, g_mix, w_in, w_pool, pool_scale, w_cmp, pe_cmp, w_out, g_xattn, g_mem, w_xq, w_xkv, w_xo, g_ffn, w_up, conv_w, conv_b, w_down, g_final):
    assert g_mix.shape[0] == 1, "single layer"
    w = _prep_weights(g_mix[0], w_in[0], w_pool[0], pool_scale[0], w_cmp[0], pe_cmp[0], w_out[0], g_xattn[0],
                      g_mem[0], w_xq[0], w_xkv[0], w_xo[0], g_ffn[0], w_up[0], conv_w[0], conv_b[0], w_down[0],
                      g_final)
    yp, kv_p, win_p, pool_p, ffn_p, mem_p = _prompt_layer(x_prompt, mem_prompt, w)
    ys, kv_s, win_s, pool_s, ffn_s = _sample_layer(x_sample, cache_kv[0], page_table, cache_win[0],
                                                    state_pool[0], state_ffn[0], cache_mem[0], w)
    lead = lambda a: a[None]
    return (yp, ys, lead(kv_p), lead(kv_s), lead(win_p), lead(win_s), lead(pool_p), lead(pool_s),
            lead(ffn_p), lead(ffn_s), lead(mem_p))
```
